```python
import jax, jax.numpy as jnp
from jax import lax
import numpy as np

D_MODEL = 2048
BATCH = 16
SEQ = 256
DEPTH = 1
DEC_BATCH = 8
DEC_SEQ = 1024
PAST_LEN = 256

GRID_W = 64
W_CONV = D_MODEL // 2
W_RWKV = D_MODEL // 2
HEAD = 64
H_RWKV = W_RWKV // HEAD
LORA_DECAY = 96
LORA_ICLR = 96
LORA_GATE = 256
N_EXPERTS = 64
TOP_K = 8
N_GROUPS = 8
TOPK_GROUPS = 4
D_EXPERT = 512
D_SHARED = 512
ROUTED_SCALE = 2.5
MOE_BLOCK = 128
NORM_EPS = 1e-6
GN_EPS = 64e-5
PROJ_SIZES = (W_CONV, W_CONV, W_CONV, W_RWKV, W_RWKV, W_RWKV, LORA_DECAY, LORA_DECAY, LORA_ICLR, LORA_ICLR, LORA_GATE, D_MODEL, D_MODEL)
D_PROJ = 3 * W_CONV + 3 * W_RWKV + 2 * LORA_DECAY + 2 * LORA_ICLR + LORA_GATE + 2 * D_MODEL

kernel_name = "bidir_rwkv7_shortconv_moe_flow_block"


def rmsnorm(x, g):
    xf = x.astype(jnp.float32)
    y = xf * lax.rsqrt(jnp.mean(xf * xf, axis=-1, keepdims=True) + NORM_EPS)
    return (y * g.astype(jnp.float32)).astype(x.dtype)


def conv3_centred(u, w):
    up = jnp.pad(u, [(0, 0)] * (u.ndim - 2) + [(1, 1), (0, 0)])
    return w[0] * up[..., :-2, :] + w[1] * up[..., 1:-1, :] + w[2] * up[..., 2:, :]


def heads(t):
    return t.reshape(t.shape[:-1] + (H_RWKV, HEAD))


def wkv7_scan(s0, r, w, k, v, kk, a, reverse):
    def step(s, inp):
        r_t, w_t, k_t, v_t, kk_t, a_t = inp
        sa = jnp.einsum('bhvk,bhk->bhv', s, kk_t)
        s = s * w_t[:, :, None, :] - sa[..., None] * (kk_t * a_t)[:, :, None, :] + v_t[..., None] * k_t[:, :, None, :]
        return s, jnp.einsum('bhvk,bhk->bhv', s, r_t)
    xs = tuple(jnp.swapaxes(t, 0, 1) for t in (r, w, k, v, kk, a))
    s_fin, ys = lax.scan(step, s0, xs, reverse=reverse)
    return s_fin, jnp.swapaxes(ys, 0, 1)


def token_mixer(h, p, s0, grid):
    f32 = jnp.float32
    bsz, seq_len, _ = h.shape
    proj = jnp.einsum('bld,de->ble', h, p['w_in'])
    idx = [int(i) for i in np.cumsum(PROJ_SIZES)[:-1]]
    cb, cc, cx, r, k, v, dl_f, dl_b, al_f, al_b, gl, ga, gb = jnp.split(proj, idx, axis=-1)
    u = cc * cx
    if grid:
        rows = seq_len // GRID_W
        u = conv3_centred(u.reshape(bsz, rows, GRID_W, W_CONV), p['conv_w']).reshape(bsz, seq_len, W_CONV)
    else:
        u = conv3_centred(u, p['conv_w'])
    y_a = jnp.einsum('blc,cd->bld', cb * u, p['w_out_conv'])
    rh = heads(r.astype(f32))
    vh = heads(v.astype(f32))
    kf = k.astype(f32)
    kk = heads(kf * p['k_k'])
    kk = kk * lax.rsqrt(jnp.sum(kk * kk, axis=-1, keepdims=True) + 1e-12)
    y_sum = jnp.zeros_like(rh)
    bonus = jnp.zeros_like(rh)
    s_out = []
    for d, (dl, al) in enumerate(((dl_f, al_f), (dl_b, al_b))):
        wl = p['decay_w0'][d] + jnp.tanh(dl.astype(f32)) @ p['decay_w2'][d]
        decay = jnp.exp(-jnp.exp(-jax.nn.softplus(-wl) - 0.5))
        a = jax.nn.sigmoid(p['iclr_a0'][d] + al.astype(f32) @ p['iclr_a2'][d])
        kdh = heads(kf * (1.0 + (a - 1.0) * p['k_a']))
        s_fin, y = wkv7_scan(s0[d].astype(f32), rh, heads(decay), kdh, vh, kk, heads(a), reverse=(d == 1))
        y_sum = y_sum + y
        bonus = bonus + jnp.sum(rh * kdh * p['r_k'], axis=-1, keepdims=True) * vh
        s_out.append(s_fin)
    mu = jnp.mean(y_sum, axis=-1, keepdims=True)
    var = jnp.mean(jnp.square(y_sum - mu), axis=-1, keepdims=True)
    yn = ((y_sum - mu) * lax.rsqrt(var + GN_EPS)).reshape(bsz, seq_len, W_RWKV) * p['lnx_w'] + p['lnx_b']
    g = jax.nn.sigmoid(gl) @ p['gate_g2']
    yb = ((yn + bonus.reshape(bsz, seq_len, W_RWKV)) * g).astype(h.dtype)
    y_b = jnp.einsum('blc,cd->bld', yb, p['w_out_rwkv'])
    merged = jax.nn.sigmoid(ga) * y_a + jax.nn.sigmoid(gb) * y_b
    out = jnp.einsum('bld,de->ble', merged, p['w_o'])
    return out, s_out[0], s_out[1]


def moe_ffn(h, p):
    f32 = jnp.float32
    n_tok, d_model = h.shape
    scores = jax.nn.sigmoid(jnp.dot(h.astype(f32), p['router_w'].astype(f32)))
    biased = scores + p['router_bias'].astype(f32)
    grp = biased.reshape(n_tok, N_GROUPS, N_EXPERTS // N_GROUPS)
    grp_score = jnp.sum(lax.top_k(grp, 2)[0], axis=-1)
    _, gidx = lax.top_k(grp_score, TOPK_GROUPS)
    gmask = jnp.sum(jax.nn.one_hot(gidx, N_GROUPS, dtype=f32), axis=1)
    emask = jnp.repeat(gmask, N_EXPERTS // N_GROUPS, axis=1)
    _, eidx = lax.top_k(jnp.where(emask > 0, biased, -jnp.inf), TOP_K)
    wsel = jnp.take_along_axis(scores, eidx, axis=1)
    wsel = wsel / jnp.sum(wsel, axis=-1, keepdims=True) * ROUTED_SCALE
    n_asg = n_tok * TOP_K
    e_flat = eidx.reshape(n_asg)
    t_flat = jnp.repeat(jnp.arange(n_tok, dtype=jnp.int32), TOP_K)
    w_flat = wsel.reshape(n_asg)
    order = jnp.argsort(e_flat)
    e_s, t_s, w_s = e_flat[order], t_flat[order], w_flat[order]
    counts = jnp.bincount(e_flat, length=N_EXPERTS)
    seg_start = jnp.cumsum(counts) - counts
    padded = (counts + MOE_BLOCK - 1) // MOE_BLOCK * MOE_BLOCK
    pad_end = jnp.cumsum(padded)
    pad_start = pad_end - padded
    pos = pad_start[e_s] + jnp.arange(n_asg) - seg_start[e_s]
    n_blocks = -(-n_asg // MOE_BLOCK) + N_EXPERTS
    n_slots = n_blocks * MOE_BLOCK
    tok_buf = jnp.full((n_slots,), n_tok, jnp.int32).at[pos].set(t_s)
    w_buf = jnp.zeros((n_slots,), f32).at[pos].set(w_s)
    blk_expert = jnp.minimum(jnp.searchsorted(pad_end, jnp.arange(n_blocks) * MOE_BLOCK, side='right'), N_EXPERTS - 1)
    h_pad = jnp.concatenate([h, jnp.zeros((1, d_model), h.dtype)], axis=0)
    exp_gate, exp_up, exp_down = p['exp_gate'], p['exp_up'], p['exp_down']

    def block(acc, inp):
        tok, wt, e = inp
        xb = h_pad[tok]
        hid = jax.nn.silu(xb @ exp_gate[e]) * (xb @ exp_up[e])
        return acc.at[tok].add((hid @ exp_down[e]) * wt[:, None]), None

    acc, _ = lax.scan(block, jnp.zeros((n_tok + 1, d_model), f32),
                      (tok_buf.reshape(n_blocks, MOE_BLOCK), w_buf.reshape(n_blocks, MOE_BLOCK), blk_expert))
    shared = (jax.nn.silu(h @ p['sh_gate']) * (h @ p['sh_up'])) @ p['sh_down']
    return (acc[:n_tok] + shared.astype(f32)).astype(h.dtype)


def trunk_layer(x, cond, p, s0, grid):
    mod = (jnp.dot(jax.nn.silu(cond), p['ada_w']) + p['ada_b'])[:, None, :]
    sh1, sc1, g1, sh2, sc2, g2 = jnp.split(mod, 6, axis=-1)
    h = rmsnorm(x, p['norm_pre_mix']) * (1.0 + sc1) + sh1
    m, s_f, s_b = token_mixer(h, p, s0, grid)
    x = x + g1 * rmsnorm(m, p['norm_post_mix'])
    h = rmsnorm(x, p['norm_pre_ffn']) * (1.0 + sc2) + sh2
    f = moe_ffn(h.reshape(-1, x.shape[-1]), p).reshape(x.shape)
    x = x + g2 * rmsnorm(f, p['norm_post_ffn'])
    return x, s_f, s_b


def setup_inputs(seed: int = 0) -> dict:
    key = jax.random.key(seed)
    ks = iter(jax.random.split(key, 40))
    f32 = jnp.float32

    def nrm(shape, scale):
        return jax.random.normal(next(ks), shape, f32) * scale

    L = DEPTH
    return {
        'x_prompt': nrm((BATCH, SEQ, D_MODEL), 1.0),
        'x_sample': nrm((DEC_BATCH, DEC_SEQ, D_MODEL), 1.0),
        'state_fwd': nrm((DEC_BATCH, DEPTH, H_RWKV, HEAD, HEAD), 1.0),
        'state_bwd': nrm((DEC_BATCH, DEPTH, H_RWKV, HEAD, HEAD), 1.0),
        'c': nrm((DEC_BATCH, D_MODEL), 1.0),
        'c_ctx': nrm((D_MODEL,), 1.0),
        'ada_w': nrm((L, D_MODEL, 6 * D_MODEL), 0.5 * D_MODEL ** -0.5),
        'ada_b': nrm((L, 6 * D_MODEL), 0.05),
        'norm_pre_mix': 1.0 + nrm((L, D_MODEL), 0.05),
        'norm_post_mix': 1.0 + nrm((L, D_MODEL), 0.05),
        'norm_pre_ffn': 1.0 + nrm((L, D_MODEL), 0.05),
        'norm_post_ffn': 1.0 + nrm((L, D_MODEL), 0.05),
        'w_in': nrm((L, D_MODEL, D_PROJ), D_MODEL ** -0.5),
        'conv_w': nrm((L, 3, W_CONV), 3 ** -0.5),
        'w_out_conv': nrm((L, W_CONV, D_MODEL), W_CONV ** -0.5),
        'decay_w0': 0.5 + nrm((L, 2, W_RWKV), 1.0),
        'decay_w2': nrm((L, 2, LORA_DECAY, W_RWKV), 0.5 * LORA_DECAY ** -0.5),
        'iclr_a0': nrm((L, 2, W_RWKV), 0.5),
        'iclr_a2': nrm((L, 2, LORA_ICLR, W_RWKV), 0.5 * LORA_ICLR ** -0.5),
        'gate_g2': nrm((L, LORA_GATE, W_RWKV), LORA_GATE ** -0.5),
        'k_k': 0.85 + nrm((L, W_RWKV), 0.05),
        'k_a': 1.0 + nrm((L, W_RWKV), 0.05),
        'r_k': nrm((L, H_RWKV, HEAD), 0.1),
        'lnx_w': 1.0 + nrm((L, W_RWKV), 0.05),
        'lnx_b': nrm((L, W_RWKV), 0.02),
        'w_out_rwkv': nrm((L, W_RWKV, D_MODEL), W_RWKV ** -0.5),
        'w_o': nrm((L, D_MODEL, D_MODEL), D_MODEL ** -0.5),
        'router_w': nrm((L, D_MODEL, N_EXPERTS), D_MODEL ** -0.5),
        'router_bias': nrm((L, N_EXPERTS), 0.01),
        'exp_gate': nrm((L, N_EXPERTS, D_MODEL, D_EXPERT), D_MODEL ** -0.5),
        'exp_up': nrm((L, N_EXPERTS, D_MODEL, D_EXPERT), D_MODEL ** -0.5),
        'exp_down': nrm((L, N_EXPERTS, D_EXPERT, D_MODEL), D_EXPERT ** -0.5),
        'sh_gate': nrm((L, D_MODEL, D_SHARED), D_MODEL ** -0.5),
        'sh_up': nrm((L, D_MODEL, D_SHARED), D_MODEL ** -0.5),
        'sh_down': nrm((L, D_SHARED, D_MODEL), D_SHARED ** -0.5),
    }


def reference(x_prompt, x_sample, state_fwd, state_bwd, c, c_ctx, ada_w, ada_b, norm_pre_mix, norm_post_mix, norm_pre_ffn, norm_post_ffn, w_in, conv_w, w_out_conv, decay_w0, decay_w2, iclr_a0, iclr_a2, gate_g2, k_k, k_a, r_k, lnx_w, lnx_b, w_out_rwkv, w_o, router_w, router_bias, exp_gate, exp_up, exp_down, sh_gate, sh_up, sh_down):
    y_p = x_prompt
    y_s = x_sample
    s_zero = jnp.zeros((x_prompt.shape[0], H_RWKV, HEAD, HEAD), jnp.float32)
    new_f, new_b = [], []
    for l in range(DEPTH):
        p = dict(ada_w=ada_w[l], ada_b=ada_b[l], norm_pre_mix=norm_pre_mix[l], norm_post_mix=norm_post_mix[l],
                 norm_pre_ffn=norm_pre_ffn[l], norm_post_ffn=norm_post_ffn[l], w_in=w_in[l], conv_w=conv_w[l],
                 w_out_conv=w_out_conv[l], decay_w0=decay_w0[l], decay_w2=decay_w2[l], iclr_a0=iclr_a0[l],
                 iclr_a2=iclr_a2[l], gate_g2=gate_g2[l], k_k=k_k[l], k_a=k_a[l], r_k=r_k[l], lnx_w=lnx_w[l],
                 lnx_b=lnx_b[l], w_out_rwkv=w_out_rwkv[l], w_o=w_o[l], router_w=router_w[l],
                 router_bias=router_bias[l], exp_gate=exp_gate[l], exp_up=exp_up[l], exp_down=exp_down[l],
                 sh_gate=sh_gate[l], sh_up=sh_up[l], sh_down=sh_down[l])
        y_p, s_f, s_b = trunk_layer(y_p, c_ctx[None, :], p, (s_zero, s_zero), False)
        new_f.append(s_f)
        new_b.append(s_b)
        y_s, _, _ = trunk_layer(y_s, c, p, (state_fwd[:, l], state_bwd[:, l]), True)
    new_state_fwd = jnp.stack(new_f, axis=1).astype(x_prompt.dtype)
    new_state_bwd = jnp.stack(new_b, axis=1).astype(x_prompt.dtype)
    return (y_p, y_s, new_state_fwd, new_state_bwd)
```

```python
import functools
import math

import jax
import jax.numpy as jnp
from jax import lax
from jax.experimental import pallas as pl
from jax.experimental.pallas import tpu as pltpu

F32 = jnp.float32
BF16 = jnp.bfloat16
I32 = jnp.int32

D_MODEL = 2048
N_PROMPT = 16 * 256
SEQ_PROMPT = 256
N_SAMPLE = 8 * 1024
SEQ_SAMPLE = 1024
N_TOK = N_PROMPT + N_SAMPLE
GRID_W = 64
W_BRANCH = 1024
HEAD = 64
N_HEADS = 16
N_COND = 16
LORA_W = 640
N_EXPERTS = 64
TOP_K = 8
N_GROUPS = 8
TOPK_GROUPS = 4
D_EXPERT = 512
ROUTED_SCALE = 2.5
NORM_EPS = 1e-6
GN_EPS = 64e-5
LANES = 128
MOE_ROWS = 256
MOE_BLOCKS = N_TOK * TOP_K // MOE_ROWS + N_EXPERTS
COMBINE_ROWS = 128
VMEM_LIMIT = 56 * 1024 * 1024


def _params(sem, vmem=VMEM_LIMIT):
    return pltpu.CompilerParams(dimension_semantics=sem, vmem_limit_bytes=vmem)


def _cond_of_tile(i, tm):
    n_p = N_PROMPT // tm
    return jnp.where(i < n_p, 0, 1 + (i - n_p) // (SEQ_SAMPLE // tm))


def _mod_spec(tm, chunk):
    return pl.BlockSpec((None, 1, D_MODEL), lambda i: (_cond_of_tile(i, tm), 0, chunk))


def _rms(x):
    return x * lax.rsqrt(jnp.mean(x * x, axis=-1, keepdims=True) + NORM_EPS)


def _ada_kernel(c_ref, w_ref, b_ref, o_ref):
    c = c_ref[...]
    s = (c * jax.nn.sigmoid(c)).astype(BF16)
    o_ref[...] = jnp.dot(s, w_ref[...].astype(BF16), preferred_element_type=F32) + b_ref[...]


def _ada_table(cond, ada_w, ada_b):
    tn = 1536
    return pl.pallas_call(
        _ada_kernel,
        grid=(6 * D_MODEL // tn,),
        in_specs=[pl.BlockSpec((N_COND, D_MODEL), lambda j: (0, 0)),
                  pl.BlockSpec((D_MODEL, tn), lambda j: (0, j)),
                  pl.BlockSpec((1, tn), lambda j: (0, j))],
        out_specs=pl.BlockSpec((N_COND, tn), lambda j: (0, j)),
        out_shape=jax.ShapeDtypeStruct((N_COND, 6 * D_MODEL), F32),
        compiler_params=_params(("arbitrary",)),
        name="ada_table",
    )(cond, ada_w, ada_b)


def _prenorm_kernel(x_ref, g_ref, sh_ref, sc_ref, o_ref):
    y = _rms(x_ref[...]) * g_ref[...]
    o_ref[...] = (y * (1.0 + sc_ref[...]) + sh_ref[...]).astype(o_ref.dtype)


def _prenorm(x, gain, mod3):
    tm = 256
    return pl.pallas_call(
        _prenorm_kernel,
        grid=(N_TOK // tm,),
        in_specs=[pl.BlockSpec((tm, D_MODEL), lambda i: (i, 0)),
                  pl.BlockSpec((1, D_MODEL), lambda i: (0, 0)),
                  _mod_spec(tm, 0), _mod_spec(tm, 1)],
        out_specs=pl.BlockSpec((tm, D_MODEL), lambda i: (i, 0)),
        out_shape=jax.ShapeDtypeStruct((N_TOK, D_MODEL), BF16),
        compiler_params=_params(("arbitrary",)),
        name="prenorm",
    )(x, gain, mod3, mod3)


def _mm_kernel(a_ref, w_ref, o_ref, *, act):
    acc = jnp.dot(a_ref[...], w_ref[...], preferred_element_type=F32)
    if act == "sigmoid":
        acc = jax.nn.sigmoid(acc)
    o_ref[...] = acc.astype(o_ref.dtype)


def _matmul(a, w, *, col0, n_cols, tn, out_dtype, act=None, tm=1024, name="matmul"):
    m, k = a.shape
    off = col0 // tn
    return pl.pallas_call(
        functools.partial(_mm_kernel, act=act),
        grid=(m // tm, n_cols // tn),
        in_specs=[pl.BlockSpec((tm, k), lambda i, j: (i, 0)),
                  pl.BlockSpec((k, tn), lambda i, j: (0, j + off))],
        out_specs=pl.BlockSpec((tm, tn), lambda i, j: (i, j)),
        out_shape=jax.ShapeDtypeStruct((m, n_cols), out_dtype),
        compiler_params=_params(("arbitrary", "arbitrary")),
        name=name,
    )(a, w)


def _convproj_kernel(h_ref, wb_ref, wc_ref, wx_ref, cw_ref, o_ref, *, tm):
    h = h_ref[...]
    cb = jnp.dot(h, wb_ref[...], preferred_element_type=F32)
    cc = jnp.dot(h, wc_ref[...], preferred_element_type=F32)
    cx = jnp.dot(h, wx_ref[...], preferred_element_type=F32)
    u = cc * cx
    seg = jnp.where(pl.program_id(0) < N_PROMPT // tm, SEQ_PROMPT, GRID_W)
    pos = lax.broadcasted_iota(I32, u.shape, 0) & (seg - 1)
    u_prev = jnp.where(pos == 0, 0.0, pltpu.roll(u, 1, 0))
    u_next = jnp.where(pos == seg - 1, 0.0, pltpu.roll(u, tm - 1, 0))
    cw = cw_ref[...]
    conv = cw[0:1, :] * u_prev + cw[1:2, :] * u + cw[2:3, :] * u_next
    o_ref[...] = (cb * conv).astype(o_ref.dtype)


def _conv_branch(h, w_in_bf, conv_w):
    tm, tn = 1024, 256
    nb = W_BRANCH // tn
    return pl.pallas_call(
        functools.partial(_convproj_kernel, tm=tm),
        grid=(N_TOK // tm, nb),
        in_specs=[pl.BlockSpec((tm, D_MODEL), lambda i, j: (i, 0)),
                  pl.BlockSpec((D_MODEL, tn), lambda i, j: (0, j)),
                  pl.BlockSpec((D_MODEL, tn), lambda i, j: (0, j + nb)),
                  pl.BlockSpec((D_MODEL, tn), lambda i, j: (0, j + 2 * nb)),
                  pl.BlockSpec((3, tn), lambda i, j: (0, j))],
        out_specs=pl.BlockSpec((tm, tn), lambda i, j: (i, j)),
        out_shape=jax.ShapeDtypeStruct((N_TOK, W_BRANCH), BF16),
        compiler_params=_params(("arbitrary", "arbitrary")),
        name="conv_branch",
    )(h, w_in_bf, w_in_bf, w_in_bf, conv_w)


def _lora_kernel(x_ref, w2_ref, w0_ref, a0_ref, wf_ref, wb_ref, af_ref, ab_ref, g_ref):
    x = x_ref[...]
    col = lax.broadcasted_iota(I32, x.shape, 1)
    act = jnp.where(col < 192, jnp.tanh(x), jnp.where(col < 384, x, jax.nn.sigmoid(x))).astype(BF16)

    def second(i):
        return jnp.dot(act, w2_ref[i], preferred_element_type=F32)

    def decay(wl):
        return jnp.exp(-jax.nn.sigmoid(wl) * math.exp(-0.5))

    wf_ref[...] = decay(w0_ref[0:1, :] + second(0))
    wb_ref[...] = decay(w0_ref[1:2, :] + second(1))
    af_ref[...] = jax.nn.sigmoid(a0_ref[0:1, :] + second(2))
    ab_ref[...] = jax.nn.sigmoid(a0_ref[1:2, :] + second(3))
    g_ref[...] = second(4)


def _lora_stage(lora, w2, w0, a0):
    tm = 512
    out = jax.ShapeDtypeStruct((N_TOK, W_BRANCH), F32)
    ospec = pl.BlockSpec((tm, W_BRANCH), lambda i: (i, 0))
    return pl.pallas_call(
        _lora_kernel,
        grid=(N_TOK // tm,),
        in_specs=[pl.BlockSpec((tm, LORA_W), lambda i: (i, 0)),
                  pl.BlockSpec((5, LORA_W, W_BRANCH), lambda i: (0, 0, 0)),
                  pl.BlockSpec((2, W_BRANCH), lambda i: (0, 0)),
                  pl.BlockSpec((2, W_BRANCH), lambda i: (0, 0))],
        out_specs=[ospec] * 5,
        out_shape=[out] * 5,
        compiler_params=_params(("arbitrary",)),
        name="lora_stage",
    )(lora, w2, w0, a0)


def _scan_kernel(r_ref, k_ref, v_ref, w_ref, a_ref, s0_ref, kk_ref, ka_ref, rk_ref,
                 y_ref, bon_ref, sf_ref, s_ref, ops_ref, *, tc):
    d = pl.program_id(0)
    c = pl.program_id(2)

    @pl.when(c == 0)
    def _():
        s_ref[...] = s0_ref[...]

    def step(i, carry):
        t = jnp.where(d == 0, i, tc - 1 - i)
        kt = k_ref[t]
        at = a_ref[t]
        vt = v_ref[t]
        kk = kt * kk_ref[...]
        kk = kk * lax.rsqrt(jnp.sum(kk * kk, axis=0, keepdims=True) + 1e-12)
        kd = kt * (1.0 + (at - 1.0) * ka_ref[...])
        ops_ref[0] = kk
        ops_ref[1] = kk * at
        ops_ref[2] = kd
        bon_ref[t] = jnp.sum(r_ref[t] * kd * rk_ref[...], axis=0, keepdims=True) * vt

        sa = jnp.zeros((HEAD, LANES), F32)
        for j in range(HEAD):
            sa = sa + s_ref[j] * ops_ref[0, pl.ds(j, 1), :]
        y = jnp.zeros((HEAD, LANES), F32)
        for j in range(HEAD):
            sn = (s_ref[j] * w_ref[t, pl.ds(j, 1), :] - sa * ops_ref[1, pl.ds(j, 1), :]
                  + vt * ops_ref[2, pl.ds(j, 1), :])
            s_ref[j] = sn
            y = y + sn * r_ref[t, pl.ds(j, 1), :]
        y_ref[t] = y
        return carry

    lax.fori_loop(0, tc, step, 0)

    @pl.when(c == pl.num_programs(2) - 1)
    def _():
        sf_ref[...] = s_ref[...]


def _scan(r, k, v, w, a, s0, kk_l, ka_l, rk_l, *, tc=32):
    g_n, seq = r.shape[0], r.shape[1]
    nc = seq // tc

    def tchunk(d, c):
        return jnp.where(d == 0, c, nc - 1 - c)

    shared = pl.BlockSpec((None, tc, HEAD, LANES), lambda d, g, c: (g, tchunk(d, c), 0, 0))
    perdir = pl.BlockSpec((None, None, tc, HEAD, LANES), lambda d, g, c: (d, g, tchunk(d, c), 0, 0))
    state = pl.BlockSpec((None, None, HEAD, HEAD, LANES), lambda d, g, c: (d, g, 0, 0, 0))
    par = pl.BlockSpec((HEAD, LANES), lambda d, g, c: (0, 0))
    seq_shape = jax.ShapeDtypeStruct((2, g_n, seq, HEAD, LANES), F32)
    return pl.pallas_call(
        functools.partial(_scan_kernel, tc=tc),
        grid=(2, g_n, nc),
        in_specs=[shared, shared, shared, perdir, perdir, state, par, par, par],
        out_specs=[perdir, perdir, state],
        out_shape=[seq_shape, seq_shape, jax.ShapeDtypeStruct((2, g_n, HEAD, HEAD, LANES), F32)],
        scratch_shapes=[pltpu.VMEM((HEAD, HEAD, LANES), F32), pltpu.VMEM((3, HEAD, LANES), F32)],
        compiler_params=_params(("arbitrary", "arbitrary", "arbitrary")),
        name="wkv7_scan",
    )(r, k, v, w, a, s0, kk_l, ka_l, rk_l)


def _scan_post_kernel(y_ref, bon_ref, lw_ref, lb_ref, o_ref):
    ys = y_ref[0] + y_ref[1]
    mu = jnp.mean(ys, axis=1, keepdims=True)
    dev = ys - mu
    var = jnp.mean(dev * dev, axis=1, keepdims=True)
    yn = dev * lax.rsqrt(var + GN_EPS) * lw_ref[...] + lb_ref[...]
    o_ref[...] = yn + bon_ref[0] + bon_ref[1]


def _scan_post(y, bon, lw_l, lb_l):
    _, g_n, seq = y.shape[:3]
    tc = 64
    both = pl.BlockSpec((2, None, tc, HEAD, LANES), lambda g, c: (0, g, c, 0, 0))
    par = pl.BlockSpec((HEAD, LANES), lambda g, c: (0, 0))
    return pl.pallas_call(
        _scan_post_kernel,
        grid=(g_n, seq // tc),
        in_specs=[both, both, par, par],
        out_specs=pl.BlockSpec((None, tc, HEAD, LANES), lambda g, c: (g, c, 0, 0)),
        out_shape=jax.ShapeDtypeStruct((g_n, seq, HEAD, LANES), F32),
        compiler_params=_params(("arbitrary", "arbitrary")),
        name="scan_post",
    )(y, bon, lw_l, lb_l)


def _to_chain_layout(x, n_batch, seq):
    g_n = n_batch // 8
    x = x.reshape(g_n, 8, seq, N_HEADS, HEAD)
    return jnp.transpose(x, (0, 2, 4, 1, 3)).reshape(g_n, seq, HEAD, LANES)


def _from_chain_layout(y):
    g_n, seq = y.shape[:2]
    y = y.reshape(g_n, seq, HEAD, 8, N_HEADS)
    return jnp.transpose(y, (0, 3, 1, 4, 2)).reshape(g_n * 8 * seq, W_BRANCH)


def _state_to_chain_layout(s):
    g_n = s.shape[0] // 8
    s = s.reshape(g_n, 8, N_HEADS, HEAD, HEAD)
    return jnp.transpose(s, (0, 4, 3, 1, 2)).reshape(g_n, HEAD, HEAD, LANES)


def _state_from_chain_layout(s):
    g_n = s.shape[0]
    s = s.reshape(g_n, HEAD, HEAD, 8, N_HEADS)
    return jnp.transpose(s, (0, 3, 4, 2, 1)).reshape(g_n * 8, N_HEADS, HEAD, HEAD)


def _head_param_to_chain_layout(p):
    return jnp.tile(p.reshape(N_HEADS, HEAD).T, (1, LANES // N_HEADS))


def _merge_kernel(z_ref, yb_ref, g_ref, ga_ref, gb_ref, wc_ref, wr_ref, o_ref):
    y_a = jnp.dot(z_ref[...], wc_ref[...], preferred_element_type=F32)
    yb = (yb_ref[...] * g_ref[...]).astype(BF16)
    y_b = jnp.dot(yb, wr_ref[...], preferred_element_type=F32)
    o_ref[...] = (ga_ref[...].astype(F32) * y_a + gb_ref[...].astype(F32) * y_b).astype(o_ref.dtype)


def _merge(z, ybpre, g, gates, w_conv_bf, w_rwkv_bf):
    tm = 512
    row = lambda w: pl.BlockSpec((tm, w), lambda i: (i, 0))
    return pl.pallas_call(
        _merge_kernel,
        grid=(N_TOK // tm,),
        in_specs=[row(W_BRANCH), row(W_BRANCH), row(W_BRANCH),
                  pl.BlockSpec((tm, D_MODEL), lambda i: (i, 0)),
                  pl.BlockSpec((tm, D_MODEL), lambda i: (i, 1)),
                  pl.BlockSpec((W_BRANCH, D_MODEL), lambda i: (0, 0)),
                  pl.BlockSpec((W_BRANCH, D_MODEL), lambda i: (0, 0))],
        out_specs=row(D_MODEL),
        out_shape=jax.ShapeDtypeStruct((N_TOK, D_MODEL), BF16),
        compiler_params=_params(("arbitrary",)),
        name="merge",
    )(z, ybpre, g, gates, gates, w_conv_bf, w_rwkv_bf)


def _outproj_kernel(m_ref, wo_ref, x_ref, g1_ref, sh2_ref, sc2_ref, npost_ref, npre_ref,
                    x1_ref, h2_ref):
    out = jnp.dot(m_ref[...], wo_ref[...], preferred_element_type=F32)
    x1 = x_ref[...] + g1_ref[...] * (_rms(out) * npost_ref[...])
    x1_ref[...] = x1
    h2_ref[...] = (_rms(x1) * npre_ref[...]) * (1.0 + sc2_ref[...]) + sh2_ref[...]


def _outproj(merged, w_o_bf, x, mod3, n_post, n_pre):
    tm = 256
    row = pl.BlockSpec((tm, D_MODEL), lambda i: (i, 0))
    vec = pl.BlockSpec((1, D_MODEL), lambda i: (0, 0))
    out = jax.ShapeDtypeStruct((N_TOK, D_MODEL), F32)
    return pl.pallas_call(
        _outproj_kernel,
        grid=(N_TOK // tm,),
        in_specs=[row, pl.BlockSpec((D_MODEL, D_MODEL), lambda i: (0, 0)), row,
                  _mod_spec(tm, 2), _mod_spec(tm, 3), _mod_spec(tm, 4), vec, vec],
        out_specs=[row, row],
        out_shape=[out, out],
        compiler_params=_params(("arbitrary",)),
        name="outproj",
    )(merged, w_o_bf, x, mod3, mod3, mod3, n_post, n_pre)


def _first_index_of_max(x, axis, n):
    m = jnp.max(x, axis=axis, keepdims=True)
    idx = lax.broadcasted_iota(I32, x.shape, axis).astype(F32)
    first = jnp.min(jnp.where(x == m, idx, float(n)), axis=axis, keepdims=True)
    return m, idx, first


def _router_kernel(h_ref, rw_ref, rb_ref, eidx_ref, wsel_ref, rank_ref, cnt_ref, base_ref, *, tm):
    @pl.when(pl.program_id(0) == 0)
    def _():
        base_ref[...] = jnp.zeros_like(base_ref)

    logits = lax.dot_general(rw_ref[...], h_ref[...], (((1,), (1,)), ((), ())),
                             precision=lax.Precision.HIGHEST, preferred_element_type=F32)
    scores = jax.nn.sigmoid(logits)
    biased = scores + rb_ref[...]
    neg = -jnp.inf

    per_group = N_EXPERTS // N_GROUPS
    grp = biased.reshape(N_GROUPS, per_group, tm)
    m1, idx, first = _first_index_of_max(grp, 1, per_group)
    m2 = jnp.max(jnp.where(idx == first, neg, grp), axis=1, keepdims=True)
    gscore = (m1 + m2).reshape(N_GROUPS, tm)

    gsel = jnp.zeros((N_GROUPS, tm), F32)
    for _ in range(TOPK_GROUPS):
        _, gidx, gfirst = _first_index_of_max(gscore, 0, N_GROUPS)
        hit = gidx == gfirst
        gsel = jnp.where(hit, 1.0, gsel)
        gscore = jnp.where(hit, neg, gscore)
    emask = jnp.broadcast_to(gsel[:, None, :], (N_GROUPS, per_group, tm)).reshape(N_EXPERTS, tm)

    cand = jnp.where(emask > 0.5, biased, neg)
    mem = jnp.zeros((N_EXPERTS, tm), F32)
    picks = []
    for _ in range(TOP_K):
        _, eidx, efirst = _first_index_of_max(cand, 0, N_EXPERTS)
        hit = eidx == efirst
        mem = jnp.where(hit, 1.0, mem)
        cand = jnp.where(hit, neg, cand)
        picks.append((efirst, hit))

    s_i = lax.broadcasted_iota(I32, (tm, tm), 0)
    t_i = lax.broadcasted_iota(I32, (tm, tm), 1)
    upper = (s_i <= t_i).astype(BF16)
    incl = jnp.dot(mem.astype(BF16), upper, preferred_element_type=F32)
    rank = base_ref[...] + incl - mem
    base_ref[...] = base_ref[...] + jnp.sum(mem, axis=1, keepdims=True)
    cnt_ref[...] = base_ref[...]

    wsum = jnp.zeros((1, tm), F32)
    wrows = []
    for j, (efirst, hit) in enumerate(picks):
        wj = jnp.sum(jnp.where(hit, scores, 0.0), axis=0, keepdims=True)
        wrows.append(wj)
        wsum = wsum + wj
        eidx_ref[pl.ds(j, 1), :] = efirst.astype(I32)
        rank_ref[pl.ds(j, 1), :] = jnp.sum(jnp.where(hit, rank, 0.0), axis=0, keepdims=True).astype(I32)
    for j, wj in enumerate(wrows):
        wsel_ref[pl.ds(j, 1), :] = wj / wsum * ROUTED_SCALE


def _router(h2, rw_t, rb_col):
    tm = 256
    tok = lambda dt: jax.ShapeDtypeStruct((TOP_K, N_TOK), dt)
    tspec = pl.BlockSpec((TOP_K, tm), lambda i: (0, i))
    return pl.pallas_call(
        functools.partial(_router_kernel, tm=tm),
        grid=(N_TOK // tm,),
        in_specs=[pl.BlockSpec((tm, D_MODEL), lambda i: (i, 0)),
                  pl.BlockSpec((N_EXPERTS, D_MODEL), lambda i: (0, 0)),
                  pl.BlockSpec((N_EXPERTS, 1), lambda i: (0, 0))],
        out_specs=[tspec, tspec, tspec, pl.BlockSpec((N_EXPERTS, 1), lambda i: (0, 0))],
        out_shape=[tok(I32), tok(F32), tok(I32), jax.ShapeDtypeStruct((N_EXPERTS, 1), F32)],
        scratch_shapes=[pltpu.VMEM((N_EXPERTS, 1), F32)],
        compiler_params=_params(("arbitrary",)),
        name="router",
    )(h2, rw_t, rb_col)


def _expert_kernel(be_ref, first_ref, nused_ref, tok_ref,
                   h_hbm, wg_ref, wu_ref, wd_ref, o_ref, xbuf, wgb, wub, wdb, sem):
    b = pl.program_id(0)
    valid = b < nused_ref[0]

    def row_copy(tok, i):
        return pltpu.make_async_copy(h_hbm.at[pl.ds(tok, 1), :], xbuf.at[pl.ds(i, 1), :], sem)

    @pl.when(valid)
    def _():
        base = b * MOE_ROWS

        def issue(i, carry):
            row_copy(tok_ref[base + i], i).start()
            return carry

        lax.fori_loop(0, MOE_ROWS, issue, 0)

        @pl.when(first_ref[b] == 1)
        def _():
            wgb[...] = wg_ref[...].astype(BF16)
            wub[...] = wu_ref[...].astype(BF16)
            wdb[...] = wd_ref[...].astype(BF16)

        def drain(i, carry):
            row_copy(0, i).wait()
            return carry

        lax.fori_loop(0, MOE_ROWS, drain, 0)

        x = xbuf[...].astype(BF16)
        gate = jnp.dot(x, wgb[...], preferred_element_type=F32)
        up = jnp.dot(x, wub[...], preferred_element_type=F32)
        hid = (gate * jax.nn.sigmoid(gate) * up).astype(BF16)
        o_ref[...] = jnp.dot(hid, wdb[...], preferred_element_type=F32)

    @pl.when(jnp.logical_not(valid))
    def _():
        o_ref[...] = jnp.zeros_like(o_ref)


def _experts(blk_expert, blk_first, n_used, tok_of_slot, h2, exp_gate, exp_up, exp_down):
    wspec_in = pl.BlockSpec((None, D_MODEL, D_EXPERT), lambda b, be, fi, nu, tk: (be[b], 0, 0))
    wspec_out = pl.BlockSpec((None, D_EXPERT, D_MODEL), lambda b, be, fi, nu, tk: (be[b], 0, 0))
    grid_spec = pltpu.PrefetchScalarGridSpec(
        num_scalar_prefetch=4,
        grid=(MOE_BLOCKS,),
        in_specs=[pl.BlockSpec(memory_space=pl.ANY), wspec_in, wspec_in, wspec_out],
        out_specs=pl.BlockSpec((MOE_ROWS, D_MODEL), lambda b, be, fi, nu, tk: (b, 0)),
        scratch_shapes=[pltpu.VMEM((MOE_ROWS, D_MODEL), F32),
                        pltpu.VMEM((D_MODEL, D_EXPERT), BF16),
                        pltpu.VMEM((D_MODEL, D_EXPERT), BF16),
                        pltpu.VMEM((D_EXPERT, D_MODEL), BF16),
                        pltpu.SemaphoreType.DMA(())],
    )
    return pl.pallas_call(
        _expert_kernel,
        grid_spec=grid_spec,
        out_shape=jax.ShapeDtypeStruct((MOE_BLOCKS * MOE_ROWS, D_MODEL), F32),
        compiler_params=_params(("arbitrary",)),
        name="experts",
    )(blk_expert, blk_first, n_used, tok_of_slot, h2, exp_gate, exp_up, exp_down)


def _combine_kernel(pos_ref, y_hbm, wt_ref, h_ref, x1_ref, g2_ref, npost_ref,
                    sg_ref, su_ref, sd_ref, o_ref, buf, sem, *, tm):
    i = pl.program_id(0)
    base = i * (TOP_K * tm)

    def row_copy(p, j):
        return pltpu.make_async_copy(y_hbm.at[pl.ds(p, 1), :], buf.at[pl.ds(j, 1), :], sem)

    def issue(j, carry):
        row_copy(pos_ref[base + j], j).start()
        return carry

    lax.fori_loop(0, TOP_K * tm, issue, 0)

    hb = h_ref[...].astype(BF16)
    gate = jnp.dot(hb, sg_ref[...], preferred_element_type=F32)
    up = jnp.dot(hb, su_ref[...], preferred_element_type=F32)
    hid = (gate * jax.nn.sigmoid(gate) * up).astype(BF16)
    shared = jnp.dot(hid, sd_ref[...], preferred_element_type=F32)

    def drain(j, carry):
        row_copy(0, j).wait()
        return carry

    lax.fori_loop(0, TOP_K * tm, drain, 0)

    wt = wt_ref[...]
    routed = jnp.zeros((tm, D_MODEL), F32)
    for j in range(TOP_K):
        routed = routed + buf[pl.ds(j * tm, tm), :] * wt[:, j:j + 1]
    f = routed + shared
    o_ref[...] = x1_ref[...] + g2_ref[...] * (_rms(f) * npost_ref[...])


def _combine(pos_flat, y_sorted, wsel_t, h2, x1, mod3, n_post, sg_bf, su_bf, sd_bf):
    tm = COMBINE_ROWS
    row = pl.BlockSpec((tm, D_MODEL), lambda i, p: (i, 0))
    grid_spec = pltpu.PrefetchScalarGridSpec(
        num_scalar_prefetch=1,
        grid=(N_TOK // tm,),
        in_specs=[pl.BlockSpec(memory_space=pl.ANY),
                  pl.BlockSpec((tm, TOP_K), lambda i, p: (i, 0)),
                  row, row,
                  pl.BlockSpec((None, 1, D_MODEL), lambda i, p: (_cond_of_tile(i, tm), 0, 5)),
                  pl.BlockSpec((1, D_MODEL), lambda i, p: (0, 0)),
                  pl.BlockSpec((D_MODEL, D_EXPERT), lambda i, p: (0, 0)),
                  pl.BlockSpec((D_MODEL, D_EXPERT), lambda i, p: (0, 0)),
                  pl.BlockSpec((D_EXPERT, D_MODEL), lambda i, p: (0, 0))],
        out_specs=row,
        scratch_shapes=[pltpu.VMEM((TOP_K * tm, D_MODEL), F32), pltpu.SemaphoreType.DMA(())],
    )
    return pl.pallas_call(
        functools.partial(_combine_kernel, tm=tm),
        grid_spec=grid_spec,
        out_shape=jax.ShapeDtypeStruct((N_TOK, D_MODEL), F32),
        compiler_params=_params(("arbitrary",)),
        name="combine",
    )(pos_flat, y_sorted, wsel_t, h2, x1, mod3, n_post, sg_bf, su_bf, sd_bf)


def kernel(x_prompt, x_sample, state_fwd, state_bwd, c, c_ctx, ada_w, ada_b, norm_pre_mix, norm_post_mix, norm_pre_ffn, norm_post_ffn, w_in, conv_w, w_out_conv, decay_w0, decay_w2, iclr_a0, iclr_a2, gate_g2, k_k, k_a, r_k, lnx_w, lnx_b, w_out_rwkv, w_o, router_w, router_bias, exp_gate, exp_up, exp_down, sh_gate, sh_up, sh_down):
    x = jnp.concatenate([x_prompt.reshape(N_PROMPT, D_MODEL), x_sample.reshape(N_SAMPLE, D_MODEL)], axis=0)

    cond = jnp.concatenate([c_ctx[None, :], c, jnp.zeros((N_COND - 1 - c.shape[0], D_MODEL), F32)], axis=0)
    mod3 = _ada_table(cond, ada_w[0], ada_b).reshape(N_COND, 1, 6 * D_MODEL)

    h = _prenorm(x, norm_pre_mix, mod3)
    w_in_bf = w_in[0].astype(BF16)
    z = _conv_branch(h, w_in_bf, conv_w[0])
    rkv = _matmul(h, w_in_bf, col0=3 * W_BRANCH, n_cols=3 * W_BRANCH, tn=512, out_dtype=F32, name="proj_rkv")
    lora = _matmul(h, w_in_bf[:, 6 * W_BRANCH:6 * W_BRANCH + LORA_W], col0=0, n_cols=LORA_W, tn=LORA_W,
                   out_dtype=F32, name="proj_lora")
    gates = _matmul(h, w_in_bf[:, 6 * W_BRANCH + LORA_W:], col0=0, n_cols=2 * D_MODEL, tn=512,
                    out_dtype=BF16, act="sigmoid", name="proj_gates")

    w2 = jnp.zeros((5, LORA_W, W_BRANCH), F32)
    w2 = w2.at[0, 0:96].set(decay_w2[0, 0]).at[1, 96:192].set(decay_w2[0, 1])
    w2 = w2.at[2, 192:288].set(iclr_a2[0, 0]).at[3, 288:384].set(iclr_a2[0, 1])
    w2 = w2.at[4, 384:640].set(gate_g2[0]).astype(BF16)
    w_f, w_b, a_f, a_b, g = _lora_stage(lora, w2, decay_w0[0], iclr_a0[0])

    kk_l = _head_param_to_chain_layout(k_k[0])
    ka_l = _head_param_to_chain_layout(k_a[0])
    rk_l = _head_param_to_chain_layout(r_k[0].reshape(-1))
    lw_l = _head_param_to_chain_layout(lnx_w[0])
    lb_l = _head_param_to_chain_layout(lnx_b[0])

    def run_scan(lo, n_batch, seq, s0_f, s0_b):
        sl = slice(lo, lo + n_batch * seq)
        lay = lambda t: _to_chain_layout(t[sl], n_batch, seq)
        r_l = lay(rkv[:, 0:W_BRANCH])
        k_l = lay(rkv[:, W_BRANCH:2 * W_BRANCH])
        v_l = lay(rkv[:, 2 * W_BRANCH:])
        w_l = jnp.stack([lay(w_f), lay(w_b)])
        a_l = jnp.stack([lay(a_f), lay(a_b)])
        s0 = jnp.stack([_state_to_chain_layout(s0_f), _state_to_chain_layout(s0_b)])
        y, bon, s_fin = _scan(r_l, k_l, v_l, w_l, a_l, s0, kk_l, ka_l, rk_l)
        return _from_chain_layout(_scan_post(y, bon, lw_l, lb_l)), s_fin

    zero_state = jnp.zeros((16, N_HEADS, HEAD, HEAD), F32)
    yb_p, s_fin = run_scan(0, 16, SEQ_PROMPT, zero_state, zero_state)
    yb_s, _ = run_scan(N_PROMPT, 8, SEQ_SAMPLE, state_fwd[:, 0], state_bwd[:, 0])
    ybpre = jnp.concatenate([yb_p, yb_s], axis=0)

    merged = _merge(z, ybpre, g, gates, w_out_conv[0].astype(BF16), w_out_rwkv[0].astype(BF16))
    x1, h2 = _outproj(merged, w_o[0].astype(BF16), x, mod3, norm_post_mix, norm_pre_ffn)

    eidx, wsel, rank, counts = _router(h2, router_w[0].T, router_bias[0][:, None])
    counts = counts[:, 0].astype(I32)
    padded = (counts + MOE_ROWS - 1) // MOE_ROWS * MOE_ROWS
    pad_end = jnp.cumsum(padded)
    pad_start = pad_end - padded
    pos = pad_start[eidx] + rank
    tok_ids = jnp.broadcast_to(jnp.arange(N_TOK, dtype=I32)[None, :], (TOP_K, N_TOK))
    tok_of_slot = jnp.zeros((MOE_BLOCKS * MOE_ROWS,), I32).at[pos.reshape(-1)].set(tok_ids.reshape(-1))
    blk_start = jnp.arange(MOE_BLOCKS, dtype=I32) * MOE_ROWS
    blk_expert = jnp.minimum(jnp.searchsorted(pad_end, blk_start, side="right"), N_EXPERTS - 1).astype(I32)
    n_used = (pad_end[-1] // MOE_ROWS).astype(I32).reshape(1)
    blk_first = (blk_start == pad_start[blk_expert]).astype(I32)
    y_sorted = _experts(blk_expert, blk_first, n_used, tok_of_slot, h2, exp_gate[0], exp_up[0], exp_down[0])

    pos_flat = pos.reshape(TOP_K, N_TOK // COMBINE_ROWS, COMBINE_ROWS).transpose(1, 0, 2).reshape(-1)
    out = _combine(pos_flat, y_sorted, wsel.T, h2, x1, mod3, norm_post_ffn,
                  sh_gate[0].astype(BF16), sh_up[0].astype(BF16), sh_down[0].astype(BF16))

    y_p = out[:N_PROMPT].reshape(x_prompt.shape)
    y_s = out[N_PROMPT:].reshape(x_sample.shape)
    new_f = _state_from_chain_layout(s_fin[0])[:, None]
    new_b = _state_from_chain_layout(s_fin[1])[:, None]
    return (y_p, y_s, new_f, new_b)
```

```python
import functools
import math

import jax
import jax.numpy as jnp
from jax import lax
from jax.experimental import pallas as pl
from jax.experimental.pallas import tpu as pltpu

F32 = jnp.float32
BF16 = jnp.bfloat16
I32 = jnp.int32

D_MODEL = 2048
N_SAMPLE = 8 * 1024
SEQ_SAMPLE = 1024
N_PROMPT = 16 * 256
SEQ_PROMPT = 256
N_TOK = N_SAMPLE + N_PROMPT
GRID_W = 64
W_BRANCH = 1024
HEAD = 64
N_HEADS = 16
N_COND = 16
LORA_PAD = 128
LORA_W = 4 * LORA_PAD + 256
N_EXPERTS = 64
TOP_K = 8
N_GROUPS = 8
TOPK_GROUPS = 4
D_EXPERT = 512
ROUTED_SCALE = 2.5
NORM_EPS = 1e-6
GN_EPS = 64e-5
LANES = 128
CHAIN_BATCH = 8
TIME_ROWS = N_TOK // CHAIN_BATCH
CONV_HALO = 16
MOE_ROWS = 256
MOE_BLOCKS = N_TOK * TOP_K // MOE_ROWS + N_EXPERTS
COMBINE_ROWS = 128
DMA_UNROLL = 8
VMEM_LIMIT = 56 * 1024 * 1024


def _params(sem, vmem=VMEM_LIMIT):
    return pltpu.CompilerParams(dimension_semantics=sem, vmem_limit_bytes=vmem)


def _split_specs(tm, width):
    n_s = N_SAMPLE // tm
    return (pl.BlockSpec((tm, width), lambda i, *_: (jnp.minimum(i, n_s - 1), 0)),
            pl.BlockSpec((tm, width), lambda i, *_: (jnp.maximum(i - n_s, 0), 0)))


def _on_owner(is_sample, fn):
    pl.when(is_sample)(lambda: fn(True))
    pl.when(jnp.logical_not(is_sample))(lambda: fn(False))


class _ColumnTiles:
    def __init__(self, tt):
        self.tt = tt
        self.n_s = SEQ_SAMPLE // tt
        self.per_group = SEQ_PROMPT // tt
        self.grid = (TIME_ROWS // tt, CHAIN_BATCH)

    def is_sample(self):
        return pl.program_id(0) < self.n_s

    def tokens(self, width):
        return pl.BlockSpec((self.tt, width), lambda j, b, *_: (j, b))

    def sample_major(self, width):
        n_s = self.n_s
        return pl.BlockSpec((None, self.tt, width), lambda j, b, *_: (
            jnp.where(j < n_s, b, CHAIN_BATCH - 1), jnp.minimum(j, n_s - 1), 0))

    def prompt_major(self, width):
        n_s, per = self.n_s, self.per_group
        return pl.BlockSpec((None, self.tt, width), lambda j, b, *_: (
            jnp.where(j < n_s, 0, (j - n_s) // per * CHAIN_BATCH + b),
            jnp.where(j < n_s, 0, (j - n_s) % per), 0))

    def mod(self, chunk):
        n_s = self.n_s
        return pl.BlockSpec((None, 1, D_MODEL), lambda j, b, *_: (jnp.where(j < n_s, 1 + b, 0), 0, chunk))

    def const(self, shape):
        return pl.BlockSpec(shape, lambda j, b, *_: (0,) * len(shape))


def _rms(x):
    return x * lax.rsqrt(jnp.mean(x * x, axis=-1, keepdims=True) + NORM_EPS)


def _silu(x):
    return x * jax.nn.sigmoid(x)


def _ada_kernel(c_ref, w_ref, b_ref, o_ref):
    s = _silu(c_ref[...]).astype(BF16)
    o_ref[...] = jnp.dot(s, w_ref[...].astype(BF16), preferred_element_type=F32) + b_ref[...]


def _ada_table(cond, ada_w, ada_b):
    tn = 1536
    return pl.pallas_call(
        _ada_kernel,
        grid=(6 * D_MODEL // tn,),
        in_specs=[pl.BlockSpec((N_COND, D_MODEL), lambda j: (0, 0)),
                  pl.BlockSpec((D_MODEL, tn), lambda j: (0, j)),
                  pl.BlockSpec((1, tn), lambda j: (0, j))],
        out_specs=pl.BlockSpec((N_COND, tn), lambda j: (0, j)),
        out_shape=jax.ShapeDtypeStruct((N_COND, 6 * D_MODEL), F32),
        compiler_params=_params(("arbitrary",)),
        name="ada_table",
    )(cond, ada_w, ada_b)


def _prenorm_kernel(xs_ref, xp_ref, g_ref, sh_ref, sc_ref, o_ref, *, tiles):
    def run(is_sample):
        y = _rms((xs_ref if is_sample else xp_ref)[...]) * g_ref[...]
        o_ref[...] = (y * (1.0 + sc_ref[...]) + sh_ref[...]).astype(o_ref.dtype)

    _on_owner(tiles.is_sample(), run)


def _prenorm(x_sample, x_prompt, gain, mod3):
    tiles = _ColumnTiles(256)
    return pl.pallas_call(
        functools.partial(_prenorm_kernel, tiles=tiles),
        grid=tiles.grid,
        in_specs=[tiles.sample_major(D_MODEL), tiles.prompt_major(D_MODEL),
                  tiles.const((1, D_MODEL)), tiles.mod(0), tiles.mod(1)],
        out_specs=tiles.tokens(D_MODEL),
        out_shape=jax.ShapeDtypeStruct((TIME_ROWS, CHAIN_BATCH * D_MODEL), BF16),
        compiler_params=_params(("arbitrary", "arbitrary")),
        name="prenorm",
    )(x_sample, x_prompt, gain, mod3, mod3)


def _mm_kernel(a_ref, w_ref, o_ref, *, act):
    acc = jnp.dot(a_ref[...], w_ref[...], preferred_element_type=F32)
    if act == "sigmoid":
        acc = jax.nn.sigmoid(acc)
    o_ref[...] = acc.astype(o_ref.dtype)


def _matmul(a, w, *, col0, n_cols, tn, out_dtype, act=None, tm=1024, name="matmul"):
    m, k = a.shape
    off = col0 // tn
    return pl.pallas_call(
        functools.partial(_mm_kernel, act=act),
        grid=(m // tm, n_cols // tn),
        in_specs=[pl.BlockSpec((tm, k), lambda i, j: (i, 0)),
                  pl.BlockSpec((k, tn), lambda i, j: (0, j + off))],
        out_specs=pl.BlockSpec((tm, tn), lambda i, j: (i, j)),
        out_shape=jax.ShapeDtypeStruct((m, n_cols), out_dtype),
        compiler_params=_params(("arbitrary", "arbitrary")),
        name=name,
    )(a, w)


def _convproj_kernel(hp_ref, h_ref, hn_ref, wb_ref, wc_ref, wx_ref, cw_ref, o_ref, *, tm):
    h = h_ref[...]
    h_ext = jnp.concatenate([hp_ref[...], h, hn_ref[...]], axis=0)
    cb = jnp.dot(h, wb_ref[...], preferred_element_type=F32)
    u_ext = (jnp.dot(h_ext, wc_ref[...], preferred_element_type=F32)
             * jnp.dot(h_ext, wx_ref[...], preferred_element_type=F32))
    u = u_ext[CONV_HALO:CONV_HALO + tm]
    u_prev = u_ext[CONV_HALO - CHAIN_BATCH:CONV_HALO - CHAIN_BATCH + tm]
    u_next = u_ext[CONV_HALO + CHAIN_BATCH:CONV_HALO + CHAIN_BATCH + tm]
    seg = jnp.where(pl.program_id(0) < N_SAMPLE // tm, GRID_W, SEQ_PROMPT)
    row = lax.broadcasted_iota(I32, u.shape, 0)
    pos = (pl.program_id(0) * (tm // CHAIN_BATCH) + row // CHAIN_BATCH) & (seg - 1)
    u_prev = jnp.where(pos == 0, 0.0, u_prev)
    u_next = jnp.where(pos == seg - 1, 0.0, u_next)
    cw = cw_ref[...]
    conv = cw[0:1, :] * u_prev + cw[1:2, :] * u + cw[2:3, :] * u_next
    o_ref[...] = (cb * conv).astype(o_ref.dtype)


def _conv_branch(h, w_in_bf, conv_w):
    tm, tn = 1024, 256
    nb = W_BRANCH // tn
    per = tm // CONV_HALO
    return pl.pallas_call(
        functools.partial(_convproj_kernel, tm=tm),
        grid=(N_TOK // tm, nb),
        in_specs=[pl.BlockSpec((CONV_HALO, D_MODEL), lambda i, j: (jnp.maximum(i * per - 1, 0), 0)),
                  pl.BlockSpec((tm, D_MODEL), lambda i, j: (i, 0)),
                  pl.BlockSpec((CONV_HALO, D_MODEL),
                               lambda i, j: (jnp.minimum((i + 1) * per, N_TOK // CONV_HALO - 1), 0)),
                  pl.BlockSpec((D_MODEL, tn), lambda i, j: (0, j)),
                  pl.BlockSpec((D_MODEL, tn), lambda i, j: (0, j + nb)),
                  pl.BlockSpec((D_MODEL, tn), lambda i, j: (0, j + 2 * nb)),
                  pl.BlockSpec((3, tn), lambda i, j: (0, j))],
        out_specs=pl.BlockSpec((tm, tn), lambda i, j: (i, j)),
        out_shape=jax.ShapeDtypeStruct((N_TOK, W_BRANCH), BF16),
        compiler_params=_params(("arbitrary", "arbitrary")),
        name="conv_branch",
    )(h, h, h, w_in_bf, w_in_bf, w_in_bf, conv_w)


def _lora_kernel(x_ref, wd_ref, wa_ref, wg_ref, w0_ref, a0_ref, wa_out_ref, g_ref):
    def group(i):
        return x_ref[:, i * LORA_PAD:(i + 1) * LORA_PAD]

    def decay(wl):
        return jnp.exp(-jax.nn.sigmoid(wl) * math.exp(-0.5))

    for d in range(2):
        dl = jnp.dot(jnp.tanh(group(d)).astype(BF16), wd_ref[d], preferred_element_type=F32)
        wa_out_ref[:, d * W_BRANCH:(d + 1) * W_BRANCH] = decay(w0_ref[d:d + 1, :] + dl)
        al = jnp.dot(group(2 + d).astype(BF16), wa_ref[d], preferred_element_type=F32)
        wa_out_ref[:, (2 + d) * W_BRANCH:(3 + d) * W_BRANCH] = jax.nn.sigmoid(a0_ref[d:d + 1, :] + al)
    gl = jax.nn.sigmoid(x_ref[:, 4 * LORA_PAD:]).astype(BF16)
    g_ref[...] = jnp.dot(gl, wg_ref[...], preferred_element_type=F32)


def _lora_stage(lora, wd2, wa2, wg2, w0, a0):
    tm = 512
    return pl.pallas_call(
        _lora_kernel,
        grid=(N_TOK // tm,),
        in_specs=[pl.BlockSpec((tm, LORA_W), lambda i: (i, 0)),
                  pl.BlockSpec((2, LORA_PAD, W_BRANCH), lambda i: (0, 0, 0)),
                  pl.BlockSpec((2, LORA_PAD, W_BRANCH), lambda i: (0, 0, 0)),
                  pl.BlockSpec((256, W_BRANCH), lambda i: (0, 0)),
                  pl.BlockSpec((2, W_BRANCH), lambda i: (0, 0)),
                  pl.BlockSpec((2, W_BRANCH), lambda i: (0, 0))],
        out_specs=[pl.BlockSpec((tm, 4 * W_BRANCH), lambda i: (i, 0)),
                   pl.BlockSpec((tm, W_BRANCH), lambda i: (i, 0))],
        out_shape=[jax.ShapeDtypeStruct((N_TOK, 4 * W_BRANCH), F32),
                   jax.ShapeDtypeStruct((N_TOK, W_BRANCH), F32)],
        compiler_params=_params(("arbitrary",)),
        name="lora_stage",
    )(lora, wd2, wa2, wg2, w0, a0)


def _chain_tiles(x_ref, t0):
    both = (x_ref[t0], x_ref[t0 + 1])
    rows = [both[tl][:, p * LANES:(p + 1) * LANES] for tl in range(2) for p in range(CHAIN_BATCH)]
    sq = jnp.concatenate(rows, axis=0).T
    top, bot = sq[:HEAD], sq[HEAD:]
    low = lax.broadcasted_iota(I32, (HEAD, LANES), 1) < HEAD
    return (jnp.where(low, top, pltpu.roll(bot, HEAD, 1)),
            jnp.where(low, pltpu.roll(top, HEAD, 1), bot))


def _scan_kernel(r_ref, k_ref, v_ref, w_ref, a_ref, s0_ref, kk_ref, ka_ref, rk_ref,
                 y_ref, bon_ref, sf_ref, s_ref, ops_ref, *, tc):
    d = pl.program_id(0)
    c = pl.program_id(2)

    @pl.when(c == 0)
    def _():
        s_ref[...] = s0_ref[...]

    def pair(i, carry):
        t0 = 2 * jnp.where(d == 0, i, tc // 2 - 1 - i)
        r2, k2, v2 = _chain_tiles(r_ref, t0), _chain_tiles(k_ref, t0), _chain_tiles(v_ref, t0)
        w2, a2 = _chain_tiles(w_ref, t0), _chain_tiles(a_ref, t0)
        for tl in range(2):
            kk = k2[tl] * kk_ref[...]
            kk = kk * lax.rsqrt(jnp.sum(kk * kk, axis=0, keepdims=True) + 1e-12)
            kd = k2[tl] * (1.0 + (a2[tl] - 1.0) * ka_ref[...])
            for q, val in enumerate((kk, kk * a2[tl], kd, w2[tl], r2[tl], v2[tl])):
                ops_ref[tl, q] = val
            bon_ref[t0 + tl] = jnp.sum(r2[tl] * kd * rk_ref[...], axis=0, keepdims=True) * v2[tl]

        for s in range(2):
            tl = jnp.where(d == 0, s, 1 - s)
            vt = ops_ref[tl, 5]
            sa = jnp.zeros((HEAD, LANES), F32)
            for j in range(HEAD):
                sa = sa + s_ref[j] * ops_ref[tl, 0, pl.ds(j, 1), :]
            y = jnp.zeros((HEAD, LANES), F32)
            for j in range(HEAD):
                sn = (s_ref[j] * ops_ref[tl, 3, pl.ds(j, 1), :] - sa * ops_ref[tl, 1, pl.ds(j, 1), :]
                      + vt * ops_ref[tl, 2, pl.ds(j, 1), :])
                s_ref[j] = sn
                y = y + sn * ops_ref[tl, 4, pl.ds(j, 1), :]
            y_ref[t0 + tl] = y
        return carry

    lax.fori_loop(0, tc // 2, pair, 0)

    @pl.when(c == pl.num_programs(2) - 1)
    def _():
        sf_ref[...] = s_ref[...]


def _scan(rkv3, wa3, row0, seq, s0, kk_l, ka_l, rk_l, *, tc=32):
    g_n = s0.shape[1]
    nc = seq // tc

    def tchunk(d, c):
        return jnp.where(d == 0, c, nc - 1 - c)

    def tok(col):
        return pl.BlockSpec((tc, CHAIN_BATCH, W_BRANCH),
                            lambda d, g, c: (row0 // tc + g * nc + tchunk(d, c), 0, col))

    def tok_dir(col):
        return pl.BlockSpec((tc, CHAIN_BATCH, W_BRANCH),
                            lambda d, g, c: (row0 // tc + g * nc + tchunk(d, c), 0, col + d))

    chain = pl.BlockSpec((None, None, tc, HEAD, LANES), lambda d, g, c: (d, g, tchunk(d, c), 0, 0))
    state = pl.BlockSpec((None, None, HEAD, HEAD, LANES), lambda d, g, c: (d, g, 0, 0, 0))
    par = pl.BlockSpec((HEAD, LANES), lambda d, g, c: (0, 0))
    seq_shape = jax.ShapeDtypeStruct((2, g_n, seq, HEAD, LANES), F32)
    return pl.pallas_call(
        functools.partial(_scan_kernel, tc=tc),
        grid=(2, g_n, nc),
        in_specs=[tok(0), tok(1), tok(2), tok_dir(0), tok_dir(2), state, par, par, par],
        out_specs=[chain, chain, state],
        out_shape=[seq_shape, seq_shape, jax.ShapeDtypeStruct((2, g_n, HEAD, HEAD, LANES), F32)],
        scratch_shapes=[pltpu.VMEM((HEAD, HEAD, LANES), F32), pltpu.VMEM((2, 6, HEAD, LANES), F32)],
        compiler_params=_params(("arbitrary", "arbitrary", "arbitrary")),
        name="wkv7_scan",
    )(rkv3, rkv3, rkv3, wa3, wa3, s0, kk_l, ka_l, rk_l)


def _scan_post_kernel(y_ref, bon_ref, lw_ref, lb_ref, o_ref, *, tc):
    low = lax.broadcasted_iota(I32, (HEAD, LANES), 1) < HEAD

    def pair(i, carry):
        t0 = 2 * i
        tiles = []
        for tl in range(2):
            ys = y_ref[0, t0 + tl] + y_ref[1, t0 + tl]
            dev = ys - jnp.mean(ys, axis=0, keepdims=True)
            var = jnp.mean(dev * dev, axis=0, keepdims=True)
            yn = dev * lax.rsqrt(var + GN_EPS) * lw_ref[...] + lb_ref[...]
            tiles.append(yn + bon_ref[0, t0 + tl] + bon_ref[1, t0 + tl])
        top = jnp.where(low, tiles[0], pltpu.roll(tiles[1], HEAD, 1))
        bot = jnp.where(low, pltpu.roll(tiles[0], HEAD, 1), tiles[1])
        sq = jnp.concatenate([top, bot], axis=0).T
        for tl in range(2):
            for p in range(CHAIN_BATCH):
                r0 = tl * HEAD + p * CHAIN_BATCH
                o_ref[t0 + tl, :, pl.ds(p * LANES, LANES)] = sq[r0:r0 + CHAIN_BATCH, :]
        return carry

    lax.fori_loop(0, tc // 2, pair, 0)


def _scan_post(y, bon, lw_l, lb_l):
    _, g_n, seq = y.shape[:3]
    tc = 64
    nc = seq // tc
    both = pl.BlockSpec((2, None, tc, HEAD, LANES), lambda g, c: (0, g, c, 0, 0))
    par = pl.BlockSpec((HEAD, LANES), lambda g, c: (0, 0))
    return pl.pallas_call(
        functools.partial(_scan_post_kernel, tc=tc),
        grid=(g_n, nc),
        in_specs=[both, both, par, par],
        out_specs=pl.BlockSpec((tc, CHAIN_BATCH, W_BRANCH), lambda g, c: (g * nc + c, 0, 0)),
        out_shape=jax.ShapeDtypeStruct((g_n * seq, CHAIN_BATCH, W_BRANCH), F32),
        compiler_params=_params(("arbitrary", "arbitrary")),
        name="scan_post",
    )(y, bon, lw_l, lb_l)


def _state_to_chain_layout(s):
    g_n = s.shape[0] // CHAIN_BATCH
    s = s.reshape(g_n, CHAIN_BATCH, N_HEADS // 2, 2, HEAD, HEAD)
    return jnp.transpose(s, (0, 5, 4, 3, 2, 1)).reshape(g_n, HEAD, HEAD, LANES)


def _state_from_chain_layout(s):
    g_n = s.shape[0]
    s = s.reshape(g_n, HEAD, HEAD, 2, N_HEADS // 2, CHAIN_BATCH)
    return jnp.transpose(s, (0, 5, 4, 3, 2, 1)).reshape(g_n * CHAIN_BATCH, N_HEADS, HEAD, HEAD)


def _head_param_to_chain_layout(p):
    p = jnp.transpose(p.reshape(N_HEADS // 2, 2, HEAD), (1, 0, 2))
    p = jnp.broadcast_to(p[:, :, None, :], (2, N_HEADS // 2, CHAIN_BATCH, HEAD))
    return p.reshape(LANES, HEAD).T


def _merge_kernel(z_ref, ybs_ref, ybp_ref, g_ref, ga_ref, gb_ref, wc_ref, wr_ref, o_ref, *, tm):
    def run(is_sample):
        y_a = jnp.dot(z_ref[...], wc_ref[...], preferred_element_type=F32)
        yb = ((ybs_ref if is_sample else ybp_ref)[...] * g_ref[...]).astype(BF16)
        y_b = jnp.dot(yb, wr_ref[...], preferred_element_type=F32)
        o_ref[...] = (ga_ref[...].astype(F32) * y_a + gb_ref[...].astype(F32) * y_b).astype(o_ref.dtype)

    _on_owner(pl.program_id(0) < N_SAMPLE // tm, run)


def _merge(z, yb_s, yb_p, g, gates, w_conv_bf, w_rwkv_bf):
    tm = 512
    row = lambda w: pl.BlockSpec((tm, w), lambda i: (i, 0))
    return pl.pallas_call(
        functools.partial(_merge_kernel, tm=tm),
        grid=(N_TOK // tm,),
        in_specs=[row(W_BRANCH), *_split_specs(tm, W_BRANCH), row(W_BRANCH),
                  pl.BlockSpec((tm, D_MODEL), lambda i: (i, 0)),
                  pl.BlockSpec((tm, D_MODEL), lambda i: (i, 1)),
                  pl.BlockSpec((W_BRANCH, D_MODEL), lambda i: (0, 0)),
                  pl.BlockSpec((W_BRANCH, D_MODEL), lambda i: (0, 0))],
        out_specs=row(D_MODEL),
        out_shape=jax.ShapeDtypeStruct((N_TOK, D_MODEL), BF16),
        compiler_params=_params(("arbitrary",)),
        name="merge",
    )(z, yb_s, yb_p, g, gates, gates, w_conv_bf, w_rwkv_bf)


def _outproj_kernel(m_ref, wo_ref, xs_ref, xp_ref, g1_ref, sh2_ref, sc2_ref, npost_ref, npre_ref,
                    x1_ref, h2_ref, *, tiles):
    def run(is_sample):
        out = jnp.dot(m_ref[...], wo_ref[...], preferred_element_type=F32)
        x1 = (xs_ref if is_sample else xp_ref)[...] + g1_ref[...] * (_rms(out) * npost_ref[...])
        x1_ref[...] = x1
        h2_ref[...] = (_rms(x1) * npre_ref[...]) * (1.0 + sc2_ref[...]) + sh2_ref[...]

    _on_owner(tiles.is_sample(), run)


def _outproj(merged, w_o_bf, x_sample, x_prompt, mod3, n_post, n_pre):
    tiles = _ColumnTiles(256)
    row = tiles.tokens(D_MODEL)
    vec = tiles.const((1, D_MODEL))
    out = jax.ShapeDtypeStruct((TIME_ROWS, CHAIN_BATCH * D_MODEL), F32)
    return pl.pallas_call(
        functools.partial(_outproj_kernel, tiles=tiles),
        grid=tiles.grid,
        in_specs=[row, tiles.const((D_MODEL, D_MODEL)),
                  tiles.sample_major(D_MODEL), tiles.prompt_major(D_MODEL),
                  tiles.mod(2), tiles.mod(3), tiles.mod(4), vec, vec],
        out_specs=[row, row],
        out_shape=[out, out],
        compiler_params=_params(("arbitrary", "arbitrary")),
        name="outproj",
    )(merged, w_o_bf, x_sample, x_prompt, mod3, mod3, mod3, n_post, n_pre)


def _first_index_of_max(x, axis, n):
    m = jnp.max(x, axis=axis, keepdims=True)
    idx = lax.broadcasted_iota(I32, x.shape, axis).astype(F32)
    first = jnp.min(jnp.where(x == m, idx, float(n)), axis=axis, keepdims=True)
    return m, idx, first


def _router_kernel(h_ref, rw_ref, rb_ref, eidx_ref, wsel_ref, rank_ref, cnt_ref, base_ref, *, tm):
    @pl.when(pl.program_id(0) == 0)
    def _():
        base_ref[...] = jnp.zeros_like(base_ref)

    logits = lax.dot_general(rw_ref[...], h_ref[...], (((1,), (1,)), ((), ())),
                             precision=lax.Precision.HIGHEST, preferred_element_type=F32)
    scores = jax.nn.sigmoid(logits)
    biased = scores + rb_ref[...]
    neg = -jnp.inf

    per_group = N_EXPERTS // N_GROUPS
    grp = biased.reshape(N_GROUPS, per_group, tm)
    m1, idx, first = _first_index_of_max(grp, 1, per_group)
    m2 = jnp.max(jnp.where(idx == first, neg, grp), axis=1, keepdims=True)
    gscore = (m1 + m2).reshape(N_GROUPS, tm)

    gsel = jnp.zeros((N_GROUPS, tm), F32)
    for _ in range(TOPK_GROUPS):
        _, gidx, gfirst = _first_index_of_max(gscore, 0, N_GROUPS)
        hit = gidx == gfirst
        gsel = jnp.where(hit, 1.0, gsel)
        gscore = jnp.where(hit, neg, gscore)
    emask = jnp.broadcast_to(gsel[:, None, :], (N_GROUPS, per_group, tm)).reshape(N_EXPERTS, tm)

    cand = jnp.where(emask > 0.5, biased, neg)
    mem = jnp.zeros((N_EXPERTS, tm), F32)
    picks = []
    for _ in range(TOP_K):
        _, eidx, efirst = _first_index_of_max(cand, 0, N_EXPERTS)
        hit = eidx == efirst
        mem = jnp.where(hit, 1.0, mem)
        cand = jnp.where(hit, neg, cand)
        picks.append((efirst, hit))

    s_i = lax.broadcasted_iota(I32, (tm, tm), 0)
    t_i = lax.broadcasted_iota(I32, (tm, tm), 1)
    upper = (s_i <= t_i).astype(BF16)
    incl = jnp.dot(mem.astype(BF16), upper, preferred_element_type=F32)
    rank = base_ref[...] + incl - mem
    base_ref[...] = base_ref[...] + jnp.sum(mem, axis=1, keepdims=True)
    cnt_ref[...] = base_ref[...]

    wsum = jnp.zeros((1, tm), F32)
    wrows = []
    for j, (efirst, hit) in enumerate(picks):
        wj = jnp.sum(jnp.where(hit, scores, 0.0), axis=0, keepdims=True)
        wrows.append(wj)
        wsum = wsum + wj
        eidx_ref[pl.ds(j, 1), :] = efirst.astype(I32)
        rank_ref[pl.ds(j, 1), :] = jnp.sum(jnp.where(hit, rank, 0.0), axis=0, keepdims=True).astype(I32)
    for j, wj in enumerate(wrows):
        wsel_ref[pl.ds(j, 1), :] = wj / wsum * ROUTED_SCALE


def _router(h2, rw_t, rb_col):
    tm = 256
    tok = lambda dt: jax.ShapeDtypeStruct((TOP_K, N_TOK), dt)
    tspec = pl.BlockSpec((TOP_K, tm), lambda i: (0, i))
    return pl.pallas_call(
        functools.partial(_router_kernel, tm=tm),
        grid=(N_TOK // tm,),
        in_specs=[pl.BlockSpec((tm, D_MODEL), lambda i: (i, 0)),
                  pl.BlockSpec((N_EXPERTS, D_MODEL), lambda i: (0, 0)),
                  pl.BlockSpec((N_EXPERTS, 1), lambda i: (0, 0))],
        out_specs=[tspec, tspec, tspec, pl.BlockSpec((N_EXPERTS, 1), lambda i: (0, 0))],
        out_shape=[tok(I32), tok(F32), tok(I32), jax.ShapeDtypeStruct((N_EXPERTS, 1), F32)],
        scratch_shapes=[pltpu.VMEM((N_EXPERTS, 1), F32)],
        compiler_params=_params(("arbitrary",)),
        name="router",
    )(h2, rw_t, rb_col)


def _slot_kernel(start_ref, eidx_ref, rank_ref, pos_ref):
    e = eidx_ref[...]
    pos = rank_ref[...]
    for j in range(N_EXPERTS):
        pos = pos + jnp.where(e == j, start_ref[j], 0)
    pos_ref[...] = pos


def _slots(pad_start, eidx, rank):
    full = pl.BlockSpec((TOP_K, N_TOK), lambda i, s: (0, 0))
    return pl.pallas_call(
        _slot_kernel,
        grid_spec=pltpu.PrefetchScalarGridSpec(num_scalar_prefetch=1, grid=(1,),
                                               in_specs=[full, full], out_specs=full),
        out_shape=jax.ShapeDtypeStruct((TOP_K, N_TOK), I32),
        compiler_params=_params(("arbitrary",)),
        name="slots",
    )(pad_start, eidx, rank)


def _gather_rows(idx_ref, base, n_rows, src_hbm, dst, sem):
    def issue(i, carry):
        pltpu.make_async_copy(src_hbm.at[pl.ds(idx_ref[base + i], 1), :], dst.at[pl.ds(i, 1), :], sem).start()
        return carry

    lax.fori_loop(0, n_rows, issue, 0, unroll=DMA_UNROLL)


def _wait_rows(n_rows, src_hbm, dst, sem):
    pltpu.make_async_copy(src_hbm.at[pl.ds(0, n_rows), :], dst, sem).wait()


def _expert_kernel(be_ref, first_ref, nused_ref, tok_ref,
                   h_hbm, wg_ref, wu_ref, wd_ref, o_ref, xbuf, wgb, wub, wdb, sem):
    b = pl.program_id(0)
    n_used = nused_ref[0]
    slot = b % 2

    @pl.when(b == 0)
    def _():
        _gather_rows(tok_ref, 0, MOE_ROWS, h_hbm, xbuf.at[0], sem.at[0])

    @pl.when(b < n_used)
    def _():
        _gather_rows(tok_ref, (b + 1) * MOE_ROWS, MOE_ROWS, h_hbm, xbuf.at[1 - slot], sem.at[1 - slot])

        @pl.when(first_ref[b] == 1)
        def _():
            wgb[...] = wg_ref[...].astype(BF16)
            wub[...] = wu_ref[...].astype(BF16)
            wdb[...] = wd_ref[...].astype(BF16)

        _wait_rows(MOE_ROWS, h_hbm, xbuf.at[slot], sem.at[slot])
        x = xbuf[slot].astype(BF16)
        gate = jnp.dot(x, wgb[...], preferred_element_type=F32)
        up = jnp.dot(x, wub[...], preferred_element_type=F32)
        hid = (_silu(gate) * up).astype(BF16)
        o_ref[...] = jnp.dot(hid, wdb[...], preferred_element_type=F32)

    @pl.when(b >= n_used)
    def _():
        @pl.when(b == n_used)
        def _():
            _wait_rows(MOE_ROWS, h_hbm, xbuf.at[slot], sem.at[slot])

        o_ref[...] = jnp.zeros_like(o_ref)


def _experts(blk_expert, blk_first, n_used, tok_of_slot, h2, exp_gate, exp_up, exp_down):
    wspec_in = pl.BlockSpec((None, D_MODEL, D_EXPERT), lambda b, be, fi, nu, tk: (be[b], 0, 0))
    wspec_out = pl.BlockSpec((None, D_EXPERT, D_MODEL), lambda b, be, fi, nu, tk: (be[b], 0, 0))
    grid_spec = pltpu.PrefetchScalarGridSpec(
        num_scalar_prefetch=4,
        grid=(MOE_BLOCKS,),
        in_specs=[pl.BlockSpec(memory_space=pl.ANY), wspec_in, wspec_in, wspec_out],
        out_specs=pl.BlockSpec((MOE_ROWS, D_MODEL), lambda b, be, fi, nu, tk: (b, 0)),
        scratch_shapes=[pltpu.VMEM((2, MOE_ROWS, D_MODEL), F32),
                        pltpu.VMEM((D_MODEL, D_EXPERT), BF16),
                        pltpu.VMEM((D_MODEL, D_EXPERT), BF16),
                        pltpu.VMEM((D_EXPERT, D_MODEL), BF16),
                        pltpu.SemaphoreType.DMA((2,))],
    )
    return pl.pallas_call(
        _expert_kernel,
        grid_spec=grid_spec,
        out_shape=jax.ShapeDtypeStruct((MOE_BLOCKS * MOE_ROWS, D_MODEL), F32),
        compiler_params=_params(("arbitrary",)),
        name="experts",
    )(blk_expert, blk_first, n_used, tok_of_slot, h2, exp_gate, exp_up, exp_down)


def _combine_kernel(pos_ref, y_hbm, wt_ref, h_ref, x1_ref, g2_ref, npost_ref,
                    sg_ref, su_ref, sd_ref, os_ref, op_ref, buf, sem, *, tiles):
    i = pl.program_id(0) * pl.num_programs(1) + pl.program_id(1)
    n_steps = pl.num_programs(0) * pl.num_programs(1)
    slot = i % 2
    tm = tiles.tt
    rows = TOP_K * tm

    @pl.when(i == 0)
    def _():
        _gather_rows(pos_ref, 0, rows, y_hbm, buf.at[0], sem.at[0])

    @pl.when(i + 1 < n_steps)
    def _():
        _gather_rows(pos_ref, (i + 1) * rows, rows, y_hbm, buf.at[1 - slot], sem.at[1 - slot])

    hb = h_ref[...].astype(BF16)
    gate = jnp.dot(hb, sg_ref[...], preferred_element_type=F32)
    up = jnp.dot(hb, su_ref[...], preferred_element_type=F32)
    hid = (_silu(gate) * up).astype(BF16)
    f = jnp.dot(hid, sd_ref[...], preferred_element_type=F32)

    _wait_rows(rows, y_hbm, buf.at[slot], sem.at[slot])
    wt = wt_ref[...]
    for j in range(TOP_K):
        f = f + buf[slot, pl.ds(j * tm, tm), :] * wt[:, j:j + 1]
    out = x1_ref[...] + g2_ref[...] * (_rms(f) * npost_ref[...])

    def store(is_sample):
        (os_ref if is_sample else op_ref)[...] = out

    _on_owner(tiles.is_sample(), store)


def _combine(pos_tiles, y_sorted, wsel_tiles, h2, x1, mod3, n_post, sg_bf, su_bf, sd_bf):
    tiles = _ColumnTiles(COMBINE_ROWS)
    tm = tiles.tt
    row = tiles.tokens(D_MODEL)
    grid_spec = pltpu.PrefetchScalarGridSpec(
        num_scalar_prefetch=1,
        grid=tiles.grid,
        in_specs=[pl.BlockSpec(memory_space=pl.ANY),
                  pl.BlockSpec((tm, TOP_K), lambda j, b, p: (j * CHAIN_BATCH + b, 0)),
                  row, row, tiles.mod(5),
                  tiles.const((1, D_MODEL)), tiles.const((D_MODEL, D_EXPERT)),
                  tiles.const((D_MODEL, D_EXPERT)), tiles.const((D_EXPERT, D_MODEL))],
        out_specs=[tiles.sample_major(D_MODEL), tiles.prompt_major(D_MODEL)],
        scratch_shapes=[pltpu.VMEM((2, TOP_K * tm, D_MODEL), F32), pltpu.SemaphoreType.DMA((2,))],
    )
    return pl.pallas_call(
        functools.partial(_combine_kernel, tiles=tiles),
        grid_spec=grid_spec,
        out_shape=[jax.ShapeDtypeStruct((N_SAMPLE // SEQ_SAMPLE, SEQ_SAMPLE, D_MODEL), F32),
                   jax.ShapeDtypeStruct((N_PROMPT // SEQ_PROMPT, SEQ_PROMPT, D_MODEL), F32)],
        compiler_params=_params(("arbitrary", "arbitrary")),
        name="combine",
    )(pos_tiles, y_sorted, wsel_tiles, h2, x1, mod3, n_post, sg_bf, su_bf, sd_bf)


def _pad_rows(w, rows):
    return jnp.pad(w, ((0, rows - w.shape[0]), (0, 0)))


def kernel(x_prompt, x_sample, state_fwd, state_bwd, c, c_ctx, ada_w, ada_b, norm_pre_mix, norm_post_mix, norm_pre_ffn, norm_post_ffn, w_in, conv_w, w_out_conv, decay_w0, decay_w2, iclr_a0, iclr_a2, gate_g2, k_k, k_a, r_k, lnx_w, lnx_b, w_out_rwkv, w_o, router_w, router_bias, exp_gate, exp_up, exp_down, sh_gate, sh_up, sh_down):
    cond = jnp.concatenate([c_ctx[None, :], c, jnp.zeros((N_COND - 1 - c.shape[0], D_MODEL), F32)], axis=0)
    mod3 = _ada_table(cond, ada_w[0], ada_b).reshape(N_COND, 1, 6 * D_MODEL)

    h = _prenorm(x_sample, x_prompt, norm_pre_mix, mod3).reshape(N_TOK, D_MODEL)
    w_in_bf = w_in[0].astype(BF16)
    z = _conv_branch(h, w_in_bf, conv_w[0])
    rkv = _matmul(h, w_in_bf, col0=3 * W_BRANCH, n_cols=3 * W_BRANCH, tn=512, out_dtype=F32, name="proj_rkv")
    c0 = 6 * W_BRANCH
    pad_cols = lambda lo: jnp.pad(w_in_bf[:, lo:lo + 96], ((0, 0), (0, LORA_PAD - 96)))
    w_lora = jnp.concatenate([pad_cols(c0), pad_cols(c0 + 96), pad_cols(c0 + 192), pad_cols(c0 + 288),
                              w_in_bf[:, c0 + 384:c0 + 640]], axis=1)
    lora = _matmul(h, w_lora, col0=0, n_cols=LORA_W, tn=LORA_W, out_dtype=F32, name="proj_lora")
    gates = _matmul(h, w_in_bf[:, c0 + 640:], col0=0, n_cols=2 * D_MODEL, tn=512,
                    out_dtype=BF16, act="sigmoid", name="proj_gates")

    wd2 = jnp.stack([_pad_rows(decay_w2[0, 0], LORA_PAD), _pad_rows(decay_w2[0, 1], LORA_PAD)]).astype(BF16)
    wa2 = jnp.stack([_pad_rows(iclr_a2[0, 0], LORA_PAD), _pad_rows(iclr_a2[0, 1], LORA_PAD)]).astype(BF16)
    wa, g = _lora_stage(lora, wd2, wa2, gate_g2[0].astype(BF16), decay_w0[0], iclr_a0[0])

    kk_l = _head_param_to_chain_layout(k_k[0])
    ka_l = _head_param_to_chain_layout(k_a[0])
    rk_l = _head_param_to_chain_layout(r_k[0].reshape(-1))
    lw_l = _head_param_to_chain_layout(lnx_w[0])
    lb_l = _head_param_to_chain_layout(lnx_b[0])

    rkv3 = rkv.reshape(TIME_ROWS, CHAIN_BATCH, 3 * W_BRANCH)
    wa3 = wa.reshape(TIME_ROWS, CHAIN_BATCH, 4 * W_BRANCH)

    def run_scan(row0, seq, s0_f, s0_b):
        s0 = jnp.stack([_state_to_chain_layout(s0_f), _state_to_chain_layout(s0_b)])
        y, bon, s_fin = _scan(rkv3, wa3, row0, seq, s0, kk_l, ka_l, rk_l)
        return _scan_post(y, bon, lw_l, lb_l).reshape(-1, W_BRANCH), s_fin

    zero_state = jnp.zeros((N_PROMPT // SEQ_PROMPT, N_HEADS, HEAD, HEAD), F32)
    yb_s, _ = run_scan(0, SEQ_SAMPLE, state_fwd[:, 0], state_bwd[:, 0])
    yb_p, s_fin = run_scan(SEQ_SAMPLE, SEQ_PROMPT, zero_state, zero_state)

    merged = _merge(z, yb_s, yb_p, g, gates, w_out_conv[0].astype(BF16), w_out_rwkv[0].astype(BF16))
    x1, h2 = _outproj(merged.reshape(TIME_ROWS, CHAIN_BATCH * D_MODEL), w_o[0].astype(BF16),
                      x_sample, x_prompt, mod3, norm_post_mix, norm_pre_ffn)

    h2_flat = h2.reshape(N_TOK, D_MODEL)
    eidx, wsel, rank, counts = _router(h2_flat, router_w[0].T, router_bias[0][:, None])
    counts = counts[:, 0].astype(I32)
    padded = (counts + MOE_ROWS - 1) // MOE_ROWS * MOE_ROWS
    pad_end = jnp.cumsum(padded)
    pad_start = pad_end - padded
    pos = _slots(pad_start, eidx, rank)
    tok_ids = jnp.broadcast_to(jnp.arange(N_TOK, dtype=I32)[None, :], (TOP_K, N_TOK))
    tok_of_slot = jnp.zeros((MOE_BLOCKS * MOE_ROWS,), I32).at[pos.reshape(-1)].set(
        tok_ids.reshape(-1), unique_indices=True, mode="promise_in_bounds")
    blk_start = jnp.arange(MOE_BLOCKS, dtype=I32) * MOE_ROWS
    blk_expert = jnp.minimum(jnp.sum(pad_end[None, :] <= blk_start[:, None], axis=1), N_EXPERTS - 1).astype(I32)
    n_used = (pad_end[-1] // MOE_ROWS).astype(I32).reshape(1)
    blk_first = (blk_start == pad_start[blk_expert]).astype(I32)
    y_sorted = _experts(blk_expert, blk_first, n_used, tok_of_slot, h2_flat, exp_gate[0], exp_up[0], exp_down[0])

    def tile_order(a, perm):
        a = a.reshape(TOP_K, TIME_ROWS // COMBINE_ROWS, COMBINE_ROWS, CHAIN_BATCH)
        return jnp.transpose(a, perm)

    pos_tiles = tile_order(pos, (1, 3, 0, 2)).reshape(-1)
    wsel_tiles = tile_order(wsel, (1, 3, 2, 0)).reshape(N_TOK, TOP_K)
    out_s, out_p = _combine(pos_tiles, y_sorted, wsel_tiles, h2, x1, mod3, norm_post_ffn,
                            sh_gate[0].astype(BF16), sh_up[0].astype(BF16), sh_down[0].astype(BF16))

    new_f = _state_from_chain_layout(s_fin[0])[:, None]
    new_b = _state_from_chain_layout(s_fin[1])[:, None]
    return (out_p, out_s, new_f, new_b)
```

```python
import functools
import math

import jax
import jax.numpy as jnp
from jax import lax
from jax.experimental import pallas as pl
from jax.experimental.pallas import tpu as pltpu

F32 = jnp.float32
BF16 = jnp.bfloat16
I32 = jnp.int32

D_MODEL = 2048
N_SAMPLE = 8 * 1024
SEQ_SAMPLE = 1024
N_PROMPT = 16 * 256
SEQ_PROMPT = 256
N_TOK = N_SAMPLE + N_PROMPT
GRID_W = 64
W_BRANCH = 1024
HEAD = 64
N_HEADS = 16
N_COND = 16
LORA_PAD = 128
LORA_W = 4 * LORA_PAD + 256
N_EXPERTS = 64
TOP_K = 8
N_GROUPS = 8
TOPK_GROUPS = 4
D_EXPERT = 512
ROUTED_SCALE = 2.5
NORM_EPS = 1e-6
GN_EPS = 64e-5
LANES = 128
CHAIN_BATCH = 8
TIME_ROWS = N_TOK // CHAIN_BATCH
CONV_HALO = 16
CTX_ROW = 8
SLAB = (16, 128)
MOE_ROWS = 256
MOE_BLOCKS = N_TOK * TOP_K // MOE_ROWS + N_EXPERTS
DISPATCH_TOKENS = 256
COMBINE_TIME_ROWS = 16
DMA_UNROLL = 8
VMEM_LIMIT = 56 * 1024 * 1024


def _params(sem, vmem=VMEM_LIMIT):
    return pltpu.CompilerParams(dimension_semantics=sem, vmem_limit_bytes=vmem)


def _split_specs(tm, width):
    n_s = N_SAMPLE // tm
    return (pl.BlockSpec((tm, width), lambda i, *_: (jnp.minimum(i, n_s - 1), 0)),
            pl.BlockSpec((tm, width), lambda i, *_: (jnp.maximum(i - n_s, 0), 0)))


def _on_owner(is_sample, fn):
    pl.when(is_sample)(lambda: fn(True))
    pl.when(jnp.logical_not(is_sample))(lambda: fn(False))


class _ColumnTiles:
    def __init__(self, tt):
        self.tt = tt
        self.n_s = SEQ_SAMPLE // tt
        self.per_group = SEQ_PROMPT // tt
        self.grid = (TIME_ROWS // tt, CHAIN_BATCH)

    def is_sample(self):
        return pl.program_id(0) < self.n_s

    def tokens(self, width):
        return pl.BlockSpec((self.tt, width), lambda j, b, *_: (j, b))

    def sample_major(self, width):
        n_s = self.n_s
        return pl.BlockSpec((None, self.tt, width), lambda j, b, *_: (
            jnp.where(j < n_s, b, CHAIN_BATCH - 1), jnp.minimum(j, n_s - 1), 0))

    def prompt_major(self, width):
        n_s, per = self.n_s, self.per_group
        return pl.BlockSpec((None, self.tt, width), lambda j, b, *_: (
            jnp.where(j < n_s, 0, (j - n_s) // per * CHAIN_BATCH + b),
            jnp.where(j < n_s, 0, (j - n_s) % per), 0))

    def mod(self, chunk):
        n_s = self.n_s
        return pl.BlockSpec((None, 1, D_MODEL), lambda j, b, *_: (jnp.where(j < n_s, b, CTX_ROW), 0, chunk))

    def const(self, shape):
        return pl.BlockSpec(shape, lambda j, b, *_: (0,) * len(shape))


class _TimeTiles:
    def __init__(self, tt):
        self.tt = tt
        self.tm = tt * CHAIN_BATCH
        self.n_s = SEQ_SAMPLE // tt
        self.per_group = SEQ_PROMPT // tt
        self.grid = (TIME_ROWS // tt,)

    def is_sample(self):
        return pl.program_id(0) < self.n_s

    def tokens(self, width):
        return pl.BlockSpec((self.tm, width), lambda i, *_: (i, 0))

    def tokens3(self, width):
        return pl.BlockSpec((self.tt, CHAIN_BATCH, width), lambda i, *_: (i, 0, 0))

    def slabs3(self):
        return pl.BlockSpec((self.tm,) + SLAB, lambda i, *_: (i, 0, 0))

    def slabs4(self):
        return pl.BlockSpec((self.tt, CHAIN_BATCH) + SLAB, lambda i, *_: (i, 0, 0, 0))

    def sample_major(self, width):
        n_s = self.n_s
        return pl.BlockSpec((CHAIN_BATCH, self.tt, width), lambda i, *_: (0, jnp.minimum(i, n_s - 1), 0))

    def prompt_major(self, width):
        n_s, per = self.n_s, self.per_group
        return pl.BlockSpec((CHAIN_BATCH, self.tt, width), lambda i, *_: (
            jnp.where(i < n_s, 0, (i - n_s) // per), jnp.where(i < n_s, 0, (i - n_s) % per), 0))

    def mod(self, chunk):
        n_s = self.n_s
        return pl.BlockSpec((CHAIN_BATCH, D_MODEL), lambda i, *_: (jnp.where(i < n_s, 0, 1), chunk))

    def const(self, shape):
        return pl.BlockSpec(shape, lambda i, *_: (0,) * len(shape))


def _mod_rows(ref, is_sample):
    return ref[...] if is_sample else jnp.broadcast_to(ref[0:1, :], ref.shape)


def _from_slabs(ref):
    return jnp.concatenate([ref[:, j, :] for j in range(SLAB[0])], axis=1)


def _to_slabs(ref, value):
    for j in range(SLAB[0]):
        ref[:, j, :] = value[:, j * SLAB[1]:(j + 1) * SLAB[1]]


def _rms(x):
    return x * lax.rsqrt(jnp.mean(x * x, axis=-1, keepdims=True) + NORM_EPS)


def _silu(x):
    return x * jax.nn.sigmoid(x)


def _ada_kernel(c_ref, w_ref, b_ref, o_ref):
    s = _silu(c_ref[...]).astype(BF16)
    o_ref[...] = jnp.dot(s, w_ref[...].astype(BF16), preferred_element_type=F32) + b_ref[...]


def _ada_table(cond, ada_w, ada_b):
    tn = 1536
    return pl.pallas_call(
        _ada_kernel,
        grid=(6 * D_MODEL // tn,),
        in_specs=[pl.BlockSpec((N_COND, D_MODEL), lambda j: (0, 0)),
                  pl.BlockSpec((D_MODEL, tn), lambda j: (0, j)),
                  pl.BlockSpec((1, tn), lambda j: (0, j))],
        out_specs=pl.BlockSpec((N_COND, tn), lambda j: (0, j)),
        out_shape=jax.ShapeDtypeStruct((N_COND, 6 * D_MODEL), F32),
        compiler_params=_params(("arbitrary",)),
        name="ada_table",
    )(cond, ada_w, ada_b)


def _prenorm_kernel(xs_ref, xp_ref, g_ref, sh_ref, sc_ref, o_ref, *, tiles):
    def run(is_sample):
        y = _rms((xs_ref if is_sample else xp_ref)[...]) * g_ref[...]
        o_ref[...] = (y * (1.0 + sc_ref[...]) + sh_ref[...]).astype(o_ref.dtype)

    _on_owner(tiles.is_sample(), run)


def _prenorm(x_sample, x_prompt, gain, mod3):
    tiles = _ColumnTiles(256)
    return pl.pallas_call(
        functools.partial(_prenorm_kernel, tiles=tiles),
        grid=tiles.grid,
        in_specs=[tiles.sample_major(D_MODEL), tiles.prompt_major(D_MODEL),
                  tiles.const((1, D_MODEL)), tiles.mod(0), tiles.mod(1)],
        out_specs=tiles.tokens(D_MODEL),
        out_shape=jax.ShapeDtypeStruct((TIME_ROWS, CHAIN_BATCH * D_MODEL), BF16),
        compiler_params=_params(("arbitrary", "arbitrary")),
        name="prenorm",
    )(x_sample, x_prompt, gain, mod3, mod3)


def _mm_kernel(a_ref, w_ref, o_ref, *, act):
    acc = jnp.dot(a_ref[...], w_ref[...], preferred_element_type=F32)
    if act == "sigmoid":
        acc = jax.nn.sigmoid(acc)
    o_ref[...] = acc.astype(o_ref.dtype)


def _matmul(a, w, *, col0, n_cols, tn, out_dtype, act=None, tm=1024, name="matmul"):
    m, k = a.shape
    off = col0 // tn
    return pl.pallas_call(
        functools.partial(_mm_kernel, act=act),
        grid=(m // tm, n_cols // tn),
        in_specs=[pl.BlockSpec((tm, k), lambda i, j: (i, 0)),
                  pl.BlockSpec((k, tn), lambda i, j: (0, j + off))],
        out_specs=pl.BlockSpec((tm, tn), lambda i, j: (i, j)),
        out_shape=jax.ShapeDtypeStruct((m, n_cols), out_dtype),
        compiler_params=_params(("arbitrary", "arbitrary")),
        name=name,
    )(a, w)


def _convproj_kernel(hp_ref, h_ref, hn_ref, wb_ref, wc_ref, wx_ref, cw_ref, o_ref, *, tm):
    h = h_ref[...]
    h_ext = jnp.concatenate([hp_ref[...], h, hn_ref[...]], axis=0)
    cb = jnp.dot(h, wb_ref[...], preferred_element_type=F32)
    u_ext = (jnp.dot(h_ext, wc_ref[...], preferred_element_type=F32)
             * jnp.dot(h_ext, wx_ref[...], preferred_element_type=F32))
    u = u_ext[CONV_HALO:CONV_HALO + tm]
    u_prev = u_ext[CONV_HALO - CHAIN_BATCH:CONV_HALO - CHAIN_BATCH + tm]
    u_next = u_ext[CONV_HALO + CHAIN_BATCH:CONV_HALO + CHAIN_BATCH + tm]
    seg = jnp.where(pl.program_id(0) < N_SAMPLE // tm, GRID_W, SEQ_PROMPT)
    row = lax.broadcasted_iota(I32, u.shape, 0)
    pos = (pl.program_id(0) * (tm // CHAIN_BATCH) + row // CHAIN_BATCH) & (seg - 1)
    u_prev = jnp.where(pos == 0, 0.0, u_prev)
    u_next = jnp.where(pos == seg - 1, 0.0, u_next)
    cw = cw_ref[...]
    conv = cw[0:1, :] * u_prev + cw[1:2, :] * u + cw[2:3, :] * u_next
    o_ref[...] = (cb * conv).astype(o_ref.dtype)


def _conv_branch(h, w_in_bf, conv_w):
    tm, tn = 1024, 256
    nb = W_BRANCH // tn
    per = tm // CONV_HALO
    return pl.pallas_call(
        functools.partial(_convproj_kernel, tm=tm),
        grid=(N_TOK // tm, nb),
        in_specs=[pl.BlockSpec((CONV_HALO, D_MODEL), lambda i, j: (jnp.maximum(i * per - 1, 0), 0)),
                  pl.BlockSpec((tm, D_MODEL), lambda i, j: (i, 0)),
                  pl.BlockSpec((CONV_HALO, D_MODEL),
                               lambda i, j: (jnp.minimum((i + 1) * per, N_TOK // CONV_HALO - 1), 0)),
                  pl.BlockSpec((D_MODEL, tn), lambda i, j: (0, j)),
                  pl.BlockSpec((D_MODEL, tn), lambda i, j: (0, j + nb)),
                  pl.BlockSpec((D_MODEL, tn), lambda i, j: (0, j + 2 * nb)),
                  pl.BlockSpec((3, tn), lambda i, j: (0, j))],
        out_specs=pl.BlockSpec((tm, tn), lambda i, j: (i, j)),
        out_shape=jax.ShapeDtypeStruct((N_TOK, W_BRANCH), BF16),
        compiler_params=_params(("arbitrary", "arbitrary")),
        name="conv_branch",
    )(h, h, h, w_in_bf, w_in_bf, w_in_bf, conv_w)


def _lora_kernel(x_ref, wd_ref, wa_ref, wg_ref, w0_ref, a0_ref, wa_out_ref, g_ref):
    def group(i):
        return x_ref[:, i * LORA_PAD:(i + 1) * LORA_PAD]

    def decay(wl):
        return jnp.exp(-jax.nn.sigmoid(wl) * math.exp(-0.5))

    for d in range(2):
        dl = jnp.dot(jnp.tanh(group(d)).astype(BF16), wd_ref[d], preferred_element_type=F32)
        wa_out_ref[:, d * W_BRANCH:(d + 1) * W_BRANCH] = decay(w0_ref[d:d + 1, :] + dl)
        al = jnp.dot(group(2 + d).astype(BF16), wa_ref[d], preferred_element_type=F32)
        wa_out_ref[:, (2 + d) * W_BRANCH:(3 + d) * W_BRANCH] = jax.nn.sigmoid(a0_ref[d:d + 1, :] + al)
    gl = jax.nn.sigmoid(x_ref[:, 4 * LORA_PAD:]).astype(BF16)
    g_ref[...] = jnp.dot(gl, wg_ref[...], preferred_element_type=F32)


def _lora_stage(lora, wd2, wa2, wg2, w0, a0):
    tm = 512
    return pl.pallas_call(
        _lora_kernel,
        grid=(N_TOK // tm,),
        in_specs=[pl.BlockSpec((tm, LORA_W), lambda i: (i, 0)),
                  pl.BlockSpec((2, LORA_PAD, W_BRANCH), lambda i: (0, 0, 0)),
                  pl.BlockSpec((2, LORA_PAD, W_BRANCH), lambda i: (0, 0, 0)),
                  pl.BlockSpec((256, W_BRANCH), lambda i: (0, 0)),
                  pl.BlockSpec((2, W_BRANCH), lambda i: (0, 0)),
                  pl.BlockSpec((2, W_BRANCH), lambda i: (0, 0))],
        out_specs=[pl.BlockSpec((tm, 4 * W_BRANCH), lambda i: (i, 0)),
                   pl.BlockSpec((tm, W_BRANCH), lambda i: (i, 0))],
        out_shape=[jax.ShapeDtypeStruct((N_TOK, 4 * W_BRANCH), F32),
                   jax.ShapeDtypeStruct((N_TOK, W_BRANCH), F32)],
        compiler_params=_params(("arbitrary",)),
        name="lora_stage",
    )(lora, wd2, wa2, wg2, w0, a0)


def _chain_tiles(x_ref, t0):
    both = (x_ref[t0], x_ref[t0 + 1])
    rows = [both[tl][:, p * LANES:(p + 1) * LANES] for tl in range(2) for p in range(CHAIN_BATCH)]
    sq = jnp.concatenate(rows, axis=0).T
    top, bot = sq[:HEAD], sq[HEAD:]
    low = lax.broadcasted_iota(I32, (HEAD, LANES), 1) < HEAD
    return (jnp.where(low, top, pltpu.roll(bot, HEAD, 1)),
            jnp.where(low, pltpu.roll(top, HEAD, 1), bot))


def _scan_kernel(r_ref, k_ref, v_ref, w_ref, a_ref, s0_ref, kk_ref, ka_ref, rk_ref,
                 y_ref, bon_ref, sf_ref, s_ref, ops_ref, *, tc):
    d = pl.program_id(0)
    c = pl.program_id(2)

    @pl.when(c == 0)
    def _():
        s_ref[...] = s0_ref[...]

    def pair(i, carry):
        t0 = 2 * jnp.where(d == 0, i, tc // 2 - 1 - i)
        r2, k2, v2 = _chain_tiles(r_ref, t0), _chain_tiles(k_ref, t0), _chain_tiles(v_ref, t0)
        w2, a2 = _chain_tiles(w_ref, t0), _chain_tiles(a_ref, t0)
        for tl in range(2):
            kk = k2[tl] * kk_ref[...]
            kk = kk * lax.rsqrt(jnp.sum(kk * kk, axis=0, keepdims=True) + 1e-12)
            kd = k2[tl] * (1.0 + (a2[tl] - 1.0) * ka_ref[...])
            for q, val in enumerate((kk, kk * a2[tl], kd, w2[tl], r2[tl], v2[tl])):
                ops_ref[tl, q] = val
            bon_ref[t0 + tl] = jnp.sum(r2[tl] * kd * rk_ref[...], axis=0, keepdims=True) * v2[tl]

        for s in range(2):
            tl = jnp.where(d == 0, s, 1 - s)
            vt = ops_ref[tl, 5]
            sa = jnp.zeros((HEAD, LANES), F32)
            for j in range(HEAD):
                sa = sa + s_ref[j] * ops_ref[tl, 0, pl.ds(j, 1), :]
            y = jnp.zeros((HEAD, LANES), F32)
            for j in range(HEAD):
                sn = (s_ref[j] * ops_ref[tl, 3, pl.ds(j, 1), :] - sa * ops_ref[tl, 1, pl.ds(j, 1), :]
                      + vt * ops_ref[tl, 2, pl.ds(j, 1), :])
                s_ref[j] = sn
                y = y + sn * ops_ref[tl, 4, pl.ds(j, 1), :]
            y_ref[t0 + tl] = y
        return carry

    lax.fori_loop(0, tc // 2, pair, 0)

    @pl.when(c == pl.num_programs(2) - 1)
    def _():
        sf_ref[...] = s_ref[...]


def _scan(rkv3, wa3, row0, seq, s0, kk_l, ka_l, rk_l, *, tc=32):
    g_n = s0.shape[1]
    nc = seq // tc

    def tchunk(d, c):
        return jnp.where(d == 0, c, nc - 1 - c)

    def tok(col):
        return pl.BlockSpec((tc, CHAIN_BATCH, W_BRANCH),
                            lambda d, g, c: (row0 // tc + g * nc + tchunk(d, c), 0, col))

    def tok_dir(col):
        return pl.BlockSpec((tc, CHAIN_BATCH, W_BRANCH),
                            lambda d, g, c: (row0 // tc + g * nc + tchunk(d, c), 0, col + d))

    chain = pl.BlockSpec((None, None, tc, HEAD, LANES), lambda d, g, c: (d, g, tchunk(d, c), 0, 0))
    state = pl.BlockSpec((None, None, HEAD, HEAD, LANES), lambda d, g, c: (d, g, 0, 0, 0))
    par = pl.BlockSpec((HEAD, LANES), lambda d, g, c: (0, 0))
    seq_shape = jax.ShapeDtypeStruct((2, g_n, seq, HEAD, LANES), F32)
    return pl.pallas_call(
        functools.partial(_scan_kernel, tc=tc),
        grid=(2, g_n, nc),
        in_specs=[tok(0), tok(1), tok(2), tok_dir(0), tok_dir(2), state, par, par, par],
        out_specs=[chain, chain, state],
        out_shape=[seq_shape, seq_shape, jax.ShapeDtypeStruct((2, g_n, HEAD, HEAD, LANES), F32)],
        scratch_shapes=[pltpu.VMEM((HEAD, HEAD, LANES), F32), pltpu.VMEM((2, 6, HEAD, LANES), F32)],
        compiler_params=_params(("arbitrary", "arbitrary", "arbitrary")),
        name="wkv7_scan",
    )(rkv3, rkv3, rkv3, wa3, wa3, s0, kk_l, ka_l, rk_l)


def _scan_post_kernel(y_ref, bon_ref, lw_ref, lb_ref, o_ref, *, tc):
    low = lax.broadcasted_iota(I32, (HEAD, LANES), 1) < HEAD

    def pair(i, carry):
        t0 = 2 * i
        tiles = []
        for tl in range(2):
            ys = y_ref[0, t0 + tl] + y_ref[1, t0 + tl]
            dev = ys - jnp.mean(ys, axis=0, keepdims=True)
            var = jnp.mean(dev * dev, axis=0, keepdims=True)
            yn = dev * lax.rsqrt(var + GN_EPS) * lw_ref[...] + lb_ref[...]
            tiles.append(yn + bon_ref[0, t0 + tl] + bon_ref[1, t0 + tl])
        top = jnp.where(low, tiles[0], pltpu.roll(tiles[1], HEAD, 1))
        bot = jnp.where(low, pltpu.roll(tiles[0], HEAD, 1), tiles[1])
        sq = jnp.concatenate([top, bot], axis=0).T
        for tl in range(2):
            for p in range(CHAIN_BATCH):
                r0 = tl * HEAD + p * CHAIN_BATCH
                o_ref[t0 + tl, :, pl.ds(p * LANES, LANES)] = sq[r0:r0 + CHAIN_BATCH, :]
        return carry

    lax.fori_loop(0, tc // 2, pair, 0)


def _scan_post(y, bon, lw_l, lb_l):
    _, g_n, seq = y.shape[:3]
    tc = 64
    nc = seq // tc
    both = pl.BlockSpec((2, None, tc, HEAD, LANES), lambda g, c: (0, g, c, 0, 0))
    par = pl.BlockSpec((HEAD, LANES), lambda g, c: (0, 0))
    return pl.pallas_call(
        functools.partial(_scan_post_kernel, tc=tc),
        grid=(g_n, nc),
        in_specs=[both, both, par, par],
        out_specs=pl.BlockSpec((tc, CHAIN_BATCH, W_BRANCH), lambda g, c: (g * nc + c, 0, 0)),
        out_shape=jax.ShapeDtypeStruct((g_n * seq, CHAIN_BATCH, W_BRANCH), F32),
        compiler_params=_params(("arbitrary", "arbitrary")),
        name="scan_post",
    )(y, bon, lw_l, lb_l)


def _state_to_chain_layout(s):
    g_n = s.shape[0] // CHAIN_BATCH
    s = s.reshape(g_n, CHAIN_BATCH, N_HEADS // 2, 2, HEAD, HEAD)
    return jnp.transpose(s, (0, 5, 4, 3, 2, 1)).reshape(g_n, HEAD, HEAD, LANES)


def _state_from_chain_layout(s):
    g_n = s.shape[0]
    s = s.reshape(g_n, HEAD, HEAD, 2, N_HEADS // 2, CHAIN_BATCH)
    return jnp.transpose(s, (0, 5, 4, 3, 2, 1)).reshape(g_n * CHAIN_BATCH, N_HEADS, HEAD, HEAD)


def _head_param_to_chain_layout(p):
    p = jnp.transpose(p.reshape(N_HEADS // 2, 2, HEAD), (1, 0, 2))
    p = jnp.broadcast_to(p[:, :, None, :], (2, N_HEADS // 2, CHAIN_BATCH, HEAD))
    return p.reshape(LANES, HEAD).T


def _merge_kernel(z_ref, ybs_ref, ybp_ref, g_ref, ga_ref, gb_ref, wc_ref, wr_ref, o_ref, *, tm):
    def run(is_sample):
        y_a = jnp.dot(z_ref[...], wc_ref[...], preferred_element_type=F32)
        yb = ((ybs_ref if is_sample else ybp_ref)[...] * g_ref[...]).astype(BF16)
        y_b = jnp.dot(yb, wr_ref[...], preferred_element_type=F32)
        o_ref[...] = (ga_ref[...].astype(F32) * y_a + gb_ref[...].astype(F32) * y_b).astype(o_ref.dtype)

    _on_owner(pl.program_id(0) < N_SAMPLE // tm, run)


def _merge(z, yb_s, yb_p, g, gates, w_conv_bf, w_rwkv_bf):
    tm = 512
    row = lambda w: pl.BlockSpec((tm, w), lambda i: (i, 0))
    return pl.pallas_call(
        functools.partial(_merge_kernel, tm=tm),
        grid=(N_TOK // tm,),
        in_specs=[row(W_BRANCH), *_split_specs(tm, W_BRANCH), row(W_BRANCH),
                  pl.BlockSpec((tm, D_MODEL), lambda i: (i, 0)),
                  pl.BlockSpec((tm, D_MODEL), lambda i: (i, 1)),
                  pl.BlockSpec((W_BRANCH, D_MODEL), lambda i: (0, 0)),
                  pl.BlockSpec((W_BRANCH, D_MODEL), lambda i: (0, 0))],
        out_specs=row(D_MODEL),
        out_shape=jax.ShapeDtypeStruct((N_TOK, D_MODEL), BF16),
        compiler_params=_params(("arbitrary",)),
        name="merge",
    )(z, yb_s, yb_p, g, gates, gates, w_conv_bf, w_rwkv_bf)


def _outproj_kernel(m_ref, wo_ref, xs_ref, xp_ref, g1_ref, sh2_ref, sc2_ref, npost_ref, npre_ref,
                    x1_ref, h2_ref, out_scr, *, tiles):
    out = jnp.dot(m_ref[...], wo_ref[...], preferred_element_type=F32)
    out_scr[...] = out.reshape(out_scr.shape)

    def run(is_sample):
        x_ref = xs_ref if is_sample else xp_ref
        g1, sh2, sc2 = (_mod_rows(r, is_sample) for r in (g1_ref, sh2_ref, sc2_ref))
        for b in range(CHAIN_BATCH):
            x1 = x_ref[b] + g1[b:b + 1, :] * (_rms(out_scr[:, b, :]) * npost_ref[...])
            x1_ref[:, b, :] = x1
            h2 = (_rms(x1) * npre_ref[...]) * (1.0 + sc2[b:b + 1, :]) + sh2[b:b + 1, :]
            for j in range(SLAB[0]):
                h2_ref[:, b, j, :] = h2[:, j * SLAB[1]:(j + 1) * SLAB[1]]

    _on_owner(tiles.is_sample(), run)


def _outproj(merged, w_o_bf, x_sample, x_prompt, mod, n_post, n_pre):
    tiles = _TimeTiles(32)
    vec = tiles.const((1, D_MODEL))
    return pl.pallas_call(
        functools.partial(_outproj_kernel, tiles=tiles),
        grid=tiles.grid,
        in_specs=[tiles.tokens(D_MODEL), tiles.const((D_MODEL, D_MODEL)),
                  tiles.sample_major(D_MODEL), tiles.prompt_major(D_MODEL),
                  tiles.mod(2), tiles.mod(3), tiles.mod(4), vec, vec],
        out_specs=[tiles.tokens3(D_MODEL), tiles.slabs4()],
        out_shape=[jax.ShapeDtypeStruct((TIME_ROWS, CHAIN_BATCH, D_MODEL), F32),
                   jax.ShapeDtypeStruct((TIME_ROWS, CHAIN_BATCH) + SLAB, F32)],
        scratch_shapes=[pltpu.VMEM((tiles.tt, CHAIN_BATCH, D_MODEL), F32)],
        compiler_params=_params(("arbitrary",)),
        name="outproj",
    )(merged, w_o_bf, x_sample, x_prompt, mod, mod, mod, n_post, n_pre)


def _first_index_of_max(x, axis, n):
    m = jnp.max(x, axis=axis, keepdims=True)
    idx = lax.broadcasted_iota(I32, x.shape, axis).astype(F32)
    first = jnp.min(jnp.where(x == m, idx, float(n)), axis=axis, keepdims=True)
    return m, idx, first


def _router_kernel(h_ref, rw_ref, rb_ref, eidx_ref, wsel_ref, rank_ref, cnt_ref, base_ref, *, tm):
    @pl.when(pl.program_id(0) == 0)
    def _():
        base_ref[...] = jnp.zeros_like(base_ref)

    logits = lax.dot_general(rw_ref[...], _from_slabs(h_ref), (((1,), (1,)), ((), ())),
                             precision=lax.Precision.HIGHEST, preferred_element_type=F32)
    scores = jax.nn.sigmoid(logits)
    biased = scores + rb_ref[...]
    neg = -jnp.inf

    per_group = N_EXPERTS // N_GROUPS
    grp = biased.reshape(N_GROUPS, per_group, tm)
    m1, idx, first = _first_index_of_max(grp, 1, per_group)
    m2 = jnp.max(jnp.where(idx == first, neg, grp), axis=1, keepdims=True)
    gscore = (m1 + m2).reshape(N_GROUPS, tm)

    gsel = jnp.zeros((N_GROUPS, tm), F32)
    for _ in range(TOPK_GROUPS):
        _, gidx, gfirst = _first_index_of_max(gscore, 0, N_GROUPS)
        hit = gidx == gfirst
        gsel = jnp.where(hit, 1.0, gsel)
        gscore = jnp.where(hit, neg, gscore)
    emask = jnp.broadcast_to(gsel[:, None, :], (N_GROUPS, per_group, tm)).reshape(N_EXPERTS, tm)

    cand = jnp.where(emask > 0.5, biased, neg)
    mem = jnp.zeros((N_EXPERTS, tm), F32)
    picks = []
    for _ in range(TOP_K):
        _, eidx, efirst = _first_index_of_max(cand, 0, N_EXPERTS)
        hit = eidx == efirst
        mem = jnp.where(hit, 1.0, mem)
        cand = jnp.where(hit, neg, cand)
        picks.append((efirst, hit))

    s_i = lax.broadcasted_iota(I32, (tm, tm), 0)
    t_i = lax.broadcasted_iota(I32, (tm, tm), 1)
    upper = (s_i <= t_i).astype(BF16)
    incl = jnp.dot(mem.astype(BF16), upper, preferred_element_type=F32)
    rank = base_ref[...] + incl - mem
    base_ref[...] = base_ref[...] + jnp.sum(mem, axis=1, keepdims=True)
    cnt_ref[...] = base_ref[...]

    wsum = jnp.zeros((1, tm), F32)
    wrows = []
    for j, (efirst, hit) in enumerate(picks):
        wj = jnp.sum(jnp.where(hit, scores, 0.0), axis=0, keepdims=True)
        wrows.append(wj)
        wsum = wsum + wj
        eidx_ref[pl.ds(j, 1), :] = efirst.astype(I32)
        rank_ref[pl.ds(j, 1), :] = jnp.sum(jnp.where(hit, rank, 0.0), axis=0, keepdims=True).astype(I32)
    for j, wj in enumerate(wrows):
        wsel_ref[pl.ds(j, 1), :] = wj / wsum * ROUTED_SCALE


def _router(h2, rw_t, rb_col):
    tm = 256
    tok = lambda dt: jax.ShapeDtypeStruct((TOP_K, N_TOK), dt)
    tspec = pl.BlockSpec((TOP_K, tm), lambda i: (0, i))
    return pl.pallas_call(
        functools.partial(_router_kernel, tm=tm),
        grid=(N_TOK // tm,),
        in_specs=[pl.BlockSpec((tm,) + SLAB, lambda i: (i, 0, 0)),
                  pl.BlockSpec((N_EXPERTS, D_MODEL), lambda i: (0, 0)),
                  pl.BlockSpec((N_EXPERTS, 1), lambda i: (0, 0))],
        out_specs=[tspec, tspec, tspec, pl.BlockSpec((N_EXPERTS, 1), lambda i: (0, 0))],
        out_shape=[tok(I32), tok(F32), tok(I32), jax.ShapeDtypeStruct((N_EXPERTS, 1), F32)],
        scratch_shapes=[pltpu.VMEM((N_EXPERTS, 1), F32)],
        compiler_params=_params(("arbitrary",)),
        name="router",
    )(h2, rw_t, rb_col)


def _slot_kernel(start_ref, eidx_ref, rank_ref, pos_ref):
    e = eidx_ref[...]
    pos = rank_ref[...]
    for j in range(N_EXPERTS):
        pos = pos + jnp.where(e == j, start_ref[j], 0)
    pos_ref[...] = pos


def _slots(pad_start, eidx, rank):
    full = pl.BlockSpec((TOP_K, N_TOK), lambda i, s: (0, 0))
    return pl.pallas_call(
        _slot_kernel,
        grid_spec=pltpu.PrefetchScalarGridSpec(num_scalar_prefetch=1, grid=(1,),
                                               in_specs=[full, full], out_specs=full),
        out_shape=jax.ShapeDtypeStruct((TOP_K, N_TOK), I32),
        compiler_params=_params(("arbitrary",)),
        name="slots",
    )(pad_start, eidx, rank)


def _dispatch_kernel(pos_ref, zstart_ref, zlen_ref, nused_ref, h_hbm, xs_hbm, zbuf, sem, zsem):
    i = pl.program_id(0)
    slot = i % 2

    def unused_block_copy(b):
        return pltpu.make_async_copy(zbuf, xs_hbm.at[pl.ds(b * MOE_ROWS, MOE_ROWS)], zsem)

    def pad_copies(e):
        off = zstart_ref[e]
        out = []
        for bit in (128, 64, 32, 16, 8, 4, 2, 1):
            cond = (zlen_ref[e] & bit) != 0
            out.append((cond, pltpu.make_async_copy(zbuf.at[pl.ds(0, bit)], xs_hbm.at[pl.ds(off, bit)], zsem)))
            off = off + jnp.where(cond, bit, 0)
        return out

    @pl.when(i == 0)
    def _():
        zbuf[...] = jnp.zeros_like(zbuf)

        def start(e, carry):
            for cond, cp in pad_copies(e):
                pl.when(cond)(cp.start)
            return carry

        def wait(e, carry):
            for cond, cp in pad_copies(e):
                pl.when(cond)(cp.wait)
            return carry

        lax.fori_loop(0, N_EXPERTS, start, 0)
        lax.fori_loop(nused_ref[0], MOE_BLOCKS, lambda b, c: (unused_block_copy(b).start(), c)[1], 0)
        lax.fori_loop(0, N_EXPERTS, wait, 0)
        lax.fori_loop(nused_ref[0], MOE_BLOCKS, lambda b, c: (unused_block_copy(b).wait(), c)[1], 0)

    def issue(r, carry):
        t = i * DISPATCH_TOKENS + r
        for k in range(TOP_K):
            pltpu.make_async_copy(h_hbm.at[pl.ds(t, 1)], xs_hbm.at[pl.ds(pos_ref[k * N_TOK + t], 1)],
                                  sem.at[slot]).start()
        return carry

    lax.fori_loop(0, DISPATCH_TOKENS, issue, 0)

    def wait_step(s):
        n = DISPATCH_TOKENS * TOP_K
        pltpu.make_async_copy(h_hbm.at[pl.ds(0, n)], xs_hbm.at[pl.ds(0, n)], sem.at[s]).wait()

    pl.when(i > 0)(lambda: wait_step(1 - slot))
    pl.when(i == pl.num_programs(0) - 1)(lambda: wait_step(slot))


def _dispatch(pos_flat, zstart, zlen, n_used, h2r):
    grid_spec = pltpu.PrefetchScalarGridSpec(
        num_scalar_prefetch=4,
        grid=(N_TOK // DISPATCH_TOKENS,),
        in_specs=[pl.BlockSpec(memory_space=pl.ANY)],
        out_specs=pl.BlockSpec(memory_space=pl.ANY),
        scratch_shapes=[pltpu.VMEM((MOE_ROWS,) + SLAB, F32), pltpu.SemaphoreType.DMA((2,)),
                        pltpu.SemaphoreType.DMA(())],
    )
    return pl.pallas_call(
        _dispatch_kernel,
        grid_spec=grid_spec,
        out_shape=jax.ShapeDtypeStruct((MOE_BLOCKS * MOE_ROWS,) + SLAB, F32),
        compiler_params=_params(("arbitrary",)),
        name="dispatch",
    )(pos_flat, zstart, zlen, n_used, h2r)


def _expert_kernel(be_ref, nused_ref, x_ref, wg_ref, wu_ref, wd_ref, o_ref, wgb, wub, wdb):
    b = pl.program_id(0)

    @pl.when(b < nused_ref[0])
    def _():
        @pl.when(jnp.logical_or(b == 0, be_ref[b] != be_ref[jnp.maximum(b - 1, 0)]))
        def _():
            wgb[...] = wg_ref[...].astype(BF16)
            wub[...] = wu_ref[...].astype(BF16)
            wdb[...] = wd_ref[...].astype(BF16)

        x = _from_slabs(x_ref).astype(BF16)
        gate = jnp.dot(x, wgb[...], preferred_element_type=F32)
        up = jnp.dot(x, wub[...], preferred_element_type=F32)
        hid = (_silu(gate) * up).astype(BF16)
        _to_slabs(o_ref, jnp.dot(hid, wdb[...], preferred_element_type=F32))

    @pl.when(b >= nused_ref[0])
    def _():
        o_ref[...] = jnp.zeros_like(o_ref)


def _experts(blk_expert, n_used, xs, exp_gate, exp_up, exp_down):
    wspec_in = pl.BlockSpec((None, D_MODEL, D_EXPERT), lambda b, be, nu: (be[b], 0, 0))
    wspec_out = pl.BlockSpec((None, D_EXPERT, D_MODEL), lambda b, be, nu: (be[b], 0, 0))
    grid_spec = pltpu.PrefetchScalarGridSpec(
        num_scalar_prefetch=2,
        grid=(MOE_BLOCKS,),
        in_specs=[pl.BlockSpec((MOE_ROWS,) + SLAB, lambda b, be, nu: (jnp.minimum(b, nu[0] - 1), 0, 0)),
                  wspec_in, wspec_in, wspec_out],
        out_specs=pl.BlockSpec((MOE_ROWS,) + SLAB, lambda b, be, nu: (b, 0, 0)),
        scratch_shapes=[pltpu.VMEM((D_MODEL, D_EXPERT), BF16),
                        pltpu.VMEM((D_MODEL, D_EXPERT), BF16),
                        pltpu.VMEM((D_EXPERT, D_MODEL), BF16)],
    )
    return pl.pallas_call(
        _expert_kernel,
        grid_spec=grid_spec,
        out_shape=jax.ShapeDtypeStruct((MOE_BLOCKS * MOE_ROWS,) + SLAB, F32),
        compiler_params=_params(("arbitrary",)),
        name="experts",
    )(blk_expert, n_used, xs, exp_gate, exp_up, exp_down)


def _gather_rows(idx_ref, base, n_rows, src_hbm, dst, sem):
    def issue(i, carry):
        pltpu.make_async_copy(src_hbm.at[pl.ds(idx_ref[base + i], 1)], dst.at[pl.ds(i, 1)], sem).start()
        return carry

    lax.fori_loop(0, n_rows, issue, 0, unroll=DMA_UNROLL)


def _wait_rows(n_rows, src_hbm, dst, sem):
    pltpu.make_async_copy(src_hbm.at[pl.ds(0, n_rows)], dst, sem).wait()


def _combine_kernel(pos_ref, y_hbm, wt_ref, h_ref, x1_ref, g2_ref, npost_ref,
                    sg_ref, su_ref, sd_ref, os_ref, op_ref, buf, out_scr, sem, *, tiles):
    i = pl.program_id(0)
    slot = i % 2
    tm = tiles.tm
    rows = TOP_K * tm

    @pl.when(i == 0)
    def _():
        _gather_rows(pos_ref, 0, rows, y_hbm, buf.at[0], sem.at[0])

    @pl.when(i + 1 < pl.num_programs(0))
    def _():
        _gather_rows(pos_ref, (i + 1) * rows, rows, y_hbm, buf.at[1 - slot], sem.at[1 - slot])

    hb = _from_slabs(h_ref).astype(BF16)
    gate = jnp.dot(hb, sg_ref[...], preferred_element_type=F32)
    up = jnp.dot(hb, su_ref[...], preferred_element_type=F32)
    hid = (_silu(gate) * up).astype(BF16)
    shared = jnp.dot(hid, sd_ref[...], preferred_element_type=F32)

    _wait_rows(rows, y_hbm, buf.at[slot], sem.at[slot])
    wt = wt_ref[...]
    wk = [jnp.broadcast_to(wt[:, k:k + 1], (tm, SLAB[1])) for k in range(TOP_K)]
    pieces = []
    for j in range(SLAB[0]):
        acc = shared[:, j * SLAB[1]:(j + 1) * SLAB[1]]
        for k in range(TOP_K):
            acc = acc + buf[slot, pl.ds(k * tm, tm), j, :] * wk[k]
        pieces.append(acc)
    f = jnp.concatenate(pieces, axis=1)
    fn = (_rms(f) * npost_ref[...]).reshape(out_scr.shape)

    def store(is_sample):
        out_scr[...] = x1_ref[...] + _mod_rows(g2_ref, is_sample)[None, :, :] * fn
        o_ref = os_ref if is_sample else op_ref
        for b in range(CHAIN_BATCH):
            o_ref[b] = out_scr[:, b, :]

    _on_owner(tiles.is_sample(), store)


def _combine(pos_tiles, y_sorted, wsel_t, h2r, x1, mod, n_post, sg_bf, su_bf, sd_bf):
    tiles = _TimeTiles(COMBINE_TIME_ROWS)
    grid_spec = pltpu.PrefetchScalarGridSpec(
        num_scalar_prefetch=1,
        grid=tiles.grid,
        in_specs=[pl.BlockSpec(memory_space=pl.ANY), tiles.tokens(TOP_K),
                  tiles.slabs3(), tiles.tokens3(D_MODEL), tiles.mod(5),
                  tiles.const((1, D_MODEL)), tiles.const((D_MODEL, D_EXPERT)),
                  tiles.const((D_MODEL, D_EXPERT)), tiles.const((D_EXPERT, D_MODEL))],
        out_specs=[tiles.sample_major(D_MODEL), tiles.prompt_major(D_MODEL)],
        scratch_shapes=[pltpu.VMEM((2, TOP_K * tiles.tm) + SLAB, F32),
                        pltpu.VMEM((tiles.tt, CHAIN_BATCH, D_MODEL), F32),
                        pltpu.SemaphoreType.DMA((2,))],
    )
    return pl.pallas_call(
        functools.partial(_combine_kernel, tiles=tiles),
        grid_spec=grid_spec,
        out_shape=[jax.ShapeDtypeStruct((N_SAMPLE // SEQ_SAMPLE, SEQ_SAMPLE, D_MODEL), F32),
                   jax.ShapeDtypeStruct((N_PROMPT // SEQ_PROMPT, SEQ_PROMPT, D_MODEL), F32)],
        compiler_params=_params(("arbitrary",)),
        name="combine",
    )(pos_tiles, y_sorted, wsel_t, h2r, x1, mod, n_post, sg_bf, su_bf, sd_bf)


def _pad_rows(w, rows):
    return jnp.pad(w, ((0, rows - w.shape[0]), (0, 0)))


def kernel(x_prompt, x_sample, state_fwd, state_bwd, c, c_ctx, ada_w, ada_b, norm_pre_mix, norm_post_mix, norm_pre_ffn, norm_post_ffn, w_in, conv_w, w_out_conv, decay_w0, decay_w2, iclr_a0, iclr_a2, gate_g2, k_k, k_a, r_k, lnx_w, lnx_b, w_out_rwkv, w_o, router_w, router_bias, exp_gate, exp_up, exp_down, sh_gate, sh_up, sh_down):
    cond = jnp.concatenate([c, c_ctx[None, :], jnp.zeros((N_COND - 1 - c.shape[0], D_MODEL), F32)], axis=0)
    mod = _ada_table(cond, ada_w[0], ada_b)
    mod3 = mod.reshape(N_COND, 1, 6 * D_MODEL)

    h = _prenorm(x_sample, x_prompt, norm_pre_mix, mod3).reshape(N_TOK, D_MODEL)
    w_in_bf = w_in[0].astype(BF16)
    z = _conv_branch(h, w_in_bf, conv_w[0])
    rkv = _matmul(h, w_in_bf, col0=3 * W_BRANCH, n_cols=3 * W_BRANCH, tn=512, out_dtype=F32, name="proj_rkv")
    c0 = 6 * W_BRANCH
    pad_cols = lambda lo: jnp.pad(w_in_bf[:, lo:lo + 96], ((0, 0), (0, LORA_PAD - 96)))
    w_lora = jnp.concatenate([pad_cols(c0), pad_cols(c0 + 96), pad_cols(c0 + 192), pad_cols(c0 + 288),
                              w_in_bf[:, c0 + 384:c0 + 640]], axis=1)
    lora = _matmul(h, w_lora, col0=0, n_cols=LORA_W, tn=LORA_W, out_dtype=F32, name="proj_lora")
    gates = _matmul(h, w_in_bf[:, c0 + 640:], col0=0, n_cols=2 * D_MODEL, tn=512,
                    out_dtype=BF16, act="sigmoid", name="proj_gates")

    wd2 = jnp.stack([_pad_rows(decay_w2[0, 0], LORA_PAD), _pad_rows(decay_w2[0, 1], LORA_PAD)]).astype(BF16)
    wa2 = jnp.stack([_pad_rows(iclr_a2[0, 0], LORA_PAD), _pad_rows(iclr_a2[0, 1], LORA_PAD)]).astype(BF16)
    wa, g = _lora_stage(lora, wd2, wa2, gate_g2[0].astype(BF16), decay_w0[0], iclr_a0[0])

    kk_l = _head_param_to_chain_layout(k_k[0])
    ka_l = _head_param_to_chain_layout(k_a[0])
    rk_l = _head_param_to_chain_layout(r_k[0].reshape(-1))
    lw_l = _head_param_to_chain_layout(lnx_w[0])
    lb_l = _head_param_to_chain_layout(lnx_b[0])

    rkv3 = rkv.reshape(TIME_ROWS, CHAIN_BATCH, 3 * W_BRANCH)
    wa3 = wa.reshape(TIME_ROWS, CHAIN_BATCH, 4 * W_BRANCH)

    def run_scan(row0, seq, s0_f, s0_b):
        s0 = jnp.stack([_state_to_chain_layout(s0_f), _state_to_chain_layout(s0_b)])
        y, bon, s_fin = _scan(rkv3, wa3, row0, seq, s0, kk_l, ka_l, rk_l)
        return _scan_post(y, bon, lw_l, lb_l).reshape(-1, W_BRANCH), s_fin

    zero_state = jnp.zeros((N_PROMPT // SEQ_PROMPT, N_HEADS, HEAD, HEAD), F32)
    yb_s, _ = run_scan(0, SEQ_SAMPLE, state_fwd[:, 0], state_bwd[:, 0])
    yb_p, s_fin = run_scan(SEQ_SAMPLE, SEQ_PROMPT, zero_state, zero_state)

    merged = _merge(z, yb_s, yb_p, g, gates, w_out_conv[0].astype(BF16), w_out_rwkv[0].astype(BF16))
    x1, h2 = _outproj(merged, w_o[0].astype(BF16), x_sample, x_prompt, mod, norm_post_mix, norm_pre_ffn)
    h2r = h2.reshape((N_TOK,) + SLAB)

    eidx, wsel, rank, counts = _router(h2r, router_w[0].T, router_bias[0][:, None])
    counts = counts[:, 0].astype(I32)
    padded = (counts + MOE_ROWS - 1) // MOE_ROWS * MOE_ROWS
    pad_end = jnp.cumsum(padded)
    pad_start = pad_end - padded
    pos = _slots(pad_start, eidx, rank)
    n_used = (pad_end[-1] // MOE_ROWS).astype(I32).reshape(1)
    xs = _dispatch(pos.reshape(-1), pad_start + counts, padded - counts, n_used, h2r)
    blk_start = jnp.arange(MOE_BLOCKS, dtype=I32) * MOE_ROWS
    blk_expert = jnp.minimum(jnp.sum(pad_end[None, :] <= blk_start[:, None], axis=1), N_EXPERTS - 1).astype(I32)
    y_sorted = _experts(blk_expert, n_used, xs, exp_gate[0], exp_up[0], exp_down[0])

    tm_c = COMBINE_TIME_ROWS * CHAIN_BATCH
    pos_tiles = jnp.transpose(pos.reshape(TOP_K, N_TOK // tm_c, tm_c), (1, 0, 2)).reshape(-1)
    out_s, out_p = _combine(pos_tiles, y_sorted, wsel.T, h2r, x1, mod, norm_post_ffn,
                            sh_gate[0].astype(BF16), sh_up[0].astype(BF16), sh_down[0].astype(BF16))

    new_f = _state_from_chain_layout(s_fin[0])[:, None]
    new_b = _state_from_chain_layout(s_fin[1])[:, None]
    return (out_p, out_s, new_f, new_b)
```

```python
import functools
import math

import jax
import jax.numpy as jnp
from jax import lax
from jax.experimental import pallas as pl
from jax.experimental.pallas import tpu as pltpu

F32 = jnp.float32
BF16 = jnp.bfloat16
I32 = jnp.int32

D_MODEL = 2048
N_SAMPLE = 8 * 1024
SEQ_SAMPLE = 1024
N_PROMPT = 16 * 256
SEQ_PROMPT = 256
N_TOK = N_SAMPLE + N_PROMPT
GRID_W = 64
W_BRANCH = 1024
HEAD = 64
N_HEADS = 16
N_COND = 16
LORA_PAD = 128
LORA_W = 4 * LORA_PAD + 256
N_EXPERTS = 64
TOP_K = 8
N_GROUPS = 8
TOPK_GROUPS = 4
D_EXPERT = 512
ROUTED_SCALE = 2.5
NORM_EPS = 1e-6
GN_EPS = 64e-5
LANES = 128
CHAIN_BATCH = 8
TIME_ROWS = N_TOK // CHAIN_BATCH
CONV_HALO = 16
MOE_ROWS = 512
MOE_BLOCKS = N_TOK * TOP_K // MOE_ROWS + N_EXPERTS
COMBINE_ROWS = 128
DMA_UNROLL = 8
VMEM_LIMIT = 56 * 1024 * 1024


def _params(sem, vmem=VMEM_LIMIT):
    return pltpu.CompilerParams(dimension_semantics=sem, vmem_limit_bytes=vmem)


def _split_specs(tm, width):
    n_s = N_SAMPLE // tm
    return (pl.BlockSpec((tm, width), lambda i, *_: (jnp.minimum(i, n_s - 1), 0)),
            pl.BlockSpec((tm, width), lambda i, *_: (jnp.maximum(i - n_s, 0), 0)))


def _on_owner(is_sample, fn):
    pl.when(is_sample)(lambda: fn(True))
    pl.when(jnp.logical_not(is_sample))(lambda: fn(False))


class _ColumnTiles:
    def __init__(self, tt):
        self.tt = tt
        self.n_s = SEQ_SAMPLE // tt
        self.per_group = SEQ_PROMPT // tt
        self.grid = (TIME_ROWS // tt, CHAIN_BATCH)

    def is_sample(self):
        return pl.program_id(0) < self.n_s

    def tokens(self, width):
        return pl.BlockSpec((self.tt, width), lambda j, b, *_: (j, b))

    def sample_major(self, width):
        n_s = self.n_s
        return pl.BlockSpec((None, self.tt, width), lambda j, b, *_: (
            jnp.where(j < n_s, b, CHAIN_BATCH - 1), jnp.minimum(j, n_s - 1), 0))

    def prompt_major(self, width):
        n_s, per = self.n_s, self.per_group
        return pl.BlockSpec((None, self.tt, width), lambda j, b, *_: (
            jnp.where(j < n_s, 0, (j - n_s) // per * CHAIN_BATCH + b),
            jnp.where(j < n_s, 0, (j - n_s) % per), 0))

    def mod(self, chunk):
        n_s = self.n_s
        return pl.BlockSpec((None, 1, D_MODEL), lambda j, b, *_: (jnp.where(j < n_s, 1 + b, 0), 0, chunk))

    def const(self, shape):
        return pl.BlockSpec(shape, lambda j, b, *_: (0,) * len(shape))


def _rms(x):
    return x * lax.rsqrt(jnp.mean(x * x, axis=-1, keepdims=True) + NORM_EPS)


def _silu(x):
    return x * jax.nn.sigmoid(x)


def _ada_kernel(c_ref, w_ref, b_ref, o_ref):
    s = _silu(c_ref[...]).astype(BF16)
    o_ref[...] = jnp.dot(s, w_ref[...].astype(BF16), preferred_element_type=F32) + b_ref[...]


def _ada_table(cond, ada_w, ada_b):
    tn = 1536
    return pl.pallas_call(
        _ada_kernel,
        grid=(6 * D_MODEL // tn,),
        in_specs=[pl.BlockSpec((N_COND, D_MODEL), lambda j: (0, 0)),
                  pl.BlockSpec((D_MODEL, tn), lambda j: (0, j)),
                  pl.BlockSpec((1, tn), lambda j: (0, j))],
        out_specs=pl.BlockSpec((N_COND, tn), lambda j: (0, j)),
        out_shape=jax.ShapeDtypeStruct((N_COND, 6 * D_MODEL), F32),
        compiler_params=_params(("arbitrary",)),
        name="ada_table",
    )(cond, ada_w, ada_b)


def _prenorm_kernel(xs_ref, xp_ref, g_ref, sh_ref, sc_ref, o_ref, *, tiles):
    def run(is_sample):
        y = _rms((xs_ref if is_sample else xp_ref)[...]) * g_ref[...]
        o_ref[...] = (y * (1.0 + sc_ref[...]) + sh_ref[...]).astype(o_ref.dtype)

    _on_owner(tiles.is_sample(), run)


def _prenorm(x_sample, x_prompt, gain, mod3):
    tiles = _ColumnTiles(256)
    return pl.pallas_call(
        functools.partial(_prenorm_kernel, tiles=tiles),
        grid=tiles.grid,
        in_specs=[tiles.sample_major(D_MODEL), tiles.prompt_major(D_MODEL),
                  tiles.const((1, D_MODEL)), tiles.mod(0), tiles.mod(1)],
        out_specs=tiles.tokens(D_MODEL),
        out_shape=jax.ShapeDtypeStruct((TIME_ROWS, CHAIN_BATCH * D_MODEL), BF16),
        compiler_params=_params(("arbitrary", "arbitrary")),
        name="prenorm",
    )(x_sample, x_prompt, gain, mod3, mod3)


def _mm_kernel(a_ref, w_ref, o_ref, *, act):
    acc = jnp.dot(a_ref[...], w_ref[...], preferred_element_type=F32)
    if act == "sigmoid":
        acc = jax.nn.sigmoid(acc)
    o_ref[...] = acc.astype(o_ref.dtype)


def _matmul(a, w, *, col0, n_cols, tn, out_dtype, act=None, tm=1024, name="matmul"):
    m, k = a.shape
    off = col0 // tn
    return pl.pallas_call(
        functools.partial(_mm_kernel, act=act),
        grid=(m // tm, n_cols // tn),
        in_specs=[pl.BlockSpec((tm, k), lambda i, j: (i, 0)),
                  pl.BlockSpec((k, tn), lambda i, j: (0, j + off))],
        out_specs=pl.BlockSpec((tm, tn), lambda i, j: (i, j)),
        out_shape=jax.ShapeDtypeStruct((m, n_cols), out_dtype),
        compiler_params=_params(("arbitrary", "arbitrary")),
        name=name,
    )(a, w)


def _convproj_kernel(hp_ref, h_ref, hn_ref, wb_ref, wc_ref, wx_ref, cw_ref, o_ref, *, tm):
    h = h_ref[...]
    h_ext = jnp.concatenate([hp_ref[...], h, hn_ref[...]], axis=0)
    cb = jnp.dot(h, wb_ref[...], preferred_element_type=F32)
    u_ext = (jnp.dot(h_ext, wc_ref[...], preferred_element_type=F32)
             * jnp.dot(h_ext, wx_ref[...], preferred_element_type=F32))
    u = u_ext[CONV_HALO:CONV_HALO + tm]
    u_prev = u_ext[CONV_HALO - CHAIN_BATCH:CONV_HALO - CHAIN_BATCH + tm]
    u_next = u_ext[CONV_HALO + CHAIN_BATCH:CONV_HALO + CHAIN_BATCH + tm]
    seg = jnp.where(pl.program_id(0) < N_SAMPLE // tm, GRID_W, SEQ_PROMPT)
    row = lax.broadcasted_iota(I32, u.shape, 0)
    pos = (pl.program_id(0) * (tm // CHAIN_BATCH) + row // CHAIN_BATCH) & (seg - 1)
    u_prev = jnp.where(pos == 0, 0.0, u_prev)
    u_next = jnp.where(pos == seg - 1, 0.0, u_next)
    cw = cw_ref[...]
    conv = cw[0:1, :] * u_prev + cw[1:2, :] * u + cw[2:3, :] * u_next
    o_ref[...] = (cb * conv).astype(o_ref.dtype)


def _conv_branch(h, w_in_bf, conv_w):
    tm, tn = 1024, 256
    nb = W_BRANCH // tn
    per = tm // CONV_HALO
    return pl.pallas_call(
        functools.partial(_convproj_kernel, tm=tm),
        grid=(N_TOK // tm, nb),
        in_specs=[pl.BlockSpec((CONV_HALO, D_MODEL), lambda i, j: (jnp.maximum(i * per - 1, 0), 0)),
                  pl.BlockSpec((tm, D_MODEL), lambda i, j: (i, 0)),
                  pl.BlockSpec((CONV_HALO, D_MODEL),
                               lambda i, j: (jnp.minimum((i + 1) * per, N_TOK // CONV_HALO - 1), 0)),
                  pl.BlockSpec((D_MODEL, tn), lambda i, j: (0, j)),
                  pl.BlockSpec((D_MODEL, tn), lambda i, j: (0, j + nb)),
                  pl.BlockSpec((D_MODEL, tn), lambda i, j: (0, j + 2 * nb)),
                  pl.BlockSpec((3, tn), lambda i, j: (0, j))],
        out_specs=pl.BlockSpec((tm, tn), lambda i, j: (i, j)),
        out_shape=jax.ShapeDtypeStruct((N_TOK, W_BRANCH), BF16),
        compiler_params=_params(("arbitrary", "arbitrary")),
        name="conv_branch",
    )(h, h, h, w_in_bf, w_in_bf, w_in_bf, conv_w)


def _lora_kernel(x_ref, wd_ref, wa_ref, wg_ref, w0_ref, a0_ref, wa_out_ref, g_ref):
    def group(i):
        return x_ref[:, i * LORA_PAD:(i + 1) * LORA_PAD]

    def decay(wl):
        return jnp.exp(-jax.nn.sigmoid(wl) * math.exp(-0.5))

    for d in range(2):
        dl = jnp.dot(jnp.tanh(group(d)).astype(BF16), wd_ref[d], preferred_element_type=F32)
        wa_out_ref[:, d * W_BRANCH:(d + 1) * W_BRANCH] = decay(w0_ref[d:d + 1, :] + dl)
        al = jnp.dot(group(2 + d).astype(BF16), wa_ref[d], preferred_element_type=F32)
        wa_out_ref[:, (2 + d) * W_BRANCH:(3 + d) * W_BRANCH] = jax.nn.sigmoid(a0_ref[d:d + 1, :] + al)
    gl = jax.nn.sigmoid(x_ref[:, 4 * LORA_PAD:]).astype(BF16)
    g_ref[...] = jnp.dot(gl, wg_ref[...], preferred_element_type=F32)


def _lora_stage(lora, wd2, wa2, wg2, w0, a0):
    tm = 512
    return pl.pallas_call(
        _lora_kernel,
        grid=(N_TOK // tm,),
        in_specs=[pl.BlockSpec((tm, LORA_W), lambda i: (i, 0)),
                  pl.BlockSpec((2, LORA_PAD, W_BRANCH), lambda i: (0, 0, 0)),
                  pl.BlockSpec((2, LORA_PAD, W_BRANCH), lambda i: (0, 0, 0)),
                  pl.BlockSpec((256, W_BRANCH), lambda i: (0, 0)),
                  pl.BlockSpec((2, W_BRANCH), lambda i: (0, 0)),
                  pl.BlockSpec((2, W_BRANCH), lambda i: (0, 0))],
        out_specs=[pl.BlockSpec((tm, 4 * W_BRANCH), lambda i: (i, 0)),
                   pl.BlockSpec((tm, W_BRANCH), lambda i: (i, 0))],
        out_shape=[jax.ShapeDtypeStruct((N_TOK, 4 * W_BRANCH), F32),
                   jax.ShapeDtypeStruct((N_TOK, W_BRANCH), F32)],
        compiler_params=_params(("arbitrary",)),
        name="lora_stage",
    )(lora, wd2, wa2, wg2, w0, a0)


def _chain_tiles(x_ref, t0):
    both = (x_ref[t0], x_ref[t0 + 1])
    rows = [both[tl][:, p * LANES:(p + 1) * LANES] for tl in range(2) for p in range(CHAIN_BATCH)]
    sq = jnp.concatenate(rows, axis=0).T
    top, bot = sq[:HEAD], sq[HEAD:]
    low = lax.broadcasted_iota(I32, (HEAD, LANES), 1) < HEAD
    return (jnp.where(low, top, pltpu.roll(bot, HEAD, 1)),
            jnp.where(low, pltpu.roll(top, HEAD, 1), bot))


def _scan_kernel(r_ref, k_ref, v_ref, w_ref, a_ref, s0_ref, kk_ref, ka_ref, rk_ref,
                 y_ref, bon_ref, sf_ref, s_ref, ops_a, ops_b, *, tc):
    d = pl.program_id(0)
    c = pl.program_id(2)
    n_pairs = tc // 2

    @pl.when(c == 0)
    def _():
        s_ref[...] = s0_ref[...]

    def first_row(pair_idx):
        return 2 * jnp.where(d == 0, pair_idx, n_pairs - 1 - pair_idx)

    def prepare(pair_idx, ops_ref):
        t0 = first_row(pair_idx)
        r2, k2, v2 = _chain_tiles(r_ref, t0), _chain_tiles(k_ref, t0), _chain_tiles(v_ref, t0)
        w2, a2 = _chain_tiles(w_ref, t0), _chain_tiles(a_ref, t0)
        for tl in range(2):
            kk = k2[tl] * kk_ref[...]
            kk = kk * lax.rsqrt(jnp.sum(kk * kk, axis=0, keepdims=True) + 1e-12)
            kd = k2[tl] * (1.0 + (a2[tl] - 1.0) * ka_ref[...])
            for q, val in enumerate((kk, kk * a2[tl], kd, w2[tl], r2[tl], v2[tl])):
                ops_ref[tl, q] = val
            bon_ref[t0 + tl] = jnp.sum(r2[tl] * kd * rk_ref[...], axis=0, keepdims=True) * v2[tl]

    def recur(pair_idx, ops_ref):
        t0 = first_row(pair_idx)
        for s in range(2):
            tl = jnp.where(d == 0, s, 1 - s)
            vt = ops_ref[tl, 5]
            sa = jnp.zeros((HEAD, LANES), F32)
            for j in range(HEAD):
                sa = sa + s_ref[j] * ops_ref[tl, 0, pl.ds(j, 1), :]
            y = jnp.zeros((HEAD, LANES), F32)
            for j in range(HEAD):
                sn = (s_ref[j] * ops_ref[tl, 3, pl.ds(j, 1), :] - sa * ops_ref[tl, 1, pl.ds(j, 1), :]
                      + vt * ops_ref[tl, 2, pl.ds(j, 1), :])
                s_ref[j] = sn
                y = y + sn * ops_ref[tl, 4, pl.ds(j, 1), :]
            y_ref[t0 + tl] = y

    prepare(0, ops_a)

    def two_pairs(i, carry):
        prepare(2 * i + 1, ops_b)
        recur(2 * i, ops_a)
        prepare(jnp.minimum(2 * i + 2, n_pairs - 1), ops_a)
        recur(2 * i + 1, ops_b)
        return carry

    lax.fori_loop(0, n_pairs // 2, two_pairs, 0)

    @pl.when(c == pl.num_programs(2) - 1)
    def _():
        sf_ref[...] = s_ref[...]


def _scan(rkv3, wa3, row0, seq, s0, kk_l, ka_l, rk_l, *, tc=32):
    g_n = s0.shape[1]
    nc = seq // tc

    def tchunk(d, c):
        return jnp.where(d == 0, c, nc - 1 - c)

    def tok(col):
        return pl.BlockSpec((tc, CHAIN_BATCH, W_BRANCH),
                            lambda d, g, c: (row0 // tc + g * nc + tchunk(d, c), 0, col))

    def tok_dir(col):
        return pl.BlockSpec((tc, CHAIN_BATCH, W_BRANCH),
                            lambda d, g, c: (row0 // tc + g * nc + tchunk(d, c), 0, col + d))

    chain = pl.BlockSpec((None, None, tc, HEAD, LANES), lambda d, g, c: (d, g, tchunk(d, c), 0, 0))
    state = pl.BlockSpec((None, None, HEAD, HEAD, LANES), lambda d, g, c: (d, g, 0, 0, 0))
    par = pl.BlockSpec((HEAD, LANES), lambda d, g, c: (0, 0))
    seq_shape = jax.ShapeDtypeStruct((2, g_n, seq, HEAD, LANES), F32)
    ops = pltpu.VMEM((2, 6, HEAD, LANES), F32)
    return pl.pallas_call(
        functools.partial(_scan_kernel, tc=tc),
        grid=(2, g_n, nc),
        in_specs=[tok(0), tok(1), tok(2), tok_dir(0), tok_dir(2), state, par, par, par],
        out_specs=[chain, chain, state],
        out_shape=[seq_shape, seq_shape, jax.ShapeDtypeStruct((2, g_n, HEAD, HEAD, LANES), F32)],
        scratch_shapes=[pltpu.VMEM((HEAD, HEAD, LANES), F32), ops, ops],
        compiler_params=_params(("arbitrary", "arbitrary", "arbitrary")),
        name="wkv7_scan",
    )(rkv3, rkv3, rkv3, wa3, wa3, s0, kk_l, ka_l, rk_l)


def _scan_post_kernel(y_ref, bon_ref, lw_ref, lb_ref, o_ref, *, tc):
    low = lax.broadcasted_iota(I32, (HEAD, LANES), 1) < HEAD

    def pair(i, carry):
        t0 = 2 * i
        tiles = []
        for tl in range(2):
            ys = y_ref[0, t0 + tl] + y_ref[1, t0 + tl]
            dev = ys - jnp.mean(ys, axis=0, keepdims=True)
            var = jnp.mean(dev * dev, axis=0, keepdims=True)
            yn = dev * lax.rsqrt(var + GN_EPS) * lw_ref[...] + lb_ref[...]
            tiles.append(yn + bon_ref[0, t0 + tl] + bon_ref[1, t0 + tl])
        top = jnp.where(low, tiles[0], pltpu.roll(tiles[1], HEAD, 1))
        bot = jnp.where(low, pltpu.roll(tiles[0], HEAD, 1), tiles[1])
        sq = jnp.concatenate([top, bot], axis=0).T
        for tl in range(2):
            for p in range(CHAIN_BATCH):
                r0 = tl * HEAD + p * CHAIN_BATCH
                o_ref[t0 + tl, :, pl.ds(p * LANES, LANES)] = sq[r0:r0 + CHAIN_BATCH, :]
        return carry

    lax.fori_loop(0, tc // 2, pair, 0)


def _scan_post(y, bon, lw_l, lb_l):
    _, g_n, seq = y.shape[:3]
    tc = 64
    nc = seq // tc
    both = pl.BlockSpec((2, None, tc, HEAD, LANES), lambda g, c: (0, g, c, 0, 0))
    par = pl.BlockSpec((HEAD, LANES), lambda g, c: (0, 0))
    return pl.pallas_call(
        functools.partial(_scan_post_kernel, tc=tc),
        grid=(g_n, nc),
        in_specs=[both, both, par, par],
        out_specs=pl.BlockSpec((tc, CHAIN_BATCH, W_BRANCH), lambda g, c: (g * nc + c, 0, 0)),
        out_shape=jax.ShapeDtypeStruct((g_n * seq, CHAIN_BATCH, W_BRANCH), F32),
        compiler_params=_params(("arbitrary", "arbitrary")),
        name="scan_post",
    )(y, bon, lw_l, lb_l)


def _state_to_chain_layout(s):
    g_n = s.shape[0] // CHAIN_BATCH
    s = s.reshape(g_n, CHAIN_BATCH, N_HEADS // 2, 2, HEAD, HEAD)
    return jnp.transpose(s, (0, 5, 4, 3, 2, 1)).reshape(g_n, HEAD, HEAD, LANES)


def _state_from_chain_layout(s):
    g_n = s.shape[0]
    s = s.reshape(g_n, HEAD, HEAD, 2, N_HEADS // 2, CHAIN_BATCH)
    return jnp.transpose(s, (0, 5, 4, 3, 2, 1)).reshape(g_n * CHAIN_BATCH, N_HEADS, HEAD, HEAD)


def _head_param_to_chain_layout(p):
    p = jnp.transpose(p.reshape(N_HEADS // 2, 2, HEAD), (1, 0, 2))
    p = jnp.broadcast_to(p[:, :, None, :], (2, N_HEADS // 2, CHAIN_BATCH, HEAD))
    return p.reshape(LANES, HEAD).T


def _merge_kernel(z_ref, ybs_ref, ybp_ref, g_ref, ga_ref, gb_ref, wc_ref, wr_ref, o_ref, *, tm):
    def run(is_sample):
        y_a = jnp.dot(z_ref[...], wc_ref[...], preferred_element_type=F32)
        yb = ((ybs_ref if is_sample else ybp_ref)[...] * g_ref[...]).astype(BF16)
        y_b = jnp.dot(yb, wr_ref[...], preferred_element_type=F32)
        o_ref[...] = (ga_ref[...].astype(F32) * y_a + gb_ref[...].astype(F32) * y_b).astype(o_ref.dtype)

    _on_owner(pl.program_id(0) < N_SAMPLE // tm, run)


def _merge(z, yb_s, yb_p, g, gates, w_conv_bf, w_rwkv_bf):
    tm = 512
    row = lambda w: pl.BlockSpec((tm, w), lambda i: (i, 0))
    return pl.pallas_call(
        functools.partial(_merge_kernel, tm=tm),
        grid=(N_TOK // tm,),
        in_specs=[row(W_BRANCH), *_split_specs(tm, W_BRANCH), row(W_BRANCH),
                  pl.BlockSpec((tm, D_MODEL), lambda i: (i, 0)),
                  pl.BlockSpec((tm, D_MODEL), lambda i: (i, 1)),
                  pl.BlockSpec((W_BRANCH, D_MODEL), lambda i: (0, 0)),
                  pl.BlockSpec((W_BRANCH, D_MODEL), lambda i: (0, 0))],
        out_specs=row(D_MODEL),
        out_shape=jax.ShapeDtypeStruct((N_TOK, D_MODEL), BF16),
        compiler_params=_params(("arbitrary",)),
        name="merge",
    )(z, yb_s, yb_p, g, gates, gates, w_conv_bf, w_rwkv_bf)


def _outproj_kernel(m_ref, wo_ref, xs_ref, xp_ref, g1_ref, sh2_ref, sc2_ref, npost_ref, npre_ref,
                    x1_ref, h2_ref, *, tiles):
    def run(is_sample):
        out = jnp.dot(m_ref[...], wo_ref[...], preferred_element_type=F32)
        x1 = (xs_ref if is_sample else xp_ref)[...] + g1_ref[...] * (_rms(out) * npost_ref[...])
        x1_ref[...] = x1
        h2_ref[...] = (_rms(x1) * npre_ref[...]) * (1.0 + sc2_ref[...]) + sh2_ref[...]

    _on_owner(tiles.is_sample(), run)


def _outproj(merged, w_o_bf, x_sample, x_prompt, mod3, n_post, n_pre):
    tiles = _ColumnTiles(256)
    row = tiles.tokens(D_MODEL)
    vec = tiles.const((1, D_MODEL))
    out = jax.ShapeDtypeStruct((TIME_ROWS, CHAIN_BATCH * D_MODEL), F32)
    return pl.pallas_call(
        functools.partial(_outproj_kernel, tiles=tiles),
        grid=tiles.grid,
        in_specs=[row, tiles.const((D_MODEL, D_MODEL)),
                  tiles.sample_major(D_MODEL), tiles.prompt_major(D_MODEL),
                  tiles.mod(2), tiles.mod(3), tiles.mod(4), vec, vec],
        out_specs=[row, row],
        out_shape=[out, out],
        compiler_params=_params(("arbitrary", "arbitrary")),
        name="outproj",
    )(merged, w_o_bf, x_sample, x_prompt, mod3, mod3, mod3, n_post, n_pre)


def _first_index_of_max(x, axis, n):
    m = jnp.max(x, axis=axis, keepdims=True)
    idx = lax.broadcasted_iota(I32, x.shape, axis).astype(F32)
    first = jnp.min(jnp.where(x == m, idx, float(n)), axis=axis, keepdims=True)
    return m, idx, first


def _router_kernel(h_ref, rw_ref, rb_ref, eidx_ref, wsel_ref, rank_ref, cnt_ref, base_ref, *, tm):
    @pl.when(pl.program_id(0) == 0)
    def _():
        base_ref[...] = jnp.zeros_like(base_ref)

    logits = lax.dot_general(rw_ref[...], h_ref[...], (((1,), (1,)), ((), ())),
                             precision=lax.Precision.HIGHEST, preferred_element_type=F32)
    scores = jax.nn.sigmoid(logits)
    biased = scores + rb_ref[...]
    neg = -jnp.inf

    per_group = N_EXPERTS // N_GROUPS
    grp = biased.reshape(N_GROUPS, per_group, tm)
    m1, idx, first = _first_index_of_max(grp, 1, per_group)
    m2 = jnp.max(jnp.where(idx == first, neg, grp), axis=1, keepdims=True)
    gscore = (m1 + m2).reshape(N_GROUPS, tm)

    gsel = jnp.zeros((N_GROUPS, tm), F32)
    for _ in range(TOPK_GROUPS):
        _, gidx, gfirst = _first_index_of_max(gscore, 0, N_GROUPS)
        hit = gidx == gfirst
        gsel = jnp.where(hit, 1.0, gsel)
        gscore = jnp.where(hit, neg, gscore)
    emask = jnp.broadcast_to(gsel[:, None, :], (N_GROUPS, per_group, tm)).reshape(N_EXPERTS, tm)

    cand = jnp.where(emask > 0.5, biased, neg)
    mem = jnp.zeros((N_EXPERTS, tm), F32)
    picks = []
    for _ in range(TOP_K):
        _, eidx, efirst = _first_index_of_max(cand, 0, N_EXPERTS)
        hit = eidx == efirst
        mem = jnp.where(hit, 1.0, mem)
        cand = jnp.where(hit, neg, cand)
        picks.append((efirst, hit))

    s_i = lax.broadcasted_iota(I32, (tm, tm), 0)
    t_i = lax.broadcasted_iota(I32, (tm, tm), 1)
    upper = (s_i <= t_i).astype(BF16)
    incl = jnp.dot(mem.astype(BF16), upper, preferred_element_type=F32)
    rank = base_ref[...] + incl - mem
    base_ref[...] = base_ref[...] + jnp.sum(mem, axis=1, keepdims=True)
    cnt_ref[...] = base_ref[...]

    wsum = jnp.zeros((1, tm), F32)
    wrows = []
    for j, (efirst, hit) in enumerate(picks):
        wj = jnp.sum(jnp.where(hit, scores, 0.0), axis=0, keepdims=True)
        wrows.append(wj)
        wsum = wsum + wj
        eidx_ref[pl.ds(j, 1), :] = efirst.astype(I32)
        rank_ref[pl.ds(j, 1), :] = jnp.sum(jnp.where(hit, rank, 0.0), axis=0, keepdims=True).astype(I32)
    for j, wj in enumerate(wrows):
        wsel_ref[pl.ds(j, 1), :] = wj / wsum * ROUTED_SCALE


def _router(h2, rw_t, rb_col):
    tm = 256
    tok = lambda dt: jax.ShapeDtypeStruct((TOP_K, N_TOK), dt)
    tspec = pl.BlockSpec((TOP_K, tm), lambda i: (0, i))
    return pl.pallas_call(
        functools.partial(_router_kernel, tm=tm),
        grid=(N_TOK // tm,),
        in_specs=[pl.BlockSpec((tm, D_MODEL), lambda i: (i, 0)),
                  pl.BlockSpec((N_EXPERTS, D_MODEL), lambda i: (0, 0)),
                  pl.BlockSpec((N_EXPERTS, 1), lambda i: (0, 0))],
        out_specs=[tspec, tspec, tspec, pl.BlockSpec((N_EXPERTS, 1), lambda i: (0, 0))],
        out_shape=[tok(I32), tok(F32), tok(I32), jax.ShapeDtypeStruct((N_EXPERTS, 1), F32)],
        scratch_shapes=[pltpu.VMEM((N_EXPERTS, 1), F32)],
        compiler_params=_params(("arbitrary",)),
        name="router",
    )(h2, rw_t, rb_col)


def _slot_kernel(start_ref, eidx_ref, rank_ref, pos_ref):
    e = eidx_ref[...]
    pos = rank_ref[...]
    for j in range(N_EXPERTS):
        pos = pos + jnp.where(e == j, start_ref[j], 0)
    pos_ref[...] = pos


def _slots(pad_start, eidx, rank):
    full = pl.BlockSpec((TOP_K, N_TOK), lambda i, s: (0, 0))
    return pl.pallas_call(
        _slot_kernel,
        grid_spec=pltpu.PrefetchScalarGridSpec(num_scalar_prefetch=1, grid=(1,),
                                               in_specs=[full, full], out_specs=full),
        out_shape=jax.ShapeDtypeStruct((TOP_K, N_TOK), I32),
        compiler_params=_params(("arbitrary",)),
        name="slots",
    )(pad_start, eidx, rank)


def _gather_rows(idx_ref, base, n_rows, src_hbm, dst, sem, straight_line=False):
    def issue(i, carry):
        pltpu.make_async_copy(src_hbm.at[pl.ds(idx_ref[base + i], 1), :], dst.at[pl.ds(i, 1), :], sem).start()
        return carry

    if straight_line:
        for i in range(n_rows):
            issue(i, 0)
    else:
        lax.fori_loop(0, n_rows, issue, 0, unroll=DMA_UNROLL)


def _wait_rows(n_rows, src_hbm, dst, sem):
    pltpu.make_async_copy(src_hbm.at[pl.ds(0, n_rows), :], dst, sem).wait()


def _expert_kernel(be_ref, first_ref, nused_ref, tok_ref,
                   h_hbm, wg_ref, wu_ref, wd_ref, o_ref, xbuf, wgb, wub, wdb, sem):
    b = pl.program_id(0)
    n_used = nused_ref[0]
    slot = b % 2

    @pl.when(b == 0)
    def _():
        _gather_rows(tok_ref, 0, MOE_ROWS, h_hbm, xbuf.at[0], sem.at[0])

    @pl.when(b < n_used)
    def _():
        @pl.when(first_ref[b] == 1)
        def _():
            wgb[...] = wg_ref[...].astype(BF16)
            wub[...] = wu_ref[...].astype(BF16)
            wdb[...] = wd_ref[...].astype(BF16)

        _wait_rows(MOE_ROWS, h_hbm, xbuf.at[slot], sem.at[slot])
        _gather_rows(tok_ref, (b + 1) * MOE_ROWS, MOE_ROWS, h_hbm, xbuf.at[1 - slot], sem.at[1 - slot],
                     straight_line=True)
        x = xbuf[slot].astype(BF16)
        gate = jnp.dot(x, wgb[...], preferred_element_type=F32)
        up = jnp.dot(x, wub[...], preferred_element_type=F32)
        hid = (_silu(gate) * up).astype(BF16)
        o_ref[...] = jnp.dot(hid, wdb[...], preferred_element_type=F32)

    @pl.when(b >= n_used)
    def _():
        @pl.when(b == n_used)
        def _():
            _wait_rows(MOE_ROWS, h_hbm, xbuf.at[slot], sem.at[slot])

        o_ref[...] = jnp.zeros_like(o_ref)


def _experts(blk_expert, blk_first, n_used, tok_of_slot, h2, exp_gate, exp_up, exp_down):
    wspec_in = pl.BlockSpec((None, D_MODEL, D_EXPERT), lambda b, be, fi, nu, tk: (be[b], 0, 0))
    wspec_out = pl.BlockSpec((None, D_EXPERT, D_MODEL), lambda b, be, fi, nu, tk: (be[b], 0, 0))
    grid_spec = pltpu.PrefetchScalarGridSpec(
        num_scalar_prefetch=4,
        grid=(MOE_BLOCKS,),
        in_specs=[pl.BlockSpec(memory_space=pl.ANY), wspec_in, wspec_in, wspec_out],
        out_specs=pl.BlockSpec((MOE_ROWS, D_MODEL), lambda b, be, fi, nu, tk: (b, 0)),
        scratch_shapes=[pltpu.VMEM((2, MOE_ROWS, D_MODEL), F32),
                        pltpu.VMEM((D_MODEL, D_EXPERT), BF16),
                        pltpu.VMEM((D_MODEL, D_EXPERT), BF16),
                        pltpu.VMEM((D_EXPERT, D_MODEL), BF16),
                        pltpu.SemaphoreType.DMA((2,))],
    )
    return pl.pallas_call(
        _expert_kernel,
        grid_spec=grid_spec,
        out_shape=jax.ShapeDtypeStruct((MOE_BLOCKS * MOE_ROWS, D_MODEL), F32),
        compiler_params=_params(("arbitrary",)),
        name="experts",
    )(blk_expert, blk_first, n_used, tok_of_slot, h2, exp_gate, exp_up, exp_down)


def _combine_kernel(pos_ref, y_hbm, wt_ref, h_ref, x1_ref, g2_ref, npost_ref,
                    sg_ref, su_ref, sd_ref, os_ref, op_ref, buf, sem, *, tiles):
    i = pl.program_id(0) * pl.num_programs(1) + pl.program_id(1)
    n_steps = pl.num_programs(0) * pl.num_programs(1)
    slot = i % 2
    tm = tiles.tt
    rows = TOP_K * tm

    @pl.when(i == 0)
    def _():
        _gather_rows(pos_ref, 0, rows, y_hbm, buf.at[0], sem.at[0])

    @pl.when(i + 1 < n_steps)
    def _():
        _gather_rows(pos_ref, (i + 1) * rows, rows, y_hbm, buf.at[1 - slot], sem.at[1 - slot])

    hb = h_ref[...].astype(BF16)
    gate = jnp.dot(hb, sg_ref[...], preferred_element_type=F32)
    up = jnp.dot(hb, su_ref[...], preferred_element_type=F32)
    hid = (_silu(gate) * up).astype(BF16)
    f = jnp.dot(hid, sd_ref[...], preferred_element_type=F32)

    _wait_rows(rows, y_hbm, buf.at[slot], sem.at[slot])
    wt = wt_ref[...]
    for j in range(TOP_K):
        f = f + buf[slot, pl.ds(j * tm, tm), :] * wt[:, j:j + 1]
    out = x1_ref[...] + g2_ref[...] * (_rms(f) * npost_ref[...])

    def store(is_sample):
        (os_ref if is_sample else op_ref)[...] = out

    _on_owner(tiles.is_sample(), store)


def _combine(pos_tiles, y_sorted, wsel_tiles, h2, x1, mod3, n_post, sg_bf, su_bf, sd_bf):
    tiles = _ColumnTiles(COMBINE_ROWS)
    tm = tiles.tt
    row = tiles.tokens(D_MODEL)
    grid_spec = pltpu.PrefetchScalarGridSpec(
        num_scalar_prefetch=1,
        grid=tiles.grid,
        in_specs=[pl.BlockSpec(memory_space=pl.ANY),
                  pl.BlockSpec((tm, TOP_K), lambda j, b, p: (j * CHAIN_BATCH + b, 0)),
                  row, row, tiles.mod(5),
                  tiles.const((1, D_MODEL)), tiles.const((D_MODEL, D_EXPERT)),
                  tiles.const((D_MODEL, D_EXPERT)), tiles.const((D_EXPERT, D_MODEL))],
        out_specs=[tiles.sample_major(D_MODEL), tiles.prompt_major(D_MODEL)],
        scratch_shapes=[pltpu.VMEM((2, TOP_K * tm, D_MODEL), F32), pltpu.SemaphoreType.DMA((2,))],
    )
    return pl.pallas_call(
        functools.partial(_combine_kernel, tiles=tiles),
        grid_spec=grid_spec,
        out_shape=[jax.ShapeDtypeStruct((N_SAMPLE // SEQ_SAMPLE, SEQ_SAMPLE, D_MODEL), F32),
                   jax.ShapeDtypeStruct((N_PROMPT // SEQ_PROMPT, SEQ_PROMPT, D_MODEL), F32)],
        compiler_params=_params(("arbitrary", "arbitrary")),
        name="combine",
    )(pos_tiles, y_sorted, wsel_tiles, h2, x1, mod3, n_post, sg_bf, su_bf, sd_bf)


def _pad_rows(w, rows):
    return jnp.pad(w, ((0, rows - w.shape[0]), (0, 0)))


def kernel(x_prompt, x_sample, state_fwd, state_bwd, c, c_ctx, ada_w, ada_b, norm_pre_mix, norm_post_mix, norm_pre_ffn, norm_post_ffn, w_in, conv_w, w_out_conv, decay_w0, decay_w2, iclr_a0, iclr_a2, gate_g2, k_k, k_a, r_k, lnx_w, lnx_b, w_out_rwkv, w_o, router_w, router_bias, exp_gate, exp_up, exp_down, sh_gate, sh_up, sh_down):
    cond = jnp.concatenate([c_ctx[None, :], c, jnp.zeros((N_COND - 1 - c.shape[0], D_MODEL), F32)], axis=0)
    mod3 = _ada_table(cond, ada_w[0], ada_b).reshape(N_COND, 1, 6 * D_MODEL)

    h = _prenorm(x_sample, x_prompt, norm_pre_mix, mod3).reshape(N_TOK, D_MODEL)
    w_in_bf = w_in[0].astype(BF16)
    z = _conv_branch(h, w_in_bf, conv_w[0])
    rkv = _matmul(h, w_in_bf, col0=3 * W_BRANCH, n_cols=3 * W_BRANCH, tn=512, out_dtype=F32, name="proj_rkv")
    c0 = 6 * W_BRANCH
    pad_cols = lambda lo: jnp.pad(w_in_bf[:, lo:lo + 96], ((0, 0), (0, LORA_PAD - 96)))
    w_lora = jnp.concatenate([pad_cols(c0), pad_cols(c0 + 96), pad_cols(c0 + 192), pad_cols(c0 + 288),
                              w_in_bf[:, c0 + 384:c0 + 640]], axis=1)
    lora = _matmul(h, w_lora, col0=0, n_cols=LORA_W, tn=LORA_W, out_dtype=F32, name="proj_lora")
    gates = _matmul(h, w_in_bf[:, c0 + 640:], col0=0, n_cols=2 * D_MODEL, tn=512,
                    out_dtype=BF16, act="sigmoid", name="proj_gates")

    wd2 = jnp.stack([_pad_rows(decay_w2[0, 0], LORA_PAD), _pad_rows(decay_w2[0, 1], LORA_PAD)]).astype(BF16)
    wa2 = jnp.stack([_pad_rows(iclr_a2[0, 0], LORA_PAD), _pad_rows(iclr_a2[0, 1], LORA_PAD)]).astype(BF16)
    wa, g = _lora_stage(lora, wd2, wa2, gate_g2[0].astype(BF16), decay_w0[0], iclr_a0[0])

    kk_l = _head_param_to_chain_layout(k_k[0])
    ka_l = _head_param_to_chain_layout(k_a[0])
    rk_l = _head_param_to_chain_layout(r_k[0].reshape(-1))
    lw_l = _head_param_to_chain_layout(lnx_w[0])
    lb_l = _head_param_to_chain_layout(lnx_b[0])

    rkv3 = rkv.reshape(TIME_ROWS, CHAIN_BATCH, 3 * W_BRANCH)
    wa3 = wa.reshape(TIME_ROWS, CHAIN_BATCH, 4 * W_BRANCH)

    def run_scan(row0, seq, s0_f, s0_b):
        s0 = jnp.stack([_state_to_chain_layout(s0_f), _state_to_chain_layout(s0_b)])
        y, bon, s_fin = _scan(rkv3, wa3, row0, seq, s0, kk_l, ka_l, rk_l)
        return _scan_post(y, bon, lw_l, lb_l).reshape(-1, W_BRANCH), s_fin

    zero_state = jnp.zeros((N_PROMPT // SEQ_PROMPT, N_HEADS, HEAD, HEAD), F32)
    yb_s, _ = run_scan(0, SEQ_SAMPLE, state_fwd[:, 0], state_bwd[:, 0])
    yb_p, s_fin = run_scan(SEQ_SAMPLE, SEQ_PROMPT, zero_state, zero_state)

    merged = _merge(z, yb_s, yb_p, g, gates, w_out_conv[0].astype(BF16), w_out_rwkv[0].astype(BF16))
    x1, h2 = _outproj(merged.reshape(TIME_ROWS, CHAIN_BATCH * D_MODEL), w_o[0].astype(BF16),
                      x_sample, x_prompt, mod3, norm_post_mix, norm_pre_ffn)

    h2_flat = h2.reshape(N_TOK, D_MODEL)
    eidx, wsel, rank, counts = _router(h2_flat, router_w[0].T, router_bias[0][:, None])
    counts = counts[:, 0].astype(I32)
    padded = (counts + MOE_ROWS - 1) // MOE_ROWS * MOE_ROWS
    pad_end = jnp.cumsum(padded)
    pad_start = pad_end - padded
    pos = _slots(pad_start, eidx, rank)
    tok_ids = jnp.broadcast_to(jnp.arange(N_TOK, dtype=I32)[None, :], (TOP_K, N_TOK))
    tok_of_slot = jnp.zeros((MOE_BLOCKS * MOE_ROWS,), I32).at[pos.reshape(-1)].set(
        tok_ids.reshape(-1), unique_indices=True, mode="promise_in_bounds")
    blk_start = jnp.arange(MOE_BLOCKS, dtype=I32) * MOE_ROWS
    blk_expert = jnp.minimum(jnp.sum(pad_end[None, :] <= blk_start[:, None], axis=1), N_EXPERTS - 1).astype(I32)
    n_used = (pad_end[-1] // MOE_ROWS).astype(I32).reshape(1)
    blk_first = (blk_start == pad_start[blk_expert]).astype(I32)
    y_sorted = _experts(blk_expert, blk_first, n_used, tok_of_slot, h2_flat, exp_gate[0], exp_up[0], exp_down[0])

    def tile_order(a, perm):
        a = a.reshape(TOP_K, TIME_ROWS // COMBINE_ROWS, COMBINE_ROWS, CHAIN_BATCH)
        return jnp.transpose(a, perm)

    pos_tiles = tile_order(pos, (1, 3, 0, 2)).reshape(-1)
    wsel_tiles = tile_order(wsel, (1, 3, 2, 0)).reshape(N_TOK, TOP_K)
    out_s, out_p = _combine(pos_tiles, y_sorted, wsel_tiles, h2, x1, mod3, norm_post_ffn,
                            sh_gate[0].astype(BF16), sh_up[0].astype(BF16), sh_down[0].astype(BF16))

    new_f = _state_from_chain_layout(s_fin[0])[:, None]
    new_b = _state_from_chain_layout(s_fin[1])[:, None]
    return (out_p, out_s, new_f, new_b)
```

```python
import functools
import math

import jax
import jax.numpy as jnp
from jax import lax
from jax.experimental import pallas as pl
from jax.experimental.pallas import tpu as pltpu

F32 = jnp.float32
BF16 = jnp.bfloat16
I32 = jnp.int32

D_MODEL = 2048
N_SAMPLE = 8 * 1024
SEQ_SAMPLE = 1024
N_PROMPT = 16 * 256
SEQ_PROMPT = 256
N_TOK = N_SAMPLE + N_PROMPT
GRID_W = 64
W_BRANCH = 1024
HEAD = 64
N_HEADS = 16
N_COND = 16
LORA_PAD = 128
LORA_W = 4 * LORA_PAD + 256
N_EXPERTS = 64
TOP_K = 8
N_GROUPS = 8
TOPK_GROUPS = 4
D_EXPERT = 512
ROUTED_SCALE = 2.5
NORM_EPS = 1e-6
GN_EPS = 64e-5
LANES = 128
CHAIN_BATCH = 8
TIME_ROWS = N_TOK // CHAIN_BATCH
CONV_HALO = 16
MOE_ROWS = 512
MOE_BLOCKS = N_TOK * TOP_K // MOE_ROWS + N_EXPERTS
COMBINE_ROWS = 128
DMA_UNROLL = 8
VMEM_LIMIT = 56 * 1024 * 1024


def _params(sem, vmem=VMEM_LIMIT):
    return pltpu.CompilerParams(dimension_semantics=sem, vmem_limit_bytes=vmem)


def _split_specs(tm, width):
    n_s = N_SAMPLE // tm
    return (pl.BlockSpec((tm, width), lambda i, *_: (jnp.minimum(i, n_s - 1), 0)),
            pl.BlockSpec((tm, width), lambda i, *_: (jnp.maximum(i - n_s, 0), 0)))


def _on_owner(is_sample, fn):
    pl.when(is_sample)(lambda: fn(True))
    pl.when(jnp.logical_not(is_sample))(lambda: fn(False))


class _ColumnTiles:
    def __init__(self, tt):
        self.tt = tt
        self.n_s = SEQ_SAMPLE // tt
        self.per_group = SEQ_PROMPT // tt
        self.grid = (TIME_ROWS // tt, CHAIN_BATCH)

    def is_sample(self):
        return pl.program_id(0) < self.n_s

    def tokens(self, width):
        return pl.BlockSpec((self.tt, width), lambda j, b, *_: (j, b))

    def sample_major(self, width):
        n_s = self.n_s
        return pl.BlockSpec((None, self.tt, width), lambda j, b, *_: (
            jnp.where(j < n_s, b, CHAIN_BATCH - 1), jnp.minimum(j, n_s - 1), 0))

    def prompt_major(self, width):
        n_s, per = self.n_s, self.per_group
        return pl.BlockSpec((None, self.tt, width), lambda j, b, *_: (
            jnp.where(j < n_s, 0, (j - n_s) // per * CHAIN_BATCH + b),
            jnp.where(j < n_s, 0, (j - n_s) % per), 0))

    def mod(self, chunk):
        n_s = self.n_s
        return pl.BlockSpec((None, 1, D_MODEL), lambda j, b, *_: (jnp.where(j < n_s, 1 + b, 0), 0, chunk))

    def const(self, shape):
        return pl.BlockSpec(shape, lambda j, b, *_: (0,) * len(shape))


def _rms(x):
    return x * lax.rsqrt(jnp.mean(x * x, axis=-1, keepdims=True) + NORM_EPS)


def _silu(x):
    return x * jax.nn.sigmoid(x)


def _ada_kernel(c_ref, w_ref, b_ref, o_ref):
    s = _silu(c_ref[...]).astype(BF16)
    o_ref[...] = jnp.dot(s, w_ref[...].astype(BF16), preferred_element_type=F32) + b_ref[...]


def _ada_table(cond, ada_w, ada_b):
    tn = 1536
    return pl.pallas_call(
        _ada_kernel,
        grid=(6 * D_MODEL // tn,),
        in_specs=[pl.BlockSpec((N_COND, D_MODEL), lambda j: (0, 0)),
                  pl.BlockSpec((D_MODEL, tn), lambda j: (0, j)),
                  pl.BlockSpec((1, tn), lambda j: (0, j))],
        out_specs=pl.BlockSpec((N_COND, tn), lambda j: (0, j)),
        out_shape=jax.ShapeDtypeStruct((N_COND, 6 * D_MODEL), F32),
        compiler_params=_params(("arbitrary",)),
        name="ada_table",
    )(cond, ada_w, ada_b)


def _prenorm_kernel(xs_ref, xp_ref, g_ref, sh_ref, sc_ref, o_ref, *, tiles):
    def run(is_sample):
        y = _rms((xs_ref if is_sample else xp_ref)[...]) * g_ref[...]
        o_ref[...] = (y * (1.0 + sc_ref[...]) + sh_ref[...]).astype(o_ref.dtype)

    _on_owner(tiles.is_sample(), run)


def _prenorm(x_sample, x_prompt, gain, mod3):
    tiles = _ColumnTiles(256)
    return pl.pallas_call(
        functools.partial(_prenorm_kernel, tiles=tiles),
        grid=tiles.grid,
        in_specs=[tiles.sample_major(D_MODEL), tiles.prompt_major(D_MODEL),
                  tiles.const((1, D_MODEL)), tiles.mod(0), tiles.mod(1)],
        out_specs=tiles.tokens(D_MODEL),
        out_shape=jax.ShapeDtypeStruct((TIME_ROWS, CHAIN_BATCH * D_MODEL), BF16),
        compiler_params=_params(("arbitrary", "arbitrary")),
        name="prenorm",
    )(x_sample, x_prompt, gain, mod3, mod3)


def _mm_kernel(a_ref, w_ref, o_ref, *, act):
    acc = jnp.dot(a_ref[...], w_ref[...], preferred_element_type=F32)
    if act == "sigmoid":
        acc = jax.nn.sigmoid(acc)
    o_ref[...] = acc.astype(o_ref.dtype)


def _matmul(a, w, *, col0, n_cols, tn, out_dtype, act=None, tm=1024, name="matmul"):
    m, k = a.shape
    off = col0 // tn
    return pl.pallas_call(
        functools.partial(_mm_kernel, act=act),
        grid=(m // tm, n_cols // tn),
        in_specs=[pl.BlockSpec((tm, k), lambda i, j: (i, 0)),
                  pl.BlockSpec((k, tn), lambda i, j: (0, j + off))],
        out_specs=pl.BlockSpec((tm, tn), lambda i, j: (i, j)),
        out_shape=jax.ShapeDtypeStruct((m, n_cols), out_dtype),
        compiler_params=_params(("arbitrary", "arbitrary")),
        name=name,
    )(a, w)


def _convproj_kernel(hp_ref, h_ref, hn_ref, wb_ref, wc_ref, wx_ref, cw_ref, o_ref, *, tm):
    h = h_ref[...]
    h_ext = jnp.concatenate([hp_ref[...], h, hn_ref[...]], axis=0)
    cb = jnp.dot(h, wb_ref[...], preferred_element_type=F32)
    u_ext = (jnp.dot(h_ext, wc_ref[...], preferred_element_type=F32)
             * jnp.dot(h_ext, wx_ref[...], preferred_element_type=F32))
    u = u_ext[CONV_HALO:CONV_HALO + tm]
    u_prev = u_ext[CONV_HALO - CHAIN_BATCH:CONV_HALO - CHAIN_BATCH + tm]
    u_next = u_ext[CONV_HALO + CHAIN_BATCH:CONV_HALO + CHAIN_BATCH + tm]
    seg = jnp.where(pl.program_id(0) < N_SAMPLE // tm, GRID_W, SEQ_PROMPT)
    row = lax.broadcasted_iota(I32, u.shape, 0)
    pos = (pl.program_id(0) * (tm // CHAIN_BATCH) + row // CHAIN_BATCH) & (seg - 1)
    u_prev = jnp.where(pos == 0, 0.0, u_prev)
    u_next = jnp.where(pos == seg - 1, 0.0, u_next)
    cw = cw_ref[...]
    conv = cw[0:1, :] * u_prev + cw[1:2, :] * u + cw[2:3, :] * u_next
    o_ref[...] = (cb * conv).astype(o_ref.dtype)


def _conv_branch(h, w_in_bf, conv_w):
    tm, tn = 1024, 256
    nb = W_BRANCH // tn
    per = tm // CONV_HALO
    return pl.pallas_call(
        functools.partial(_convproj_kernel, tm=tm),
        grid=(N_TOK // tm, nb),
        in_specs=[pl.BlockSpec((CONV_HALO, D_MODEL), lambda i, j: (jnp.maximum(i * per - 1, 0), 0)),
                  pl.BlockSpec((tm, D_MODEL), lambda i, j: (i, 0)),
                  pl.BlockSpec((CONV_HALO, D_MODEL),
                               lambda i, j: (jnp.minimum((i + 1) * per, N_TOK // CONV_HALO - 1), 0)),
                  pl.BlockSpec((D_MODEL, tn), lambda i, j: (0, j)),
                  pl.BlockSpec((D_MODEL, tn), lambda i, j: (0, j + nb)),
                  pl.BlockSpec((D_MODEL, tn), lambda i, j: (0, j + 2 * nb)),
                  pl.BlockSpec((3, tn), lambda i, j: (0, j))],
        out_specs=pl.BlockSpec((tm, tn), lambda i, j: (i, j)),
        out_shape=jax.ShapeDtypeStruct((N_TOK, W_BRANCH), BF16),
        compiler_params=_params(("arbitrary", "arbitrary")),
        name="conv_branch",
    )(h, h, h, w_in_bf, w_in_bf, w_in_bf, conv_w)


def _lora_kernel(x_ref, wd_ref, wa_ref, wg_ref, w0_ref, a0_ref, wa_out_ref, g_ref):
    def group(i):
        return x_ref[:, i * LORA_PAD:(i + 1) * LORA_PAD]

    def decay(wl):
        return jnp.exp(-jax.nn.sigmoid(wl) * math.exp(-0.5))

    for d in range(2):
        dl = jnp.dot(jnp.tanh(group(d)).astype(BF16), wd_ref[d], preferred_element_type=F32)
        wa_out_ref[:, d * W_BRANCH:(d + 1) * W_BRANCH] = decay(w0_ref[d:d + 1, :] + dl)
        al = jnp.dot(group(2 + d).astype(BF16), wa_ref[d], preferred_element_type=F32)
        wa_out_ref[:, (2 + d) * W_BRANCH:(3 + d) * W_BRANCH] = jax.nn.sigmoid(a0_ref[d:d + 1, :] + al)
    gl = jax.nn.sigmoid(x_ref[:, 4 * LORA_PAD:]).astype(BF16)
    g_ref[...] = jnp.dot(gl, wg_ref[...], preferred_element_type=F32)


def _lora_stage(lora, wd2, wa2, wg2, w0, a0):
    tm = 512
    return pl.pallas_call(
        _lora_kernel,
        grid=(N_TOK // tm,),
        in_specs=[pl.BlockSpec((tm, LORA_W), lambda i: (i, 0)),
                  pl.BlockSpec((2, LORA_PAD, W_BRANCH), lambda i: (0, 0, 0)),
                  pl.BlockSpec((2, LORA_PAD, W_BRANCH), lambda i: (0, 0, 0)),
                  pl.BlockSpec((256, W_BRANCH), lambda i: (0, 0)),
                  pl.BlockSpec((2, W_BRANCH), lambda i: (0, 0)),
                  pl.BlockSpec((2, W_BRANCH), lambda i: (0, 0))],
        out_specs=[pl.BlockSpec((tm, 4 * W_BRANCH), lambda i: (i, 0)),
                   pl.BlockSpec((tm, W_BRANCH), lambda i: (i, 0))],
        out_shape=[jax.ShapeDtypeStruct((N_TOK, 4 * W_BRANCH), F32),
                   jax.ShapeDtypeStruct((N_TOK, W_BRANCH), F32)],
        compiler_params=_params(("arbitrary",)),
        name="lora_stage",
    )(lora, wd2, wa2, wg2, w0, a0)


def _chain_tiles(x_ref, t0):
    both = (x_ref[t0], x_ref[t0 + 1])
    rows = [both[tl][:, p * LANES:(p + 1) * LANES] for tl in range(2) for p in range(CHAIN_BATCH)]
    sq = jnp.concatenate(rows, axis=0).T
    top, bot = sq[:HEAD], sq[HEAD:]
    low = lax.broadcasted_iota(I32, (HEAD, LANES), 1) < HEAD
    return (jnp.where(low, top, pltpu.roll(bot, HEAD, 1)),
            jnp.where(low, pltpu.roll(top, HEAD, 1), bot))


def _scan_kernel(r_ref, k_ref, v_ref, w_ref, a_ref, s0_ref, kk_ref, ka_ref, rk_ref,
                 y_ref, bon_ref, sf_ref, s_ref, ops_a, ops_b, *, tc):
    d = pl.program_id(0)
    c = pl.program_id(2)
    n_pairs = tc // 2

    @pl.when(c == 0)
    def _():
        s_ref[...] = s0_ref[...]

    def first_row(pair_idx):
        return 2 * jnp.where(d == 0, pair_idx, n_pairs - 1 - pair_idx)

    def prepare(pair_idx, ops_ref):
        t0 = first_row(pair_idx)
        r2, k2, v2 = _chain_tiles(r_ref, t0), _chain_tiles(k_ref, t0), _chain_tiles(v_ref, t0)
        w2, a2 = _chain_tiles(w_ref, t0), _chain_tiles(a_ref, t0)
        for tl in range(2):
            kk = k2[tl] * kk_ref[...]
            kk = kk * lax.rsqrt(jnp.sum(kk * kk, axis=0, keepdims=True) + 1e-12)
            kd = k2[tl] * (1.0 + (a2[tl] - 1.0) * ka_ref[...])
            for q, val in enumerate((kk, kk * a2[tl], kd, w2[tl], r2[tl], v2[tl])):
                ops_ref[tl, q] = val
            bon_ref[t0 + tl] = jnp.sum(r2[tl] * kd * rk_ref[...], axis=0, keepdims=True) * v2[tl]

    def recur(pair_idx, ops_ref):
        t0 = first_row(pair_idx)
        for s in range(2):
            tl = jnp.where(d == 0, s, 1 - s)
            vt = ops_ref[tl, 5]
            sa = jnp.zeros((HEAD, LANES), F32)
            for j in range(HEAD):
                sa = sa + s_ref[j] * ops_ref[tl, 0, pl.ds(j, 1), :]
            y = jnp.zeros((HEAD, LANES), F32)
            for j in range(HEAD):
                sn = (s_ref[j] * ops_ref[tl, 3, pl.ds(j, 1), :] - sa * ops_ref[tl, 1, pl.ds(j, 1), :]
                      + vt * ops_ref[tl, 2, pl.ds(j, 1), :])
                s_ref[j] = sn
                y = y + sn * ops_ref[tl, 4, pl.ds(j, 1), :]
            y_ref[t0 + tl] = y

    prepare(0, ops_a)

    def two_pairs(i, carry):
        prepare(2 * i + 1, ops_b)
        recur(2 * i, ops_a)
        prepare(jnp.minimum(2 * i + 2, n_pairs - 1), ops_a)
        recur(2 * i + 1, ops_b)
        return carry

    lax.fori_loop(0, n_pairs // 2, two_pairs, 0)

    @pl.when(c == pl.num_programs(2) - 1)
    def _():
        sf_ref[...] = s_ref[...]


def _scan(rkv3, wa3, row0, seq, s0, kk_l, ka_l, rk_l, *, tc=32):
    g_n = s0.shape[1]
    nc = seq // tc

    def tchunk(d, c):
        return jnp.where(d == 0, c, nc - 1 - c)

    def tok(col):
        return pl.BlockSpec((tc, CHAIN_BATCH, W_BRANCH),
                            lambda d, g, c: (row0 // tc + g * nc + tchunk(d, c), 0, col))

    def tok_dir(col):
        return pl.BlockSpec((tc, CHAIN_BATCH, W_BRANCH),
                            lambda d, g, c: (row0 // tc + g * nc + tchunk(d, c), 0, col + d))

    chain = pl.BlockSpec((None, None, tc, HEAD, LANES), lambda d, g, c: (d, g, tchunk(d, c), 0, 0))
    state = pl.BlockSpec((None, None, HEAD, HEAD, LANES), lambda d, g, c: (d, g, 0, 0, 0))
    par = pl.BlockSpec((HEAD, LANES), lambda d, g, c: (0, 0))
    seq_shape = jax.ShapeDtypeStruct((2, g_n, seq, HEAD, LANES), F32)
    ops = pltpu.VMEM((2, 6, HEAD, LANES), F32)
    return pl.pallas_call(
        functools.partial(_scan_kernel, tc=tc),
        grid=(2, g_n, nc),
        in_specs=[tok(0), tok(1), tok(2), tok_dir(0), tok_dir(2), state, par, par, par],
        out_specs=[chain, chain, state],
        out_shape=[seq_shape, seq_shape, jax.ShapeDtypeStruct((2, g_n, HEAD, HEAD, LANES), F32)],
        scratch_shapes=[pltpu.VMEM((HEAD, HEAD, LANES), F32), ops, ops],
        compiler_params=_params(("arbitrary", "arbitrary", "arbitrary")),
        name="wkv7_scan",
    )(rkv3, rkv3, rkv3, wa3, wa3, s0, kk_l, ka_l, rk_l)


def _scan_post_kernel(y_ref, bon_ref, lw_ref, lb_ref, o_ref, *, tc):
    low = lax.broadcasted_iota(I32, (HEAD, LANES), 1) < HEAD

    def pair(i, carry):
        t0 = 2 * i
        tiles = []
        for tl in range(2):
            ys = y_ref[0, t0 + tl] + y_ref[1, t0 + tl]
            dev = ys - jnp.mean(ys, axis=0, keepdims=True)
            var = jnp.mean(dev * dev, axis=0, keepdims=True)
            yn = dev * lax.rsqrt(var + GN_EPS) * lw_ref[...] + lb_ref[...]
            tiles.append(yn + bon_ref[0, t0 + tl] + bon_ref[1, t0 + tl])
        top = jnp.where(low, tiles[0], pltpu.roll(tiles[1], HEAD, 1))
        bot = jnp.where(low, pltpu.roll(tiles[0], HEAD, 1), tiles[1])
        sq = jnp.concatenate([top, bot], axis=0).T
        for tl in range(2):
            for p in range(CHAIN_BATCH):
                r0 = tl * HEAD + p * CHAIN_BATCH
                o_ref[t0 + tl, :, pl.ds(p * LANES, LANES)] = sq[r0:r0 + CHAIN_BATCH, :]
        return carry

    lax.fori_loop(0, tc // 2, pair, 0)


def _scan_post(y, bon, lw_l, lb_l):
    _, g_n, seq = y.shape[:3]
    tc = 64
    nc = seq // tc
    both = pl.BlockSpec((2, None, tc, HEAD, LANES), lambda g, c: (0, g, c, 0, 0))
    par = pl.BlockSpec((HEAD, LANES), lambda g, c: (0, 0))
    return pl.pallas_call(
        functools.partial(_scan_post_kernel, tc=tc),
        grid=(g_n, nc),
        in_specs=[both, both, par, par],
        out_specs=pl.BlockSpec((tc, CHAIN_BATCH, W_BRANCH), lambda g, c: (g * nc + c, 0, 0)),
        out_shape=jax.ShapeDtypeStruct((g_n * seq, CHAIN_BATCH, W_BRANCH), F32),
        compiler_params=_params(("arbitrary", "arbitrary")),
        name="scan_post",
    )(y, bon, lw_l, lb_l)


def _state_to_chain_layout(s):
    g_n = s.shape[0] // CHAIN_BATCH
    s = s.reshape(g_n, CHAIN_BATCH, N_HEADS // 2, 2, HEAD, HEAD)
    return jnp.transpose(s, (0, 5, 4, 3, 2, 1)).reshape(g_n, HEAD, HEAD, LANES)


def _state_from_chain_layout(s):
    g_n = s.shape[0]
    s = s.reshape(g_n, HEAD, HEAD, 2, N_HEADS // 2, CHAIN_BATCH)
    return jnp.transpose(s, (0, 5, 4, 3, 2, 1)).reshape(g_n * CHAIN_BATCH, N_HEADS, HEAD, HEAD)


def _head_param_to_chain_layout(p):
    p = jnp.transpose(p.reshape(N_HEADS // 2, 2, HEAD), (1, 0, 2))
    p = jnp.broadcast_to(p[:, :, None, :], (2, N_HEADS // 2, CHAIN_BATCH, HEAD))
    return p.reshape(LANES, HEAD).T


def _merge_kernel(z_ref, ybs_ref, ybp_ref, g_ref, ga_ref, gb_ref, wc_ref, wr_ref, o_ref, *, tm):
    def run(is_sample):
        y_a = jnp.dot(z_ref[...], wc_ref[...], preferred_element_type=F32)
        yb = ((ybs_ref if is_sample else ybp_ref)[...] * g_ref[...]).astype(BF16)
        y_b = jnp.dot(yb, wr_ref[...], preferred_element_type=F32)
        o_ref[...] = (ga_ref[...].astype(F32) * y_a + gb_ref[...].astype(F32) * y_b).astype(o_ref.dtype)

    _on_owner(pl.program_id(0) < N_SAMPLE // tm, run)


def _merge(z, yb_s, yb_p, g, gates, w_conv_bf, w_rwkv_bf):
    tm = 512
    row = lambda w: pl.BlockSpec((tm, w), lambda i: (i, 0))
    return pl.pallas_call(
        functools.partial(_merge_kernel, tm=tm),
        grid=(N_TOK // tm,),
        in_specs=[row(W_BRANCH), *_split_specs(tm, W_BRANCH), row(W_BRANCH),
                  pl.BlockSpec((tm, D_MODEL), lambda i: (i, 0)),
                  pl.BlockSpec((tm, D_MODEL), lambda i: (i, 1)),
                  pl.BlockSpec((W_BRANCH, D_MODEL), lambda i: (0, 0)),
                  pl.BlockSpec((W_BRANCH, D_MODEL), lambda i: (0, 0))],
        out_specs=row(D_MODEL),
        out_shape=jax.ShapeDtypeStruct((N_TOK, D_MODEL), BF16),
        compiler_params=_params(("arbitrary",)),
        name="merge",
    )(z, yb_s, yb_p, g, gates, gates, w_conv_bf, w_rwkv_bf)


def _outproj_kernel(m_ref, wo_ref, xs_ref, xp_ref, g1_ref, sh2_ref, sc2_ref, npost_ref, npre_ref,
                    x1_ref, h2_ref, *, tiles):
    def run(is_sample):
        out = jnp.dot(m_ref[...], wo_ref[...], preferred_element_type=F32)
        x1 = (xs_ref if is_sample else xp_ref)[...] + g1_ref[...] * (_rms(out) * npost_ref[...])
        x1_ref[...] = x1
        h2_ref[...] = (_rms(x1) * npre_ref[...]) * (1.0 + sc2_ref[...]) + sh2_ref[...]

    _on_owner(tiles.is_sample(), run)


def _outproj(merged, w_o_bf, x_sample, x_prompt, mod3, n_post, n_pre):
    tiles = _ColumnTiles(256)
    row = tiles.tokens(D_MODEL)
    vec = tiles.const((1, D_MODEL))
    out = jax.ShapeDtypeStruct((TIME_ROWS, CHAIN_BATCH * D_MODEL), F32)
    return pl.pallas_call(
        functools.partial(_outproj_kernel, tiles=tiles),
        grid=tiles.grid,
        in_specs=[row, tiles.const((D_MODEL, D_MODEL)),
                  tiles.sample_major(D_MODEL), tiles.prompt_major(D_MODEL),
                  tiles.mod(2), tiles.mod(3), tiles.mod(4), vec, vec],
        out_specs=[row, row],
        out_shape=[out, out],
        compiler_params=_params(("arbitrary", "arbitrary")),
        name="outproj",
    )(merged, w_o_bf, x_sample, x_prompt, mod3, mod3, mod3, n_post, n_pre)


def _first_index_of_max(x, axis, n):
    m = jnp.max(x, axis=axis, keepdims=True)
    idx = lax.broadcasted_iota(I32, x.shape, axis).astype(F32)
    first = jnp.min(jnp.where(x == m, idx, float(n)), axis=axis, keepdims=True)
    return m, idx, first


def _router_kernel(h_ref, rw_ref, rb_ref, eidx_ref, wsel_ref, rank_ref, cnt_ref, base_ref, *, tm):
    @pl.when(pl.program_id(0) == 0)
    def _():
        base_ref[...] = jnp.zeros_like(base_ref)

    logits = lax.dot_general(rw_ref[...], h_ref[...], (((1,), (1,)), ((), ())),
                             precision=lax.Precision.HIGHEST, preferred_element_type=F32)
    scores = jax.nn.sigmoid(logits)
    biased = scores + rb_ref[...]
    neg = -jnp.inf

    per_group = N_EXPERTS // N_GROUPS
    grp = biased.reshape(N_GROUPS, per_group, tm)
    m1, idx, first = _first_index_of_max(grp, 1, per_group)
    m2 = jnp.max(jnp.where(idx == first, neg, grp), axis=1, keepdims=True)
    gscore = (m1 + m2).reshape(N_GROUPS, tm)

    gsel = jnp.zeros((N_GROUPS, tm), F32)
    for _ in range(TOPK_GROUPS):
        _, gidx, gfirst = _first_index_of_max(gscore, 0, N_GROUPS)
        hit = gidx == gfirst
        gsel = jnp.where(hit, 1.0, gsel)
        gscore = jnp.where(hit, neg, gscore)
    emask = jnp.broadcast_to(gsel[:, None, :], (N_GROUPS, per_group, tm)).reshape(N_EXPERTS, tm)

    cand = jnp.where(emask > 0.5, biased, neg)
    mem = jnp.zeros((N_EXPERTS, tm), F32)
    picks = []
    for _ in range(TOP_K):
        _, eidx, efirst = _first_index_of_max(cand, 0, N_EXPERTS)
        hit = eidx == efirst
        mem = jnp.where(hit, 1.0, mem)
        cand = jnp.where(hit, neg, cand)
        picks.append((efirst, hit))

    s_i = lax.broadcasted_iota(I32, (tm, tm), 0)
    t_i = lax.broadcasted_iota(I32, (tm, tm), 1)
    upper = (s_i <= t_i).astype(BF16)
    incl = jnp.dot(mem.astype(BF16), upper, preferred_element_type=F32)
    rank = base_ref[...] + incl - mem
    base_ref[...] = base_ref[...] + jnp.sum(mem, axis=1, keepdims=True)
    cnt_ref[...] = base_ref[...]

    wsum = jnp.zeros((1, tm), F32)
    wrows = []
    for j, (efirst, hit) in enumerate(picks):
        wj = jnp.sum(jnp.where(hit, scores, 0.0), axis=0, keepdims=True)
        wrows.append(wj)
        wsum = wsum + wj
        eidx_ref[pl.ds(j, 1), :] = efirst.astype(I32)
        rank_ref[pl.ds(j, 1), :] = jnp.sum(jnp.where(hit, rank, 0.0), axis=0, keepdims=True).astype(I32)
    for j, wj in enumerate(wrows):
        wsel_ref[pl.ds(j, 1), :] = wj / wsum * ROUTED_SCALE


def _router(h2, rw_t, rb_col):
    tm = 256
    tok = lambda dt: jax.ShapeDtypeStruct((TOP_K, N_TOK), dt)
    tspec = pl.BlockSpec((TOP_K, tm), lambda i: (0, i))
    return pl.pallas_call(
        functools.partial(_router_kernel, tm=tm),
        grid=(N_TOK // tm,),
        in_specs=[pl.BlockSpec((tm, D_MODEL), lambda i: (i, 0)),
                  pl.BlockSpec((N_EXPERTS, D_MODEL), lambda i: (0, 0)),
                  pl.BlockSpec((N_EXPERTS, 1), lambda i: (0, 0))],
        out_specs=[tspec, tspec, tspec, pl.BlockSpec((N_EXPERTS, 1), lambda i: (0, 0))],
        out_shape=[tok(I32), tok(F32), tok(I32), jax.ShapeDtypeStruct((N_EXPERTS, 1), F32)],
        scratch_shapes=[pltpu.VMEM((N_EXPERTS, 1), F32)],
        compiler_params=_params(("arbitrary",)),
        name="router",
    )(h2, rw_t, rb_col)


def _slot_kernel(start_ref, eidx_ref, rank_ref, pos_ref):
    e = eidx_ref[...]
    pos = rank_ref[...]
    for j in range(N_EXPERTS):
        pos = pos + jnp.where(e == j, start_ref[j], 0)
    pos_ref[...] = pos


def _slots(pad_start, eidx, rank):
    full = pl.BlockSpec((TOP_K, N_TOK), lambda i, s: (0, 0))
    return pl.pallas_call(
        _slot_kernel,
        grid_spec=pltpu.PrefetchScalarGridSpec(num_scalar_prefetch=1, grid=(1,),
                                               in_specs=[full, full], out_specs=full),
        out_shape=jax.ShapeDtypeStruct((TOP_K, N_TOK), I32),
        compiler_params=_params(("arbitrary",)),
        name="slots",
    )(pad_start, eidx, rank)


def _gather_rows(idx_ref, base, n_rows, src_hbm, dst, sem, straight_line=False):
    def issue(i, priority):
        pltpu.make_async_copy(src_hbm.at[pl.ds(idx_ref[base + i], 1), :], dst.at[pl.ds(i, 1), :],
                              sem).start(priority=priority)

    def group(g, carry):
        for j in range(DMA_UNROLL):
            issue(g * DMA_UNROLL + j, j % 2)
        return carry

    if straight_line:
        for i in range(n_rows):
            issue(i, i % 2)
    else:
        lax.fori_loop(0, n_rows // DMA_UNROLL, group, 0)


def _wait_rows(n_rows, src_hbm, dst, sem):
    pltpu.make_async_copy(src_hbm.at[pl.ds(0, n_rows), :], dst, sem).wait()


def _expert_kernel(be_ref, first_ref, nused_ref, tok_ref,
                   h_hbm, wg_ref, wu_ref, wd_ref, o_ref, xbuf, wgb, wub, wdb, sem):
    b = pl.program_id(0)
    n_used = nused_ref[0]
    slot = b % 2

    @pl.when(b == 0)
    def _():
        _gather_rows(tok_ref, 0, MOE_ROWS, h_hbm, xbuf.at[0], sem.at[0])

    @pl.when(b < n_used)
    def _():
        @pl.when(first_ref[b] == 1)
        def _():
            wgb[...] = wg_ref[...].astype(BF16)
            wub[...] = wu_ref[...].astype(BF16)
            wdb[...] = wd_ref[...].astype(BF16)

        _wait_rows(MOE_ROWS, h_hbm, xbuf.at[slot], sem.at[slot])
        _gather_rows(tok_ref, (b + 1) * MOE_ROWS, MOE_ROWS, h_hbm, xbuf.at[1 - slot], sem.at[1 - slot],
                     straight_line=True)
        x = xbuf[slot].astype(BF16)
        gate = jnp.dot(x, wgb[...], preferred_element_type=F32)
        up = jnp.dot(x, wub[...], preferred_element_type=F32)
        hid = (_silu(gate) * up).astype(BF16)
        o_ref[...] = jnp.dot(hid, wdb[...], preferred_element_type=F32)

    @pl.when(b >= n_used)
    def _():
        @pl.when(b == n_used)
        def _():
            _wait_rows(MOE_ROWS, h_hbm, xbuf.at[slot], sem.at[slot])

        o_ref[...] = jnp.zeros_like(o_ref)


def _experts(blk_expert, blk_first, n_used, tok_of_slot, h2, exp_gate, exp_up, exp_down):
    wspec_in = pl.BlockSpec((None, D_MODEL, D_EXPERT), lambda b, be, fi, nu, tk: (be[b], 0, 0))
    wspec_out = pl.BlockSpec((None, D_EXPERT, D_MODEL), lambda b, be, fi, nu, tk: (be[b], 0, 0))
    grid_spec = pltpu.PrefetchScalarGridSpec(
        num_scalar_prefetch=4,
        grid=(MOE_BLOCKS,),
        in_specs=[pl.BlockSpec(memory_space=pl.ANY), wspec_in, wspec_in, wspec_out],
        out_specs=pl.BlockSpec((MOE_ROWS, D_MODEL), lambda b, be, fi, nu, tk: (b, 0)),
        scratch_shapes=[pltpu.VMEM((2, MOE_ROWS, D_MODEL), F32),
                        pltpu.VMEM((D_MODEL, D_EXPERT), BF16),
                        pltpu.VMEM((D_MODEL, D_EXPERT), BF16),
                        pltpu.VMEM((D_EXPERT, D_MODEL), BF16),
                        pltpu.SemaphoreType.DMA((2,))],
    )
    return pl.pallas_call(
        _expert_kernel,
        grid_spec=grid_spec,
        out_shape=jax.ShapeDtypeStruct((MOE_BLOCKS * MOE_ROWS, D_MODEL), F32),
        compiler_params=_params(("arbitrary",)),
        name="experts",
    )(blk_expert, blk_first, n_used, tok_of_slot, h2, exp_gate, exp_up, exp_down)


def _combine_kernel(pos_ref, y_hbm, wt_ref, h_ref, x1_ref, g2_ref, npost_ref,
                    sg_ref, su_ref, sd_ref, os_ref, op_ref, buf, sem, *, tiles):
    i = pl.program_id(0) * pl.num_programs(1) + pl.program_id(1)
    n_steps = pl.num_programs(0) * pl.num_programs(1)
    slot = i % 2
    tm = tiles.tt
    rows = TOP_K * tm

    @pl.when(i == 0)
    def _():
        _gather_rows(pos_ref, 0, rows, y_hbm, buf.at[0], sem.at[0])

    @pl.when(i + 1 < n_steps)
    def _():
        _gather_rows(pos_ref, (i + 1) * rows, rows, y_hbm, buf.at[1 - slot], sem.at[1 - slot])

    hb = h_ref[...].astype(BF16)
    gate = jnp.dot(hb, sg_ref[...], preferred_element_type=F32)
    up = jnp.dot(hb, su_ref[...], preferred_element_type=F32)
    hid = (_silu(gate) * up).astype(BF16)
    f = jnp.dot(hid, sd_ref[...], preferred_element_type=F32)

    _wait_rows(rows, y_hbm, buf.at[slot], sem.at[slot])
    wt = wt_ref[...]
    for j in range(TOP_K):
        f = f + buf[slot, pl.ds(j * tm, tm), :] * wt[:, j:j + 1]
    out = x1_ref[...] + g2_ref[...] * (_rms(f) * npost_ref[...])

    def store(is_sample):
        (os_ref if is_sample else op_ref)[...] = out

    _on_owner(tiles.is_sample(), store)


def _combine(pos_tiles, y_sorted, wsel_tiles, h2, x1, mod3, n_post, sg_bf, su_bf, sd_bf):
    tiles = _ColumnTiles(COMBINE_ROWS)
    tm = tiles.tt
    row = tiles.tokens(D_MODEL)
    grid_spec = pltpu.PrefetchScalarGridSpec(
        num_scalar_prefetch=1,
        grid=tiles.grid,
        in_specs=[pl.BlockSpec(memory_space=pl.ANY),
                  pl.BlockSpec((tm, TOP_K), lambda j, b, p: (j * CHAIN_BATCH + b, 0)),
                  row, row, tiles.mod(5),
                  tiles.const((1, D_MODEL)), tiles.const((D_MODEL, D_EXPERT)),
                  tiles.const((D_MODEL, D_EXPERT)), tiles.const((D_EXPERT, D_MODEL))],
        out_specs=[tiles.sample_major(D_MODEL), tiles.prompt_major(D_MODEL)],
        scratch_shapes=[pltpu.VMEM((2, TOP_K * tm, D_MODEL), F32), pltpu.SemaphoreType.DMA((2,))],
    )
    return pl.pallas_call(
        functools.partial(_combine_kernel, tiles=tiles),
        grid_spec=grid_spec,
        out_shape=[jax.ShapeDtypeStruct((N_SAMPLE // SEQ_SAMPLE, SEQ_SAMPLE, D_MODEL), F32),
                   jax.ShapeDtypeStruct((N_PROMPT // SEQ_PROMPT, SEQ_PROMPT, D_MODEL), F32)],
        compiler_params=_params(("arbitrary", "arbitrary")),
        name="combine",
    )(pos_tiles, y_sorted, wsel_tiles, h2, x1, mod3, n_post, sg_bf, su_bf, sd_bf)


def _pad_rows(w, rows):
    return jnp.pad(w, ((0, rows - w.shape[0]), (0, 0)))


def kernel(x_prompt, x_sample, state_fwd, state_bwd, c, c_ctx, ada_w, ada_b, norm_pre_mix, norm_post_mix, norm_pre_ffn, norm_post_ffn, w_in, conv_w, w_out_conv, decay_w0, decay_w2, iclr_a0, iclr_a2, gate_g2, k_k, k_a, r_k, lnx_w, lnx_b, w_out_rwkv, w_o, router_w, router_bias, exp_gate, exp_up, exp_down, sh_gate, sh_up, sh_down):
    cond = jnp.concatenate([c_ctx[None, :], c, jnp.zeros((N_COND - 1 - c.shape[0], D_MODEL), F32)], axis=0)
    mod3 = _ada_table(cond, ada_w[0], ada_b).reshape(N_COND, 1, 6 * D_MODEL)

    h = _prenorm(x_sample, x_prompt, norm_pre_mix, mod3).reshape(N_TOK, D_MODEL)
    w_in_bf = w_in[0].astype(BF16)
    z = _conv_branch(h, w_in_bf, conv_w[0])
    rkv = _matmul(h, w_in_bf, col0=3 * W_BRANCH, n_cols=3 * W_BRANCH, tn=512, out_dtype=F32, name="proj_rkv")
    c0 = 6 * W_BRANCH
    pad_cols = lambda lo: jnp.pad(w_in_bf[:, lo:lo + 96], ((0, 0), (0, LORA_PAD - 96)))
    w_lora = jnp.concatenate([pad_cols(c0), pad_cols(c0 + 96), pad_cols(c0 + 192), pad_cols(c0 + 288),
                              w_in_bf[:, c0 + 384:c0 + 640]], axis=1)
    lora = _matmul(h, w_lora, col0=0, n_cols=LORA_W, tn=LORA_W, out_dtype=F32, name="proj_lora")
    gates = _matmul(h, w_in_bf[:, c0 + 640:], col0=0, n_cols=2 * D_MODEL, tn=512,
                    out_dtype=BF16, act="sigmoid", name="proj_gates")

    wd2 = jnp.stack([_pad_rows(decay_w2[0, 0], LORA_PAD), _pad_rows(decay_w2[0, 1], LORA_PAD)]).astype(BF16)
    wa2 = jnp.stack([_pad_rows(iclr_a2[0, 0], LORA_PAD), _pad_rows(iclr_a2[0, 1], LORA_PAD)]).astype(BF16)
    wa, g = _lora_stage(lora, wd2, wa2, gate_g2[0].astype(BF16), decay_w0[0], iclr_a0[0])

    kk_l = _head_param_to_chain_layout(k_k[0])
    ka_l = _head_param_to_chain_layout(k_a[0])
    rk_l = _head_param_to_chain_layout(r_k[0].reshape(-1))
    lw_l = _head_param_to_chain_layout(lnx_w[0])
    lb_l = _head_param_to_chain_layout(lnx_b[0])

    rkv3 = rkv.reshape(TIME_ROWS, CHAIN_BATCH, 3 * W_BRANCH)
    wa3 = wa.reshape(TIME_ROWS, CHAIN_BATCH, 4 * W_BRANCH)

    def run_scan(row0, seq, s0_f, s0_b):
        s0 = jnp.stack([_state_to_chain_layout(s0_f), _state_to_chain_layout(s0_b)])
        y, bon, s_fin = _scan(rkv3, wa3, row0, seq, s0, kk_l, ka_l, rk_l)
        return _scan_post(y, bon, lw_l, lb_l).reshape(-1, W_BRANCH), s_fin

    zero_state = jnp.zeros((N_PROMPT // SEQ_PROMPT, N_HEADS, HEAD, HEAD), F32)
    yb_s, _ = run_scan(0, SEQ_SAMPLE, state_fwd[:, 0], state_bwd[:, 0])
    yb_p, s_fin = run_scan(SEQ_SAMPLE, SEQ_PROMPT, zero_state, zero_state)

    merged = _merge(z, yb_s, yb_p, g, gates, w_out_conv[0].astype(BF16), w_out_rwkv[0].astype(BF16))
    x1, h2 = _outproj(merged.reshape(TIME_ROWS, CHAIN_BATCH * D_MODEL), w_o[0].astype(BF16),
                      x_sample, x_prompt, mod3, norm_post_mix, norm_pre_ffn)

    h2_flat = h2.reshape(N_TOK, D_MODEL)
    eidx, wsel, rank, counts = _router(h2_flat, router_w[0].T, router_bias[0][:, None])
    counts = counts[:, 0].astype(I32)
    padded = (counts + MOE_ROWS - 1) // MOE_ROWS * MOE_ROWS
    pad_end = jnp.cumsum(padded)
    pad_start = pad_end - padded
    pos = _slots(pad_start, eidx, rank)
    tok_ids = jnp.broadcast_to(jnp.arange(N_TOK, dtype=I32)[None, :], (TOP_K, N_TOK))
    tok_of_slot = jnp.zeros((MOE_BLOCKS * MOE_ROWS,), I32).at[pos.reshape(-1)].set(
        tok_ids.reshape(-1), unique_indices=True, mode="promise_in_bounds")
    blk_start = jnp.arange(MOE_BLOCKS, dtype=I32) * MOE_ROWS
    blk_expert = jnp.minimum(jnp.sum(pad_end[None, :] <= blk_start[:, None], axis=1), N_EXPERTS - 1).astype(I32)
    n_used = (pad_end[-1] // MOE_ROWS).astype(I32).reshape(1)
    blk_first = (blk_start == pad_start[blk_expert]).astype(I32)
    y_sorted = _experts(blk_expert, blk_first, n_used, tok_of_slot, h2_flat, exp_gate[0], exp_up[0], exp_down[0])

    def tile_order(a, perm):
        a = a.reshape(TOP_K, TIME_ROWS // COMBINE_ROWS, COMBINE_ROWS, CHAIN_BATCH)
        return jnp.transpose(a, perm)

    pos_tiles = tile_order(pos, (1, 3, 0, 2)).reshape(-1)
    wsel_tiles = tile_order(wsel, (1, 3, 2, 0)).reshape(N_TOK, TOP_K)
    out_s, out_p = _combine(pos_tiles, y_sorted, wsel_tiles, h2, x1, mod3, norm_post_ffn,
                            sh_gate[0].astype(BF16), sh_up[0].astype(BF16), sh_down[0].astype(BF16))

    new_f = _state_from_chain_layout(s_fin[0])[:, None]
    new_b = _state_from_chain_layout(s_fin[1])[:, None]
    return (out_p, out_s, new_f, new_b)
```

```python
import functools
import math

import jax
import jax.numpy as jnp
from jax import lax
from jax.experimental import pallas as pl
from jax.experimental.pallas import tpu as pltpu

F32 = jnp.float32
BF16 = jnp.bfloat16
I32 = jnp.int32

D_MODEL = 2048
N_SAMPLE = 8 * 1024
SEQ_SAMPLE = 1024
N_PROMPT = 16 * 256
SEQ_PROMPT = 256
N_TOK = N_SAMPLE + N_PROMPT
GRID_W = 64
W_BRANCH = 1024
HEAD = 64
N_HEADS = 16
N_COND = 16
LORA_PAD = 128
LORA_W = 4 * LORA_PAD + 256
N_EXPERTS = 64
TOP_K = 8
N_GROUPS = 8
TOPK_GROUPS = 4
D_EXPERT = 512
ROUTED_SCALE = 2.5
NORM_EPS = 1e-6
GN_EPS = 64e-5
LANES = 128
CHAIN_BATCH = 8
TIME_ROWS = N_TOK // CHAIN_BATCH
CONV_HALO = 16
MOE_ROWS = 512
MOE_BLOCKS = N_TOK * TOP_K // MOE_ROWS + N_EXPERTS
EXPERT_PIECES = 8
COMBINE_ROWS = 128
DMA_UNROLL = 8
VMEM_LIMIT = 56 * 1024 * 1024


def _params(sem, vmem=VMEM_LIMIT):
    return pltpu.CompilerParams(dimension_semantics=sem, vmem_limit_bytes=vmem)


def _split_specs(tm, width):
    n_s = N_SAMPLE // tm
    return (pl.BlockSpec((tm, width), lambda i, *_: (jnp.minimum(i, n_s - 1), 0)),
            pl.BlockSpec((tm, width), lambda i, *_: (jnp.maximum(i - n_s, 0), 0)))


def _on_owner(is_sample, fn):
    pl.when(is_sample)(lambda: fn(True))
    pl.when(jnp.logical_not(is_sample))(lambda: fn(False))


class _ColumnTiles:
    def __init__(self, tt):
        self.tt = tt
        self.n_s = SEQ_SAMPLE // tt
        self.per_group = SEQ_PROMPT // tt
        self.grid = (TIME_ROWS // tt, CHAIN_BATCH)

    def is_sample(self):
        return pl.program_id(0) < self.n_s

    def tokens(self, width):
        return pl.BlockSpec((self.tt, width), lambda j, b, *_: (j, b))

    def sample_major(self, width):
        n_s = self.n_s
        return pl.BlockSpec((None, self.tt, width), lambda j, b, *_: (
            jnp.where(j < n_s, b, CHAIN_BATCH - 1), jnp.minimum(j, n_s - 1), 0))

    def prompt_major(self, width):
        n_s, per = self.n_s, self.per_group
        return pl.BlockSpec((None, self.tt, width), lambda j, b, *_: (
            jnp.where(j < n_s, 0, (j - n_s) // per * CHAIN_BATCH + b),
            jnp.where(j < n_s, 0, (j - n_s) % per), 0))

    def mod(self, chunk):
        n_s = self.n_s
        return pl.BlockSpec((None, 1, D_MODEL), lambda j, b, *_: (jnp.where(j < n_s, 1 + b, 0), 0, chunk))

    def const(self, shape):
        return pl.BlockSpec(shape, lambda j, b, *_: (0,) * len(shape))


def _rms(x):
    return x * lax.rsqrt(jnp.mean(x * x, axis=-1, keepdims=True) + NORM_EPS)


def _silu(x):
    return x * jax.nn.sigmoid(x)


def _ada_kernel(c_ref, w_ref, b_ref, o_ref):
    s = _silu(c_ref[...]).astype(BF16)
    o_ref[...] = jnp.dot(s, w_ref[...].astype(BF16), preferred_element_type=F32) + b_ref[...]


def _ada_table(cond, ada_w, ada_b):
    tn = 1536
    return pl.pallas_call(
        _ada_kernel,
        grid=(6 * D_MODEL // tn,),
        in_specs=[pl.BlockSpec((N_COND, D_MODEL), lambda j: (0, 0)),
                  pl.BlockSpec((D_MODEL, tn), lambda j: (0, j)),
                  pl.BlockSpec((1, tn), lambda j: (0, j))],
        out_specs=pl.BlockSpec((N_COND, tn), lambda j: (0, j)),
        out_shape=jax.ShapeDtypeStruct((N_COND, 6 * D_MODEL), F32),
        compiler_params=_params(("arbitrary",)),
        name="ada_table",
    )(cond, ada_w, ada_b)


def _prenorm_kernel(xs_ref, xp_ref, g_ref, sh_ref, sc_ref, o_ref, *, tiles):
    def run(is_sample):
        y = _rms((xs_ref if is_sample else xp_ref)[...]) * g_ref[...]
        o_ref[...] = (y * (1.0 + sc_ref[...]) + sh_ref[...]).astype(o_ref.dtype)

    _on_owner(tiles.is_sample(), run)


def _prenorm(x_sample, x_prompt, gain, mod3):
    tiles = _ColumnTiles(256)
    return pl.pallas_call(
        functools.partial(_prenorm_kernel, tiles=tiles),
        grid=tiles.grid,
        in_specs=[tiles.sample_major(D_MODEL), tiles.prompt_major(D_MODEL),
                  tiles.const((1, D_MODEL)), tiles.mod(0), tiles.mod(1)],
        out_specs=tiles.tokens(D_MODEL),
        out_shape=jax.ShapeDtypeStruct((TIME_ROWS, CHAIN_BATCH * D_MODEL), BF16),
        compiler_params=_params(("arbitrary", "arbitrary")),
        name="prenorm",
    )(x_sample, x_prompt, gain, mod3, mod3)


def _mm_kernel(a_ref, w_ref, o_ref, *, act):
    acc = jnp.dot(a_ref[...], w_ref[...], preferred_element_type=F32)
    if act == "sigmoid":
        acc = jax.nn.sigmoid(acc)
    o_ref[...] = acc.astype(o_ref.dtype)


def _matmul(a, w, *, col0, n_cols, tn, out_dtype, act=None, tm=1024, name="matmul"):
    m, k = a.shape
    off = col0 // tn
    return pl.pallas_call(
        functools.partial(_mm_kernel, act=act),
        grid=(m // tm, n_cols // tn),
        in_specs=[pl.BlockSpec((tm, k), lambda i, j: (i, 0)),
                  pl.BlockSpec((k, tn), lambda i, j: (0, j + off))],
        out_specs=pl.BlockSpec((tm, tn), lambda i, j: (i, j)),
        out_shape=jax.ShapeDtypeStruct((m, n_cols), out_dtype),
        compiler_params=_params(("arbitrary", "arbitrary")),
        name=name,
    )(a, w)


def _convproj_kernel(hp_ref, h_ref, hn_ref, wb_ref, wc_ref, wx_ref, cw_ref, o_ref, *, tm):
    h = h_ref[...]
    h_ext = jnp.concatenate([hp_ref[...], h, hn_ref[...]], axis=0)
    cb = jnp.dot(h, wb_ref[...], preferred_element_type=F32)
    u_ext = (jnp.dot(h_ext, wc_ref[...], preferred_element_type=F32)
             * jnp.dot(h_ext, wx_ref[...], preferred_element_type=F32))
    u = u_ext[CONV_HALO:CONV_HALO + tm]
    u_prev = u_ext[CONV_HALO - CHAIN_BATCH:CONV_HALO - CHAIN_BATCH + tm]
    u_next = u_ext[CONV_HALO + CHAIN_BATCH:CONV_HALO + CHAIN_BATCH + tm]
    seg = jnp.where(pl.program_id(0) < N_SAMPLE // tm, GRID_W, SEQ_PROMPT)
    row = lax.broadcasted_iota(I32, u.shape, 0)
    pos = (pl.program_id(0) * (tm // CHAIN_BATCH) + row // CHAIN_BATCH) & (seg - 1)
    u_prev = jnp.where(pos == 0, 0.0, u_prev)
    u_next = jnp.where(pos == seg - 1, 0.0, u_next)
    cw = cw_ref[...]
    conv = cw[0:1, :] * u_prev + cw[1:2, :] * u + cw[2:3, :] * u_next
    o_ref[...] = (cb * conv).astype(o_ref.dtype)


def _conv_branch(h, w_in_bf, conv_w):
    tm, tn = 1024, 256
    nb = W_BRANCH // tn
    per = tm // CONV_HALO
    return pl.pallas_call(
        functools.partial(_convproj_kernel, tm=tm),
        grid=(N_TOK // tm, nb),
        in_specs=[pl.BlockSpec((CONV_HALO, D_MODEL), lambda i, j: (jnp.maximum(i * per - 1, 0), 0)),
                  pl.BlockSpec((tm, D_MODEL), lambda i, j: (i, 0)),
                  pl.BlockSpec((CONV_HALO, D_MODEL),
                               lambda i, j: (jnp.minimum((i + 1) * per, N_TOK // CONV_HALO - 1), 0)),
                  pl.BlockSpec((D_MODEL, tn), lambda i, j: (0, j)),
                  pl.BlockSpec((D_MODEL, tn), lambda i, j: (0, j + nb)),
                  pl.BlockSpec((D_MODEL, tn), lambda i, j: (0, j + 2 * nb)),
                  pl.BlockSpec((3, tn), lambda i, j: (0, j))],
        out_specs=pl.BlockSpec((tm, tn), lambda i, j: (i, j)),
        out_shape=jax.ShapeDtypeStruct((N_TOK, W_BRANCH), BF16),
        compiler_params=_params(("arbitrary", "arbitrary")),
        name="conv_branch",
    )(h, h, h, w_in_bf, w_in_bf, w_in_bf, conv_w)


def _lora_kernel(x_ref, wd_ref, wa_ref, wg_ref, w0_ref, a0_ref, wa_out_ref, g_ref):
    def group(i):
        return x_ref[:, i * LORA_PAD:(i + 1) * LORA_PAD]

    def decay(wl):
        return jnp.exp(-jax.nn.sigmoid(wl) * math.exp(-0.5))

    for d in range(2):
        dl = jnp.dot(jnp.tanh(group(d)).astype(BF16), wd_ref[d], preferred_element_type=F32)
        wa_out_ref[:, d * W_BRANCH:(d + 1) * W_BRANCH] = decay(w0_ref[d:d + 1, :] + dl)
        al = jnp.dot(group(2 + d).astype(BF16), wa_ref[d], preferred_element_type=F32)
        wa_out_ref[:, (2 + d) * W_BRANCH:(3 + d) * W_BRANCH] = jax.nn.sigmoid(a0_ref[d:d + 1, :] + al)
    gl = jax.nn.sigmoid(x_ref[:, 4 * LORA_PAD:]).astype(BF16)
    g_ref[...] = jnp.dot(gl, wg_ref[...], preferred_element_type=F32)


def _lora_stage(lora, wd2, wa2, wg2, w0, a0):
    tm = 512
    return pl.pallas_call(
        _lora_kernel,
        grid=(N_TOK // tm,),
        in_specs=[pl.BlockSpec((tm, LORA_W), lambda i: (i, 0)),
                  pl.BlockSpec((2, LORA_PAD, W_BRANCH), lambda i: (0, 0, 0)),
                  pl.BlockSpec((2, LORA_PAD, W_BRANCH), lambda i: (0, 0, 0)),
                  pl.BlockSpec((256, W_BRANCH), lambda i: (0, 0)),
                  pl.BlockSpec((2, W_BRANCH), lambda i: (0, 0)),
                  pl.BlockSpec((2, W_BRANCH), lambda i: (0, 0))],
        out_specs=[pl.BlockSpec((tm, 4 * W_BRANCH), lambda i: (i, 0)),
                   pl.BlockSpec((tm, W_BRANCH), lambda i: (i, 0))],
        out_shape=[jax.ShapeDtypeStruct((N_TOK, 4 * W_BRANCH), F32),
                   jax.ShapeDtypeStruct((N_TOK, W_BRANCH), F32)],
        compiler_params=_params(("arbitrary",)),
        name="lora_stage",
    )(lora, wd2, wa2, wg2, w0, a0)


def _chain_tiles(x_ref, t0):
    both = (x_ref[t0], x_ref[t0 + 1])
    rows = [both[tl][:, p * LANES:(p + 1) * LANES] for tl in range(2) for p in range(CHAIN_BATCH)]
    sq = jnp.concatenate(rows, axis=0).T
    top, bot = sq[:HEAD], sq[HEAD:]
    low = lax.broadcasted_iota(I32, (HEAD, LANES), 1) < HEAD
    return (jnp.where(low, top, pltpu.roll(bot, HEAD, 1)),
            jnp.where(low, pltpu.roll(top, HEAD, 1), bot))


def _scan_kernel(r_ref, k_ref, v_ref, w_ref, a_ref, s0_ref, kk_ref, ka_ref, rk_ref,
                 y_ref, bon_ref, sf_ref, s_ref, ops_a, ops_b, *, tc):
    d = pl.program_id(0)
    c = pl.program_id(2)
    n_pairs = tc // 2

    @pl.when(c == 0)
    def _():
        s_ref[...] = s0_ref[...]

    def first_row(pair_idx):
        return 2 * jnp.where(d == 0, pair_idx, n_pairs - 1 - pair_idx)

    def prepare(pair_idx, ops_ref):
        t0 = first_row(pair_idx)
        r2, k2, v2 = _chain_tiles(r_ref, t0), _chain_tiles(k_ref, t0), _chain_tiles(v_ref, t0)
        w2, a2 = _chain_tiles(w_ref, t0), _chain_tiles(a_ref, t0)
        for tl in range(2):
            kk = k2[tl] * kk_ref[...]
            kk = kk * lax.rsqrt(jnp.sum(kk * kk, axis=0, keepdims=True) + 1e-12)
            kd = k2[tl] * (1.0 + (a2[tl] - 1.0) * ka_ref[...])
            for q, val in enumerate((kk, kk * a2[tl], kd, w2[tl], r2[tl], v2[tl])):
                ops_ref[tl, q] = val
            bon_ref[t0 + tl] = jnp.sum(r2[tl] * kd * rk_ref[...], axis=0, keepdims=True) * v2[tl]

    def recur(pair_idx, ops_ref):
        t0 = first_row(pair_idx)
        for s in range(2):
            tl = jnp.where(d == 0, s, 1 - s)
            vt = ops_ref[tl, 5]
            sa = jnp.zeros((HEAD, LANES), F32)
            for j in range(HEAD):
                sa = sa + s_ref[j] * ops_ref[tl, 0, pl.ds(j, 1), :]
            y = jnp.zeros((HEAD, LANES), F32)
            for j in range(HEAD):
                sn = (s_ref[j] * ops_ref[tl, 3, pl.ds(j, 1), :] - sa * ops_ref[tl, 1, pl.ds(j, 1), :]
                      + vt * ops_ref[tl, 2, pl.ds(j, 1), :])
                s_ref[j] = sn
                y = y + sn * ops_ref[tl, 4, pl.ds(j, 1), :]
            y_ref[t0 + tl] = y

    prepare(0, ops_a)

    def two_pairs(i, carry):
        prepare(2 * i + 1, ops_b)
        recur(2 * i, ops_a)
        prepare(jnp.minimum(2 * i + 2, n_pairs - 1), ops_a)
        recur(2 * i + 1, ops_b)
        return carry

    lax.fori_loop(0, n_pairs // 2, two_pairs, 0)

    @pl.when(c == pl.num_programs(2) - 1)
    def _():
        sf_ref[...] = s_ref[...]


def _scan(rkv3, wa3, row0, seq, s0, kk_l, ka_l, rk_l, *, tc=32):
    g_n = s0.shape[1]
    nc = seq // tc

    def tchunk(d, c):
        return jnp.where(d == 0, c, nc - 1 - c)

    def tok(col):
        return pl.BlockSpec((tc, CHAIN_BATCH, W_BRANCH),
                            lambda d, g, c: (row0 // tc + g * nc + tchunk(d, c), 0, col))

    def tok_dir(col):
        return pl.BlockSpec((tc, CHAIN_BATCH, W_BRANCH),
                            lambda d, g, c: (row0 // tc + g * nc + tchunk(d, c), 0, col + d))

    chain = pl.BlockSpec((None, None, tc, HEAD, LANES), lambda d, g, c: (d, g, tchunk(d, c), 0, 0))
    state = pl.BlockSpec((None, None, HEAD, HEAD, LANES), lambda d, g, c: (d, g, 0, 0, 0))
    par = pl.BlockSpec((HEAD, LANES), lambda d, g, c: (0, 0))
    seq_shape = jax.ShapeDtypeStruct((2, g_n, seq, HEAD, LANES), F32)
    ops = pltpu.VMEM((2, 6, HEAD, LANES), F32)
    return pl.pallas_call(
        functools.partial(_scan_kernel, tc=tc),
        grid=(2, g_n, nc),
        in_specs=[tok(0), tok(1), tok(2), tok_dir(0), tok_dir(2), state, par, par, par],
        out_specs=[chain, chain, state],
        out_shape=[seq_shape, seq_shape, jax.ShapeDtypeStruct((2, g_n, HEAD, HEAD, LANES), F32)],
        scratch_shapes=[pltpu.VMEM((HEAD, HEAD, LANES), F32), ops, ops],
        compiler_params=_params(("arbitrary", "arbitrary", "arbitrary")),
        name="wkv7_scan",
    )(rkv3, rkv3, rkv3, wa3, wa3, s0, kk_l, ka_l, rk_l)


def _scan_post_kernel(y_ref, bon_ref, lw_ref, lb_ref, o_ref, *, tc):
    low = lax.broadcasted_iota(I32, (HEAD, LANES), 1) < HEAD

    def pair(i, carry):
        t0 = 2 * i
        tiles = []
        for tl in range(2):
            ys = y_ref[0, t0 + tl] + y_ref[1, t0 + tl]
            dev = ys - jnp.mean(ys, axis=0, keepdims=True)
            var = jnp.mean(dev * dev, axis=0, keepdims=True)
            yn = dev * lax.rsqrt(var + GN_EPS) * lw_ref[...] + lb_ref[...]
            tiles.append(yn + bon_ref[0, t0 + tl] + bon_ref[1, t0 + tl])
        top = jnp.where(low, tiles[0], pltpu.roll(tiles[1], HEAD, 1))
        bot = jnp.where(low, pltpu.roll(tiles[0], HEAD, 1), tiles[1])
        sq = jnp.concatenate([top, bot], axis=0).T
        for tl in range(2):
            for p in range(CHAIN_BATCH):
                r0 = tl * HEAD + p * CHAIN_BATCH
                o_ref[t0 + tl, :, pl.ds(p * LANES, LANES)] = sq[r0:r0 + CHAIN_BATCH, :]
        return carry

    lax.fori_loop(0, tc // 2, pair, 0)


def _scan_post(y, bon, lw_l, lb_l):
    _, g_n, seq = y.shape[:3]
    tc = 64
    nc = seq // tc
    both = pl.BlockSpec((2, None, tc, HEAD, LANES), lambda g, c: (0, g, c, 0, 0))
    par = pl.BlockSpec((HEAD, LANES), lambda g, c: (0, 0))
    return pl.pallas_call(
        functools.partial(_scan_post_kernel, tc=tc),
        grid=(g_n, nc),
        in_specs=[both, both, par, par],
        out_specs=pl.BlockSpec((tc, CHAIN_BATCH, W_BRANCH), lambda g, c: (g * nc + c, 0, 0)),
        out_shape=jax.ShapeDtypeStruct((g_n * seq, CHAIN_BATCH, W_BRANCH), F32),
        compiler_params=_params(("arbitrary", "arbitrary")),
        name="scan_post",
    )(y, bon, lw_l, lb_l)


def _state_to_chain_layout(s):
    g_n = s.shape[0] // CHAIN_BATCH
    s = s.reshape(g_n, CHAIN_BATCH, N_HEADS // 2, 2, HEAD, HEAD)
    return jnp.transpose(s, (0, 5, 4, 3, 2, 1)).reshape(g_n, HEAD, HEAD, LANES)


def _state_from_chain_layout(s):
    g_n = s.shape[0]
    s = s.reshape(g_n, HEAD, HEAD, 2, N_HEADS // 2, CHAIN_BATCH)
    return jnp.transpose(s, (0, 5, 4, 3, 2, 1)).reshape(g_n * CHAIN_BATCH, N_HEADS, HEAD, HEAD)


def _head_param_to_chain_layout(p):
    p = jnp.transpose(p.reshape(N_HEADS // 2, 2, HEAD), (1, 0, 2))
    p = jnp.broadcast_to(p[:, :, None, :], (2, N_HEADS // 2, CHAIN_BATCH, HEAD))
    return p.reshape(LANES, HEAD).T


def _merge_kernel(z_ref, ybs_ref, ybp_ref, g_ref, ga_ref, gb_ref, wc_ref, wr_ref, o_ref, *, tm):
    def run(is_sample):
        y_a = jnp.dot(z_ref[...], wc_ref[...], preferred_element_type=F32)
        yb = ((ybs_ref if is_sample else ybp_ref)[...] * g_ref[...]).astype(BF16)
        y_b = jnp.dot(yb, wr_ref[...], preferred_element_type=F32)
        o_ref[...] = (ga_ref[...].astype(F32) * y_a + gb_ref[...].astype(F32) * y_b).astype(o_ref.dtype)

    _on_owner(pl.program_id(0) < N_SAMPLE // tm, run)


def _merge(z, yb_s, yb_p, g, gates, w_conv_bf, w_rwkv_bf):
    tm = 512
    row = lambda w: pl.BlockSpec((tm, w), lambda i: (i, 0))
    return pl.pallas_call(
        functools.partial(_merge_kernel, tm=tm),
        grid=(N_TOK // tm,),
        in_specs=[row(W_BRANCH), *_split_specs(tm, W_BRANCH), row(W_BRANCH),
                  pl.BlockSpec((tm, D_MODEL), lambda i: (i, 0)),
                  pl.BlockSpec((tm, D_MODEL), lambda i: (i, 1)),
                  pl.BlockSpec((W_BRANCH, D_MODEL), lambda i: (0, 0)),
                  pl.BlockSpec((W_BRANCH, D_MODEL), lambda i: (0, 0))],
        out_specs=row(D_MODEL),
        out_shape=jax.ShapeDtypeStruct((N_TOK, D_MODEL), BF16),
        compiler_params=_params(("arbitrary",)),
        name="merge",
    )(z, yb_s, yb_p, g, gates, gates, w_conv_bf, w_rwkv_bf)


def _outproj_kernel(m_ref, wo_ref, xs_ref, xp_ref, g1_ref, sh2_ref, sc2_ref, npost_ref, npre_ref,
                    x1_ref, h2_ref, *, tiles):
    def run(is_sample):
        out = jnp.dot(m_ref[...], wo_ref[...], preferred_element_type=F32)
        x1 = (xs_ref if is_sample else xp_ref)[...] + g1_ref[...] * (_rms(out) * npost_ref[...])
        x1_ref[...] = x1
        h2_ref[...] = (_rms(x1) * npre_ref[...]) * (1.0 + sc2_ref[...]) + sh2_ref[...]

    _on_owner(tiles.is_sample(), run)


def _outproj(merged, w_o_bf, x_sample, x_prompt, mod3, n_post, n_pre):
    tiles = _ColumnTiles(256)
    row = tiles.tokens(D_MODEL)
    vec = tiles.const((1, D_MODEL))
    out = jax.ShapeDtypeStruct((TIME_ROWS, CHAIN_BATCH * D_MODEL), F32)
    return pl.pallas_call(
        functools.partial(_outproj_kernel, tiles=tiles),
        grid=tiles.grid,
        in_specs=[row, tiles.const((D_MODEL, D_MODEL)),
                  tiles.sample_major(D_MODEL), tiles.prompt_major(D_MODEL),
                  tiles.mod(2), tiles.mod(3), tiles.mod(4), vec, vec],
        out_specs=[row, row],
        out_shape=[out, out],
        compiler_params=_params(("arbitrary", "arbitrary")),
        name="outproj",
    )(merged, w_o_bf, x_sample, x_prompt, mod3, mod3, mod3, n_post, n_pre)


def _first_index_of_max(x, axis, n):
    m = jnp.max(x, axis=axis, keepdims=True)
    idx = lax.broadcasted_iota(I32, x.shape, axis).astype(F32)
    first = jnp.min(jnp.where(x == m, idx, float(n)), axis=axis, keepdims=True)
    return m, idx, first


def _router_kernel(h_ref, rw_ref, rb_ref, eidx_ref, wsel_ref, rank_ref, cnt_ref, base_ref, *, tm):
    @pl.when(pl.program_id(0) == 0)
    def _():
        base_ref[...] = jnp.zeros_like(base_ref)

    logits = lax.dot_general(rw_ref[...], h_ref[...], (((1,), (1,)), ((), ())),
                             precision=lax.Precision.HIGHEST, preferred_element_type=F32)
    scores = jax.nn.sigmoid(logits)
    biased = scores + rb_ref[...]
    neg = -jnp.inf

    per_group = N_EXPERTS // N_GROUPS
    grp = biased.reshape(N_GROUPS, per_group, tm)
    m1, idx, first = _first_index_of_max(grp, 1, per_group)
    m2 = jnp.max(jnp.where(idx == first, neg, grp), axis=1, keepdims=True)
    gscore = (m1 + m2).reshape(N_GROUPS, tm)

    gsel = jnp.zeros((N_GROUPS, tm), F32)
    for _ in range(TOPK_GROUPS):
        _, gidx, gfirst = _first_index_of_max(gscore, 0, N_GROUPS)
        hit = gidx == gfirst
        gsel = jnp.where(hit, 1.0, gsel)
        gscore = jnp.where(hit, neg, gscore)
    emask = jnp.broadcast_to(gsel[:, None, :], (N_GROUPS, per_group, tm)).reshape(N_EXPERTS, tm)

    cand = jnp.where(emask > 0.5, biased, neg)
    mem = jnp.zeros((N_EXPERTS, tm), F32)
    picks = []
    for _ in range(TOP_K):
        _, eidx, efirst = _first_index_of_max(cand, 0, N_EXPERTS)
        hit = eidx == efirst
        mem = jnp.where(hit, 1.0, mem)
        cand = jnp.where(hit, neg, cand)
        picks.append((efirst, hit))

    s_i = lax.broadcasted_iota(I32, (tm, tm), 0)
    t_i = lax.broadcasted_iota(I32, (tm, tm), 1)
    upper = (s_i <= t_i).astype(BF16)
    incl = jnp.dot(mem.astype(BF16), upper, preferred_element_type=F32)
    rank = base_ref[...] + incl - mem
    base_ref[...] = base_ref[...] + jnp.sum(mem, axis=1, keepdims=True)
    cnt_ref[...] = base_ref[...]

    wsum = jnp.zeros((1, tm), F32)
    wrows = []
    for j, (efirst, hit) in enumerate(picks):
        wj = jnp.sum(jnp.where(hit, scores, 0.0), axis=0, keepdims=True)
        wrows.append(wj)
        wsum = wsum + wj
        eidx_ref[pl.ds(j, 1), :] = efirst.astype(I32)
        rank_ref[pl.ds(j, 1), :] = jnp.sum(jnp.where(hit, rank, 0.0), axis=0, keepdims=True).astype(I32)
    for j, wj in enumerate(wrows):
        wsel_ref[pl.ds(j, 1), :] = wj / wsum * ROUTED_SCALE


def _router(h2, rw_t, rb_col):
    tm = 256
    tok = lambda dt: jax.ShapeDtypeStruct((TOP_K, N_TOK), dt)
    tspec = pl.BlockSpec((TOP_K, tm), lambda i: (0, i))
    return pl.pallas_call(
        functools.partial(_router_kernel, tm=tm),
        grid=(N_TOK // tm,),
        in_specs=[pl.BlockSpec((tm, D_MODEL), lambda i: (i, 0)),
                  pl.BlockSpec((N_EXPERTS, D_MODEL), lambda i: (0, 0)),
                  pl.BlockSpec((N_EXPERTS, 1), lambda i: (0, 0))],
        out_specs=[tspec, tspec, tspec, pl.BlockSpec((N_EXPERTS, 1), lambda i: (0, 0))],
        out_shape=[tok(I32), tok(F32), tok(I32), jax.ShapeDtypeStruct((N_EXPERTS, 1), F32)],
        scratch_shapes=[pltpu.VMEM((N_EXPERTS, 1), F32)],
        compiler_params=_params(("arbitrary",)),
        name="router",
    )(h2, rw_t, rb_col)


def _slot_kernel(start_ref, eidx_ref, rank_ref, pos_ref):
    e = eidx_ref[...]
    pos = rank_ref[...]
    for j in range(N_EXPERTS):
        pos = pos + jnp.where(e == j, start_ref[j], 0)
    pos_ref[...] = pos


def _slots(pad_start, eidx, rank):
    full = pl.BlockSpec((TOP_K, N_TOK), lambda i, s: (0, 0))
    return pl.pallas_call(
        _slot_kernel,
        grid_spec=pltpu.PrefetchScalarGridSpec(num_scalar_prefetch=1, grid=(1,),
                                               in_specs=[full, full], out_specs=full),
        out_shape=jax.ShapeDtypeStruct((TOP_K, N_TOK), I32),
        compiler_params=_params(("arbitrary",)),
        name="slots",
    )(pad_start, eidx, rank)


def _gather_rows(idx_ref, base, n_rows, src_hbm, dst, sem, first_row=None):
    def issue(i, carry):
        pltpu.make_async_copy(src_hbm.at[pl.ds(idx_ref[base + i], 1), :], dst.at[pl.ds(i, 1), :], sem).start()
        return carry

    if first_row is not None:
        for i in range(first_row, first_row + n_rows):
            issue(i, 0)
    else:
        lax.fori_loop(0, n_rows, issue, 0, unroll=DMA_UNROLL)


def _wait_rows(n_rows, src_hbm, dst, sem):
    pltpu.make_async_copy(src_hbm.at[pl.ds(0, n_rows), :], dst, sem).wait()


def _expert_kernel(be_ref, first_ref, nused_ref, tok_ref,
                   h_hbm, wg_ref, wu_ref, wd_ref, o_ref, xbuf, wgb, wub, wdb, sem):
    b = pl.program_id(0)
    n_used = nused_ref[0]
    slot = b % 2

    @pl.when(b == 0)
    def _():
        _gather_rows(tok_ref, 0, MOE_ROWS, h_hbm, xbuf.at[0], sem.at[0])

    @pl.when(b < n_used)
    def _():
        @pl.when(first_ref[b] == 1)
        def _():
            wgb[...] = wg_ref[...].astype(BF16)
            wub[...] = wu_ref[...].astype(BF16)
            wdb[...] = wd_ref[...].astype(BF16)

        _wait_rows(MOE_ROWS, h_hbm, xbuf.at[slot], sem.at[slot])
        x = xbuf[slot].astype(BF16)

        rows_per_group = MOE_ROWS // EXPERT_PIECES

        def lookahead(piece):
            _gather_rows(tok_ref, (b + 1) * MOE_ROWS, rows_per_group, h_hbm, xbuf.at[1 - slot],
                         sem.at[1 - slot], first_row=piece * rows_per_group)

        half = D_EXPERT // 2
        hid = []
        for n in range(2):
            cols = slice(n * half, (n + 1) * half)
            lookahead(2 * n)
            gate = jnp.dot(x, wgb[:, cols], preferred_element_type=F32)
            lookahead(2 * n + 1)
            up = jnp.dot(x, wub[:, cols], preferred_element_type=F32)
            hid.append((_silu(gate) * up).astype(BF16))
        hid = jnp.concatenate(hid, axis=1)
        n_out = EXPERT_PIECES - 4
        width = D_MODEL // n_out
        for n in range(n_out):
            cols = slice(n * width, (n + 1) * width)
            lookahead(4 + n)
            o_ref[:, cols] = jnp.dot(hid, wdb[:, cols], preferred_element_type=F32)

    @pl.when(b >= n_used)
    def _():
        @pl.when(b == n_used)
        def _():
            _wait_rows(MOE_ROWS, h_hbm, xbuf.at[slot], sem.at[slot])

        o_ref[...] = jnp.zeros_like(o_ref)


def _experts(blk_expert, blk_first, n_used, tok_of_slot, h2, exp_gate, exp_up, exp_down):
    wspec_in = pl.BlockSpec((None, D_MODEL, D_EXPERT), lambda b, be, fi, nu, tk: (be[b], 0, 0))
    wspec_out = pl.BlockSpec((None, D_EXPERT, D_MODEL), lambda b, be, fi, nu, tk: (be[b], 0, 0))
    grid_spec = pltpu.PrefetchScalarGridSpec(
        num_scalar_prefetch=4,
        grid=(MOE_BLOCKS,),
        in_specs=[pl.BlockSpec(memory_space=pl.ANY), wspec_in, wspec_in, wspec_out],
        out_specs=pl.BlockSpec((MOE_ROWS, D_MODEL), lambda b, be, fi, nu, tk: (b, 0)),
        scratch_shapes=[pltpu.VMEM((2, MOE_ROWS, D_MODEL), F32),
                        pltpu.VMEM((D_MODEL, D_EXPERT), BF16),
                        pltpu.VMEM((D_MODEL, D_EXPERT), BF16),
                        pltpu.VMEM((D_EXPERT, D_MODEL), BF16),
                        pltpu.SemaphoreType.DMA((2,))],
    )
    return pl.pallas_call(
        _expert_kernel,
        grid_spec=grid_spec,
        out_shape=jax.ShapeDtypeStruct((MOE_BLOCKS * MOE_ROWS, D_MODEL), F32),
        compiler_params=_params(("arbitrary",)),
        name="experts",
    )(blk_expert, blk_first, n_used, tok_of_slot, h2, exp_gate, exp_up, exp_down)


def _combine_kernel(pos_ref, y_hbm, wt_ref, h_ref, x1_ref, g2_ref, npost_ref,
                    sg_ref, su_ref, sd_ref, os_ref, op_ref, buf, sem, *, tiles):
    i = pl.program_id(0) * pl.num_programs(1) + pl.program_id(1)
    n_steps = pl.num_programs(0) * pl.num_programs(1)
    slot = i % 2
    tm = tiles.tt
    rows = TOP_K * tm

    @pl.when(i == 0)
    def _():
        _gather_rows(pos_ref, 0, rows, y_hbm, buf.at[0], sem.at[0])

    @pl.when(i + 1 < n_steps)
    def _():
        _gather_rows(pos_ref, (i + 1) * rows, rows, y_hbm, buf.at[1 - slot], sem.at[1 - slot])

    hb = h_ref[...].astype(BF16)
    gate = jnp.dot(hb, sg_ref[...], preferred_element_type=F32)
    up = jnp.dot(hb, su_ref[...], preferred_element_type=F32)
    hid = (_silu(gate) * up).astype(BF16)
    f = jnp.dot(hid, sd_ref[...], preferred_element_type=F32)

    _wait_rows(rows, y_hbm, buf.at[slot], sem.at[slot])
    wt = wt_ref[...]
    for j in range(TOP_K):
        f = f + buf[slot, pl.ds(j * tm, tm), :] * wt[:, j:j + 1]
    out = x1_ref[...] + g2_ref[...] * (_rms(f) * npost_ref[...])

    def store(is_sample):
        (os_ref if is_sample else op_ref)[...] = out

    _on_owner(tiles.is_sample(), store)


def _combine(pos_tiles, y_sorted, wsel_tiles, h2, x1, mod3, n_post, sg_bf, su_bf, sd_bf):
    tiles = _ColumnTiles(COMBINE_ROWS)
    tm = tiles.tt
    row = tiles.tokens(D_MODEL)
    grid_spec = pltpu.PrefetchScalarGridSpec(
        num_scalar_prefetch=1,
        grid=tiles.grid,
        in_specs=[pl.BlockSpec(memory_space=pl.ANY),
                  pl.BlockSpec((tm, TOP_K), lambda j, b, p: (j * CHAIN_BATCH + b, 0)),
                  row, row, tiles.mod(5),
                  tiles.const((1, D_MODEL)), tiles.const((D_MODEL, D_EXPERT)),
                  tiles.const((D_MODEL, D_EXPERT)), tiles.const((D_EXPERT, D_MODEL))],
        out_specs=[tiles.sample_major(D_MODEL), tiles.prompt_major(D_MODEL)],
        scratch_shapes=[pltpu.VMEM((2, TOP_K * tm, D_MODEL), F32), pltpu.SemaphoreType.DMA((2,))],
    )
    return pl.pallas_call(
        functools.partial(_combine_kernel, tiles=tiles),
        grid_spec=grid_spec,
        out_shape=[jax.ShapeDtypeStruct((N_SAMPLE // SEQ_SAMPLE, SEQ_SAMPLE, D_MODEL), F32),
                   jax.ShapeDtypeStruct((N_PROMPT // SEQ_PROMPT, SEQ_PROMPT, D_MODEL), F32)],
        compiler_params=_params(("arbitrary", "arbitrary")),
        name="combine",
    )(pos_tiles, y_sorted, wsel_tiles, h2, x1, mod3, n_post, sg_bf, su_bf, sd_bf)


def _pad_rows(w, rows):
    return jnp.pad(w, ((0, rows - w.shape[0]), (0, 0)))


def kernel(x_prompt, x_sample, state_fwd, state_bwd, c, c_ctx, ada_w, ada_b, norm_pre_mix, norm_post_mix, norm_pre_ffn, norm_post_ffn, w_in, conv_w, w_out_conv, decay_w0, decay_w2, iclr_a0, iclr_a2, gate_g2, k_k, k_a, r_k, lnx_w, lnx_b, w_out_rwkv, w_o, router_w, router_bias, exp_gate, exp_up, exp_down, sh_gate, sh_up, sh_down):
    cond = jnp.concatenate([c_ctx[None, :], c, jnp.zeros((N_COND - 1 - c.shape[0], D_MODEL), F32)], axis=0)
    mod3 = _ada_table(cond, ada_w[0], ada_b).reshape(N_COND, 1, 6 * D_MODEL)

    h = _prenorm(x_sample, x_prompt, norm_pre_mix, mod3).reshape(N_TOK, D_MODEL)
    w_in_bf = w_in[0].astype(BF16)
    z = _conv_branch(h, w_in_bf, conv_w[0])
    rkv = _matmul(h, w_in_bf, col0=3 * W_BRANCH, n_cols=3 * W_BRANCH, tn=512, out_dtype=F32, name="proj_rkv")
    c0 = 6 * W_BRANCH
    pad_cols = lambda lo: jnp.pad(w_in_bf[:, lo:lo + 96], ((0, 0), (0, LORA_PAD - 96)))
    w_lora = jnp.concatenate([pad_cols(c0), pad_cols(c0 + 96), pad_cols(c0 + 192), pad_cols(c0 + 288),
                              w_in_bf[:, c0 + 384:c0 + 640]], axis=1)
    lora = _matmul(h, w_lora, col0=0, n_cols=LORA_W, tn=LORA_W, out_dtype=F32, name="proj_lora")
    gates = _matmul(h, w_in_bf[:, c0 + 640:], col0=0, n_cols=2 * D_MODEL, tn=512,
                    out_dtype=BF16, act="sigmoid", name="proj_gates")

    wd2 = jnp.stack([_pad_rows(decay_w2[0, 0], LORA_PAD), _pad_rows(decay_w2[0, 1], LORA_PAD)]).astype(BF16)
    wa2 = jnp.stack([_pad_rows(iclr_a2[0, 0], LORA_PAD), _pad_rows(iclr_a2[0, 1], LORA_PAD)]).astype(BF16)
    wa, g = _lora_stage(lora, wd2, wa2, gate_g2[0].astype(BF16), decay_w0[0], iclr_a0[0])

    kk_l = _head_param_to_chain_layout(k_k[0])
    ka_l = _head_param_to_chain_layout(k_a[0])
    rk_l = _head_param_to_chain_layout(r_k[0].reshape(-1))
    lw_l = _head_param_to_chain_layout(lnx_w[0])
    lb_l = _head_param_to_chain_layout(lnx_b[0])

    rkv3 = rkv.reshape(TIME_ROWS, CHAIN_BATCH, 3 * W_BRANCH)
    wa3 = wa.reshape(TIME_ROWS, CHAIN_BATCH, 4 * W_BRANCH)

    def run_scan(row0, seq, s0_f, s0_b):
        s0 = jnp.stack([_state_to_chain_layout(s0_f), _state_to_chain_layout(s0_b)])
        y, bon, s_fin = _scan(rkv3, wa3, row0, seq, s0, kk_l, ka_l, rk_l)
        return _scan_post(y, bon, lw_l, lb_l).reshape(-1, W_BRANCH), s_fin

    zero_state = jnp.zeros((N_PROMPT // SEQ_PROMPT, N_HEADS, HEAD, HEAD), F32)
    yb_s, _ = run_scan(0, SEQ_SAMPLE, state_fwd[:, 0], state_bwd[:, 0])
    yb_p, s_fin = run_scan(SEQ_SAMPLE, SEQ_PROMPT, zero_state, zero_state)

    merged = _merge(z, yb_s, yb_p, g, gates, w_out_conv[0].astype(BF16), w_out_rwkv[0].astype(BF16))
    x1, h2 = _outproj(merged.reshape(TIME_ROWS, CHAIN_BATCH * D_MODEL), w_o[0].astype(BF16),
                      x_sample, x_prompt, mod3, norm_post_mix, norm_pre_ffn)

    h2_flat = h2.reshape(N_TOK, D_MODEL)
    eidx, wsel, rank, counts = _router(h2_flat, router_w[0].T, router_bias[0][:, None])
    counts = counts[:, 0].astype(I32)
    padded = (counts + MOE_ROWS - 1) // MOE_ROWS * MOE_ROWS
    pad_end = jnp.cumsum(padded)
    pad_start = pad_end - padded
    pos = _slots(pad_start, eidx, rank)
    tok_ids = jnp.broadcast_to(jnp.arange(N_TOK, dtype=I32)[None, :], (TOP_K, N_TOK))
    tok_of_slot = jnp.zeros((MOE_BLOCKS * MOE_ROWS,), I32).at[pos.reshape(-1)].set(
        tok_ids.reshape(-1), unique_indices=True, mode="promise_in_bounds")
    blk_start = jnp.arange(MOE_BLOCKS, dtype=I32) * MOE_ROWS
    blk_expert = jnp.minimum(jnp.sum(pad_end[None, :] <= blk_start[:, None], axis=1), N_EXPERTS - 1).astype(I32)
    n_used = (pad_end[-1] // MOE_ROWS).astype(I32).reshape(1)
    blk_first = (blk_start == pad_start[blk_expert]).astype(I32)
    y_sorted = _experts(blk_expert, blk_first, n_used, tok_of_slot, h2_flat, exp_gate[0], exp_up[0], exp_down[0])

    def tile_order(a, perm):
        a = a.reshape(TOP_K, TIME_ROWS // COMBINE_ROWS, COMBINE_ROWS, CHAIN_BATCH)
        return jnp.transpose(a, perm)

    pos_tiles = tile_order(pos, (1, 3, 0, 2)).reshape(-1)
    wsel_tiles = tile_order(wsel, (1, 3, 2, 0)).reshape(N_TOK, TOP_K)
    out_s, out_p = _combine(pos_tiles, y_sorted, wsel_tiles, h2, x1, mod3, norm_post_ffn,
                            sh_gate[0].astype(BF16), sh_up[0].astype(BF16), sh_down[0].astype(BF16))

    new_f = _state_from_chain_layout(s_fin[0])[:, None]
    new_b = _state_from_chain_layout(s_fin[1])[:, None]
    return (out_p, out_s, new_f, new_b)
```

```python
import functools
import math

import jax
import jax.numpy as jnp
from jax import lax
from jax.experimental import pallas as pl
from jax.experimental.pallas import tpu as pltpu

F32 = jnp.float32
BF16 = jnp.bfloat16
I32 = jnp.int32

D_MODEL = 2048
N_SAMPLE = 8 * 1024
SEQ_SAMPLE = 1024
N_PROMPT = 16 * 256
SEQ_PROMPT = 256
N_TOK = N_SAMPLE + N_PROMPT
GRID_W = 64
W_BRANCH = 1024
HEAD = 64
N_HEADS = 16
N_COND = 16
LORA_PAD = 128
LORA_W = 4 * LORA_PAD + 256
N_EXPERTS = 64
TOP_K = 8
N_GROUPS = 8
TOPK_GROUPS = 4
D_EXPERT = 512
ROUTED_SCALE = 2.5
NORM_EPS = 1e-6
GN_EPS = 64e-5
LANES = 128
CHAIN_BATCH = 8
TIME_ROWS = N_TOK // CHAIN_BATCH
CONV_HALO = 16
MOE_ROWS = 512
MOE_BLOCKS = N_TOK * TOP_K // MOE_ROWS + N_EXPERTS
EXPERT_PIECES = 8
COMBINE_ROWS = 128
DMA_UNROLL = 8
VMEM_LIMIT = 56 * 1024 * 1024


def _params(sem, vmem=VMEM_LIMIT):
    return pltpu.CompilerParams(dimension_semantics=sem, vmem_limit_bytes=vmem)


def _split_specs(tm, width):
    n_s = N_SAMPLE // tm
    return (pl.BlockSpec((tm, width), lambda i, *_: (jnp.minimum(i, n_s - 1), 0)),
            pl.BlockSpec((tm, width), lambda i, *_: (jnp.maximum(i - n_s, 0), 0)))


def _on_owner(is_sample, fn):
    pl.when(is_sample)(lambda: fn(True))
    pl.when(jnp.logical_not(is_sample))(lambda: fn(False))


class _ColumnTiles:
    def __init__(self, tt):
        self.tt = tt
        self.n_s = SEQ_SAMPLE // tt
        self.per_group = SEQ_PROMPT // tt
        self.grid = (TIME_ROWS // tt, CHAIN_BATCH)

    def is_sample(self):
        return pl.program_id(0) < self.n_s

    def tokens(self, width):
        return pl.BlockSpec((self.tt, width), lambda j, b, *_: (j, b))

    def sample_major(self, width):
        n_s = self.n_s
        return pl.BlockSpec((None, self.tt, width), lambda j, b, *_: (
            jnp.where(j < n_s, b, CHAIN_BATCH - 1), jnp.minimum(j, n_s - 1), 0))

    def prompt_major(self, width):
        n_s, per = self.n_s, self.per_group
        return pl.BlockSpec((None, self.tt, width), lambda j, b, *_: (
            jnp.where(j < n_s, 0, (j - n_s) // per * CHAIN_BATCH + b),
            jnp.where(j < n_s, 0, (j - n_s) % per), 0))

    def mod(self, chunk):
        n_s = self.n_s
        return pl.BlockSpec((None, 1, D_MODEL), lambda j, b, *_: (jnp.where(j < n_s, 1 + b, 0), 0, chunk))

    def const(self, shape):
        return pl.BlockSpec(shape, lambda j, b, *_: (0,) * len(shape))


def _rms(x):
    return x * lax.rsqrt(jnp.mean(x * x, axis=-1, keepdims=True) + NORM_EPS)


def _silu(x):
    return x * jax.nn.sigmoid(x)


def _ada_kernel(c_ref, w_ref, b_ref, o_ref):
    s = _silu(c_ref[...]).astype(BF16)
    o_ref[...] = jnp.dot(s, w_ref[...].astype(BF16), preferred_element_type=F32) + b_ref[...]


def _ada_table(cond, ada_w, ada_b):
    tn = 1536
    return pl.pallas_call(
        _ada_kernel,
        grid=(6 * D_MODEL // tn,),
        in_specs=[pl.BlockSpec((N_COND, D_MODEL), lambda j: (0, 0)),
                  pl.BlockSpec((D_MODEL, tn), lambda j: (0, j)),
                  pl.BlockSpec((1, tn), lambda j: (0, j))],
        out_specs=pl.BlockSpec((N_COND, tn), lambda j: (0, j)),
        out_shape=jax.ShapeDtypeStruct((N_COND, 6 * D_MODEL), F32),
        compiler_params=_params(("arbitrary",)),
        name="ada_table",
    )(cond, ada_w, ada_b)


def _prenorm_kernel(xs_ref, xp_ref, g_ref, sh_ref, sc_ref, o_ref, *, tiles):
    def run(is_sample):
        y = _rms((xs_ref if is_sample else xp_ref)[...]) * g_ref[...]
        o_ref[...] = (y * (1.0 + sc_ref[...]) + sh_ref[...]).astype(o_ref.dtype)

    _on_owner(tiles.is_sample(), run)


def _prenorm(x_sample, x_prompt, gain, mod3):
    tiles = _ColumnTiles(256)
    return pl.pallas_call(
        functools.partial(_prenorm_kernel, tiles=tiles),
        grid=tiles.grid,
        in_specs=[tiles.sample_major(D_MODEL), tiles.prompt_major(D_MODEL),
                  tiles.const((1, D_MODEL)), tiles.mod(0), tiles.mod(1)],
        out_specs=tiles.tokens(D_MODEL),
        out_shape=jax.ShapeDtypeStruct((TIME_ROWS, CHAIN_BATCH * D_MODEL), BF16),
        compiler_params=_params(("arbitrary", "arbitrary")),
        name="prenorm",
    )(x_sample, x_prompt, gain, mod3, mod3)


def _mm_kernel(a_ref, w_ref, o_ref, *, act):
    acc = jnp.dot(a_ref[...], w_ref[...], preferred_element_type=F32)
    if act == "sigmoid":
        acc = jax.nn.sigmoid(acc)
    o_ref[...] = acc.astype(o_ref.dtype)


def _matmul(a, w, *, col0, n_cols, tn, out_dtype, act=None, tm=1024, name="matmul"):
    m, k = a.shape
    off = col0 // tn
    return pl.pallas_call(
        functools.partial(_mm_kernel, act=act),
        grid=(m // tm, n_cols // tn),
        in_specs=[pl.BlockSpec((tm, k), lambda i, j: (i, 0)),
                  pl.BlockSpec((k, tn), lambda i, j: (0, j + off))],
        out_specs=pl.BlockSpec((tm, tn), lambda i, j: (i, j)),
        out_shape=jax.ShapeDtypeStruct((m, n_cols), out_dtype),
        compiler_params=_params(("arbitrary", "arbitrary")),
        name=name,
    )(a, w)


def _convproj_kernel(hp_ref, h_ref, hn_ref, wb_ref, wc_ref, wx_ref, cw_ref, o_ref, *, tm):
    h = h_ref[...]
    h_ext = jnp.concatenate([hp_ref[...], h, hn_ref[...]], axis=0)
    cb = jnp.dot(h, wb_ref[...], preferred_element_type=F32)
    u_ext = (jnp.dot(h_ext, wc_ref[...], preferred_element_type=F32)
             * jnp.dot(h_ext, wx_ref[...], preferred_element_type=F32))
    u = u_ext[CONV_HALO:CONV_HALO + tm]
    u_prev = u_ext[CONV_HALO - CHAIN_BATCH:CONV_HALO - CHAIN_BATCH + tm]
    u_next = u_ext[CONV_HALO + CHAIN_BATCH:CONV_HALO + CHAIN_BATCH + tm]
    seg = jnp.where(pl.program_id(0) < N_SAMPLE // tm, GRID_W, SEQ_PROMPT)
    row = lax.broadcasted_iota(I32, u.shape, 0)
    pos = (pl.program_id(0) * (tm // CHAIN_BATCH) + row // CHAIN_BATCH) & (seg - 1)
    u_prev = jnp.where(pos == 0, 0.0, u_prev)
    u_next = jnp.where(pos == seg - 1, 0.0, u_next)
    cw = cw_ref[...]
    conv = cw[0:1, :] * u_prev + cw[1:2, :] * u + cw[2:3, :] * u_next
    o_ref[...] = (cb * conv).astype(o_ref.dtype)


def _conv_branch(h, w_in_bf, conv_w):
    tm, tn = 1024, 256
    nb = W_BRANCH // tn
    per = tm // CONV_HALO
    return pl.pallas_call(
        functools.partial(_convproj_kernel, tm=tm),
        grid=(N_TOK // tm, nb),
        in_specs=[pl.BlockSpec((CONV_HALO, D_MODEL), lambda i, j: (jnp.maximum(i * per - 1, 0), 0)),
                  pl.BlockSpec((tm, D_MODEL), lambda i, j: (i, 0)),
                  pl.BlockSpec((CONV_HALO, D_MODEL),
                               lambda i, j: (jnp.minimum((i + 1) * per, N_TOK // CONV_HALO - 1), 0)),
                  pl.BlockSpec((D_MODEL, tn), lambda i, j: (0, j)),
                  pl.BlockSpec((D_MODEL, tn), lambda i, j: (0, j + nb)),
                  pl.BlockSpec((D_MODEL, tn), lambda i, j: (0, j + 2 * nb)),
                  pl.BlockSpec((3, tn), lambda i, j: (0, j))],
        out_specs=pl.BlockSpec((tm, tn), lambda i, j: (i, j)),
        out_shape=jax.ShapeDtypeStruct((N_TOK, W_BRANCH), BF16),
        compiler_params=_params(("arbitrary", "arbitrary")),
        name="conv_branch",
    )(h, h, h, w_in_bf, w_in_bf, w_in_bf, conv_w)


def _lora_kernel(x_ref, wd_ref, wa_ref, wg_ref, w0_ref, a0_ref, wa_out_ref, g_ref):
    def group(i):
        return x_ref[:, i * LORA_PAD:(i + 1) * LORA_PAD]

    def decay(wl):
        return jnp.exp(-jax.nn.sigmoid(wl) * math.exp(-0.5))

    for d in range(2):
        dl = jnp.dot(jnp.tanh(group(d)).astype(BF16), wd_ref[d], preferred_element_type=F32)
        wa_out_ref[:, d * W_BRANCH:(d + 1) * W_BRANCH] = decay(w0_ref[d:d + 1, :] + dl)
        al = jnp.dot(group(2 + d).astype(BF16), wa_ref[d], preferred_element_type=F32)
        wa_out_ref[:, (2 + d) * W_BRANCH:(3 + d) * W_BRANCH] = jax.nn.sigmoid(a0_ref[d:d + 1, :] + al)
    gl = jax.nn.sigmoid(x_ref[:, 4 * LORA_PAD:]).astype(BF16)
    g_ref[...] = jnp.dot(gl, wg_ref[...], preferred_element_type=F32)


def _lora_stage(lora, wd2, wa2, wg2, w0, a0):
    tm = 512
    return pl.pallas_call(
        _lora_kernel,
        grid=(N_TOK // tm,),
        in_specs=[pl.BlockSpec((tm, LORA_W), lambda i: (i, 0)),
                  pl.BlockSpec((2, LORA_PAD, W_BRANCH), lambda i: (0, 0, 0)),
                  pl.BlockSpec((2, LORA_PAD, W_BRANCH), lambda i: (0, 0, 0)),
                  pl.BlockSpec((256, W_BRANCH), lambda i: (0, 0)),
                  pl.BlockSpec((2, W_BRANCH), lambda i: (0, 0)),
                  pl.BlockSpec((2, W_BRANCH), lambda i: (0, 0))],
        out_specs=[pl.BlockSpec((tm, 4 * W_BRANCH), lambda i: (i, 0)),
                   pl.BlockSpec((tm, W_BRANCH), lambda i: (i, 0))],
        out_shape=[jax.ShapeDtypeStruct((N_TOK, 4 * W_BRANCH), F32),
                   jax.ShapeDtypeStruct((N_TOK, W_BRANCH), F32)],
        compiler_params=_params(("arbitrary",)),
        name="lora_stage",
    )(lora, wd2, wa2, wg2, w0, a0)


def _chain_tiles(x_ref, t0):
    both = (x_ref[t0], x_ref[t0 + 1])
    rows = [both[tl][:, p * LANES:(p + 1) * LANES] for tl in range(2) for p in range(CHAIN_BATCH)]
    sq = jnp.concatenate(rows, axis=0).T
    top, bot = sq[:HEAD], sq[HEAD:]
    low = lax.broadcasted_iota(I32, (HEAD, LANES), 1) < HEAD
    return (jnp.where(low, top, pltpu.roll(bot, HEAD, 1)),
            jnp.where(low, pltpu.roll(top, HEAD, 1), bot))


def _scan_kernel(r_ref, k_ref, v_ref, w_ref, a_ref, s0_ref, kk_ref, ka_ref, rk_ref,
                 y_ref, bon_ref, sf_ref, s_ref, p_ref, ops_a, ops_b, *, tc):
    d = pl.program_id(0)
    c = pl.program_id(2)
    n_pairs = tc // 2

    @pl.when(c == 0)
    def _():
        s_ref[...] = s0_ref[...]

    p_ref[...] = jnp.ones_like(p_ref)

    def first_row(pair_idx):
        return 2 * jnp.where(d == 0, pair_idx, n_pairs - 1 - pair_idx)

    def prepare(pair_idx, ops_ref):
        t0 = first_row(pair_idx)
        r2, k2, v2 = _chain_tiles(r_ref, t0), _chain_tiles(k_ref, t0), _chain_tiles(v_ref, t0)
        w2, a2 = _chain_tiles(w_ref, t0), _chain_tiles(a_ref, t0)
        fwd = d == 0
        p = p_ref[...]
        for s in range(2):
            r, k, v, w, a = (jnp.where(fwd, x[s], x[1 - s]) for x in (r2, k2, v2, w2, a2))
            kk = k * kk_ref[...]
            kk = kk * lax.rsqrt(jnp.sum(kk * kk, axis=0, keepdims=True) + 1e-12)
            kd = k * (1.0 + (a - 1.0) * ka_ref[...])
            bon_ref[t0 + jnp.where(fwd, s, 1 - s)] = jnp.sum(r * kd * rk_ref[...], axis=0, keepdims=True) * v
            p_new = p * w
            inv = 1.0 / p_new
            for q, val in enumerate((kk * p, kk * a * inv, kd * inv, r * p_new, v)):
                ops_ref[s, q] = val
            p = p_new
        p_ref[...] = p

    def recur(pair_idx, ops_ref):
        t0 = first_row(pair_idx)
        for s in range(2):
            vt = ops_ref[s, 4]
            sa = jnp.zeros((HEAD, LANES), F32)
            for j in range(HEAD):
                sa = sa + s_ref[j] * ops_ref[s, 0, pl.ds(j, 1), :]
            y = jnp.zeros((HEAD, LANES), F32)
            for j in range(HEAD):
                un = s_ref[j] - sa * ops_ref[s, 1, pl.ds(j, 1), :] + vt * ops_ref[s, 2, pl.ds(j, 1), :]
                s_ref[j] = un
                y = y + un * ops_ref[s, 3, pl.ds(j, 1), :]
            y_ref[t0 + jnp.where(d == 0, s, 1 - s)] = y

    prepare(0, ops_a)

    def two_pairs(i, carry):
        prepare(2 * i + 1, ops_b)
        recur(2 * i, ops_a)
        prepare(2 * i + 2, ops_a)
        recur(2 * i + 1, ops_b)
        return carry

    lax.fori_loop(0, n_pairs // 2 - 1, two_pairs, 0)
    prepare(n_pairs - 1, ops_b)
    recur(n_pairs - 2, ops_a)
    recur(n_pairs - 1, ops_b)

    for j in range(HEAD):
        s_ref[j] = s_ref[j] * p_ref[pl.ds(j, 1), :]

    @pl.when(c == pl.num_programs(2) - 1)
    def _():
        sf_ref[...] = s_ref[...]


def _scan(rkv3, wa3, row0, seq, s0, kk_l, ka_l, rk_l, *, tc=32):
    g_n = s0.shape[1]
    nc = seq // tc

    def tchunk(d, c):
        return jnp.where(d == 0, c, nc - 1 - c)

    def tok(col):
        return pl.BlockSpec((tc, CHAIN_BATCH, W_BRANCH),
                            lambda d, g, c: (row0 // tc + g * nc + tchunk(d, c), 0, col))

    def tok_dir(col):
        return pl.BlockSpec((tc, CHAIN_BATCH, W_BRANCH),
                            lambda d, g, c: (row0 // tc + g * nc + tchunk(d, c), 0, col + d))

    chain = pl.BlockSpec((None, None, tc, HEAD, LANES), lambda d, g, c: (d, g, tchunk(d, c), 0, 0))
    state = pl.BlockSpec((None, None, HEAD, HEAD, LANES), lambda d, g, c: (d, g, 0, 0, 0))
    par = pl.BlockSpec((HEAD, LANES), lambda d, g, c: (0, 0))
    seq_shape = jax.ShapeDtypeStruct((2, g_n, seq, HEAD, LANES), F32)
    ops = pltpu.VMEM((2, 5, HEAD, LANES), F32)
    return pl.pallas_call(
        functools.partial(_scan_kernel, tc=tc),
        grid=(2, g_n, nc),
        in_specs=[tok(0), tok(1), tok(2), tok_dir(0), tok_dir(2), state, par, par, par],
        out_specs=[chain, chain, state],
        out_shape=[seq_shape, seq_shape, jax.ShapeDtypeStruct((2, g_n, HEAD, HEAD, LANES), F32)],
        scratch_shapes=[pltpu.VMEM((HEAD, HEAD, LANES), F32), pltpu.VMEM((HEAD, LANES), F32), ops, ops],
        compiler_params=_params(("arbitrary", "arbitrary", "arbitrary")),
        name="wkv7_scan",
    )(rkv3, rkv3, rkv3, wa3, wa3, s0, kk_l, ka_l, rk_l)


def _scan_post_kernel(y_ref, bon_ref, lw_ref, lb_ref, o_ref, *, tc):
    low = lax.broadcasted_iota(I32, (HEAD, LANES), 1) < HEAD

    def pair(i, carry):
        t0 = 2 * i
        tiles = []
        for tl in range(2):
            ys = y_ref[0, t0 + tl] + y_ref[1, t0 + tl]
            dev = ys - jnp.mean(ys, axis=0, keepdims=True)
            var = jnp.mean(dev * dev, axis=0, keepdims=True)
            yn = dev * lax.rsqrt(var + GN_EPS) * lw_ref[...] + lb_ref[...]
            tiles.append(yn + bon_ref[0, t0 + tl] + bon_ref[1, t0 + tl])
        top = jnp.where(low, tiles[0], pltpu.roll(tiles[1], HEAD, 1))
        bot = jnp.where(low, pltpu.roll(tiles[0], HEAD, 1), tiles[1])
        sq = jnp.concatenate([top, bot], axis=0).T
        for tl in range(2):
            for p in range(CHAIN_BATCH):
                r0 = tl * HEAD + p * CHAIN_BATCH
                o_ref[t0 + tl, :, pl.ds(p * LANES, LANES)] = sq[r0:r0 + CHAIN_BATCH, :]
        return carry

    lax.fori_loop(0, tc // 2, pair, 0)


def _scan_post(y, bon, lw_l, lb_l):
    _, g_n, seq = y.shape[:3]
    tc = 64
    nc = seq // tc
    both = pl.BlockSpec((2, None, tc, HEAD, LANES), lambda g, c: (0, g, c, 0, 0))
    par = pl.BlockSpec((HEAD, LANES), lambda g, c: (0, 0))
    return pl.pallas_call(
        functools.partial(_scan_post_kernel, tc=tc),
        grid=(g_n, nc),
        in_specs=[both, both, par, par],
        out_specs=pl.BlockSpec((tc, CHAIN_BATCH, W_BRANCH), lambda g, c: (g * nc + c, 0, 0)),
        out_shape=jax.ShapeDtypeStruct((g_n * seq, CHAIN_BATCH, W_BRANCH), F32),
        compiler_params=_params(("arbitrary", "arbitrary")),
        name="scan_post",
    )(y, bon, lw_l, lb_l)


def _state_to_chain_layout(s):
    g_n = s.shape[0] // CHAIN_BATCH
    s = s.reshape(g_n, CHAIN_BATCH, N_HEADS // 2, 2, HEAD, HEAD)
    return jnp.transpose(s, (0, 5, 4, 3, 2, 1)).reshape(g_n, HEAD, HEAD, LANES)


def _state_from_chain_layout(s):
    g_n = s.shape[0]
    s = s.reshape(g_n, HEAD, HEAD, 2, N_HEADS // 2, CHAIN_BATCH)
    return jnp.transpose(s, (0, 5, 4, 3, 2, 1)).reshape(g_n * CHAIN_BATCH, N_HEADS, HEAD, HEAD)


def _head_param_to_chain_layout(p):
    p = jnp.transpose(p.reshape(N_HEADS // 2, 2, HEAD), (1, 0, 2))
    p = jnp.broadcast_to(p[:, :, None, :], (2, N_HEADS // 2, CHAIN_BATCH, HEAD))
    return p.reshape(LANES, HEAD).T


def _merge_kernel(z_ref, ybs_ref, ybp_ref, g_ref, ga_ref, gb_ref, wc_ref, wr_ref, o_ref, *, tm):
    def run(is_sample):
        y_a = jnp.dot(z_ref[...], wc_ref[...], preferred_element_type=F32)
        yb = ((ybs_ref if is_sample else ybp_ref)[...] * g_ref[...]).astype(BF16)
        y_b = jnp.dot(yb, wr_ref[...], preferred_element_type=F32)
        o_ref[...] = (ga_ref[...].astype(F32) * y_a + gb_ref[...].astype(F32) * y_b).astype(o_ref.dtype)

    _on_owner(pl.program_id(0) < N_SAMPLE // tm, run)


def _merge(z, yb_s, yb_p, g, gates, w_conv_bf, w_rwkv_bf):
    tm = 512
    row = lambda w: pl.BlockSpec((tm, w), lambda i: (i, 0))
    return pl.pallas_call(
        functools.partial(_merge_kernel, tm=tm),
        grid=(N_TOK // tm,),
        in_specs=[row(W_BRANCH), *_split_specs(tm, W_BRANCH), row(W_BRANCH),
                  pl.BlockSpec((tm, D_MODEL), lambda i: (i, 0)),
                  pl.BlockSpec((tm, D_MODEL), lambda i: (i, 1)),
                  pl.BlockSpec((W_BRANCH, D_MODEL), lambda i: (0, 0)),
                  pl.BlockSpec((W_BRANCH, D_MODEL), lambda i: (0, 0))],
        out_specs=row(D_MODEL),
        out_shape=jax.ShapeDtypeStruct((N_TOK, D_MODEL), BF16),
        compiler_params=_params(("arbitrary",)),
        name="merge",
    )(z, yb_s, yb_p, g, gates, gates, w_conv_bf, w_rwkv_bf)


def _outproj_kernel(m_ref, wo_ref, xs_ref, xp_ref, g1_ref, sh2_ref, sc2_ref, npost_ref, npre_ref,
                    x1_ref, h2_ref, *, tiles):
    def run(is_sample):
        out = jnp.dot(m_ref[...], wo_ref[...], preferred_element_type=F32)
        x1 = (xs_ref if is_sample else xp_ref)[...] + g1_ref[...] * (_rms(out) * npost_ref[...])
        x1_ref[...] = x1
        h2_ref[...] = (_rms(x1) * npre_ref[...]) * (1.0 + sc2_ref[...]) + sh2_ref[...]

    _on_owner(tiles.is_sample(), run)


def _outproj(merged, w_o_bf, x_sample, x_prompt, mod3, n_post, n_pre):
    tiles = _ColumnTiles(256)
    row = tiles.tokens(D_MODEL)
    vec = tiles.const((1, D_MODEL))
    out = jax.ShapeDtypeStruct((TIME_ROWS, CHAIN_BATCH * D_MODEL), F32)
    return pl.pallas_call(
        functools.partial(_outproj_kernel, tiles=tiles),
        grid=tiles.grid,
        in_specs=[row, tiles.const((D_MODEL, D_MODEL)),
                  tiles.sample_major(D_MODEL), tiles.prompt_major(D_MODEL),
                  tiles.mod(2), tiles.mod(3), tiles.mod(4), vec, vec],
        out_specs=[row, row],
        out_shape=[out, out],
        compiler_params=_params(("arbitrary", "arbitrary")),
        name="outproj",
    )(merged, w_o_bf, x_sample, x_prompt, mod3, mod3, mod3, n_post, n_pre)


def _first_index_of_max(x, axis, n):
    m = jnp.max(x, axis=axis, keepdims=True)
    idx = lax.broadcasted_iota(I32, x.shape, axis).astype(F32)
    first = jnp.min(jnp.where(x == m, idx, float(n)), axis=axis, keepdims=True)
    return m, idx, first


def _router_kernel(h_ref, rw_ref, rb_ref, eidx_ref, wsel_ref, rank_ref, cnt_ref, base_ref, *, tm):
    @pl.when(pl.program_id(0) == 0)
    def _():
        base_ref[...] = jnp.zeros_like(base_ref)

    logits = lax.dot_general(rw_ref[...], h_ref[...], (((1,), (1,)), ((), ())),
                             precision=lax.Precision.HIGHEST, preferred_element_type=F32)
    scores = jax.nn.sigmoid(logits)
    biased = scores + rb_ref[...]
    neg = -jnp.inf

    per_group = N_EXPERTS // N_GROUPS
    grp = biased.reshape(N_GROUPS, per_group, tm)
    m1, idx, first = _first_index_of_max(grp, 1, per_group)
    m2 = jnp.max(jnp.where(idx == first, neg, grp), axis=1, keepdims=True)
    gscore = (m1 + m2).reshape(N_GROUPS, tm)

    gsel = jnp.zeros((N_GROUPS, tm), F32)
    for _ in range(TOPK_GROUPS):
        _, gidx, gfirst = _first_index_of_max(gscore, 0, N_GROUPS)
        hit = gidx == gfirst
        gsel = jnp.where(hit, 1.0, gsel)
        gscore = jnp.where(hit, neg, gscore)
    emask = jnp.broadcast_to(gsel[:, None, :], (N_GROUPS, per_group, tm)).reshape(N_EXPERTS, tm)

    cand = jnp.where(emask > 0.5, biased, neg)
    mem = jnp.zeros((N_EXPERTS, tm), F32)
    picks = []
    for _ in range(TOP_K):
        _, eidx, efirst = _first_index_of_max(cand, 0, N_EXPERTS)
        hit = eidx == efirst
        mem = jnp.where(hit, 1.0, mem)
        cand = jnp.where(hit, neg, cand)
        picks.append((efirst, hit))

    s_i = lax.broadcasted_iota(I32, (tm, tm), 0)
    t_i = lax.broadcasted_iota(I32, (tm, tm), 1)
    upper = (s_i <= t_i).astype(BF16)
    incl = jnp.dot(mem.astype(BF16), upper, preferred_element_type=F32)
    rank = base_ref[...] + incl - mem
    base_ref[...] = base_ref[...] + jnp.sum(mem, axis=1, keepdims=True)
    cnt_ref[...] = base_ref[...]

    wsum = jnp.zeros((1, tm), F32)
    wrows = []
    for j, (efirst, hit) in enumerate(picks):
        wj = jnp.sum(jnp.where(hit, scores, 0.0), axis=0, keepdims=True)
        wrows.append(wj)
        wsum = wsum + wj
        eidx_ref[pl.ds(j, 1), :] = efirst.astype(I32)
        rank_ref[pl.ds(j, 1), :] = jnp.sum(jnp.where(hit, rank, 0.0), axis=0, keepdims=True).astype(I32)
    for j, wj in enumerate(wrows):
        wsel_ref[pl.ds(j, 1), :] = wj / wsum * ROUTED_SCALE


def _router(h2, rw_t, rb_col):
    tm = 256
    tok = lambda dt: jax.ShapeDtypeStruct((TOP_K, N_TOK), dt)
    tspec = pl.BlockSpec((TOP_K, tm), lambda i: (0, i))
    return pl.pallas_call(
        functools.partial(_router_kernel, tm=tm),
        grid=(N_TOK // tm,),
        in_specs=[pl.BlockSpec((tm, D_MODEL), lambda i: (i, 0)),
                  pl.BlockSpec((N_EXPERTS, D_MODEL), lambda i: (0, 0)),
                  pl.BlockSpec((N_EXPERTS, 1), lambda i: (0, 0))],
        out_specs=[tspec, tspec, tspec, pl.BlockSpec((N_EXPERTS, 1), lambda i: (0, 0))],
        out_shape=[tok(I32), tok(F32), tok(I32), jax.ShapeDtypeStruct((N_EXPERTS, 1), F32)],
        scratch_shapes=[pltpu.VMEM((N_EXPERTS, 1), F32)],
        compiler_params=_params(("arbitrary",)),
        name="router",
    )(h2, rw_t, rb_col)


def _slot_kernel(start_ref, eidx_ref, rank_ref, pos_ref):
    e = eidx_ref[...]
    pos = rank_ref[...]
    for j in range(N_EXPERTS):
        pos = pos + jnp.where(e == j, start_ref[j], 0)
    pos_ref[...] = pos


def _slots(pad_start, eidx, rank):
    full = pl.BlockSpec((TOP_K, N_TOK), lambda i, s: (0, 0))
    return pl.pallas_call(
        _slot_kernel,
        grid_spec=pltpu.PrefetchScalarGridSpec(num_scalar_prefetch=1, grid=(1,),
                                               in_specs=[full, full], out_specs=full),
        out_shape=jax.ShapeDtypeStruct((TOP_K, N_TOK), I32),
        compiler_params=_params(("arbitrary",)),
        name="slots",
    )(pad_start, eidx, rank)


def _gather_rows(idx_ref, base, n_rows, src_hbm, dst, sem, first_row=None):
    def issue(i, carry):
        pltpu.make_async_copy(src_hbm.at[pl.ds(idx_ref[base + i], 1), :], dst.at[pl.ds(i, 1), :], sem).start()
        return carry

    if first_row is not None:
        for i in range(first_row, first_row + n_rows):
            issue(i, 0)
    else:
        lax.fori_loop(0, n_rows, issue, 0, unroll=DMA_UNROLL)


def _wait_rows(n_rows, src_hbm, dst, sem):
    pltpu.make_async_copy(src_hbm.at[pl.ds(0, n_rows), :], dst, sem).wait()


def _expert_kernel(be_ref, first_ref, nused_ref, tok_ref,
                   h_hbm, wg_ref, wu_ref, wd_ref, o_ref, xbuf, wgb, wub, wdb, sem):
    b = pl.program_id(0)
    n_used = nused_ref[0]
    slot = b % 2

    @pl.when(b == 0)
    def _():
        _gather_rows(tok_ref, 0, MOE_ROWS, h_hbm, xbuf.at[0], sem.at[0])

    @pl.when(b < n_used)
    def _():
        @pl.when(first_ref[b] == 1)
        def _():
            wgb[...] = wg_ref[...].astype(BF16)
            wub[...] = wu_ref[...].astype(BF16)
            wdb[...] = wd_ref[...].astype(BF16)

        _wait_rows(MOE_ROWS, h_hbm, xbuf.at[slot], sem.at[slot])
        x = xbuf[slot].astype(BF16)

        rows_per_group = MOE_ROWS // EXPERT_PIECES

        def lookahead(piece):
            _gather_rows(tok_ref, (b + 1) * MOE_ROWS, rows_per_group, h_hbm, xbuf.at[1 - slot],
                         sem.at[1 - slot], first_row=piece * rows_per_group)

        half = D_EXPERT // 2
        hid = []
        for n in range(2):
            cols = slice(n * half, (n + 1) * half)
            lookahead(2 * n)
            gate = jnp.dot(x, wgb[:, cols], preferred_element_type=F32)
            lookahead(2 * n + 1)
            up = jnp.dot(x, wub[:, cols], preferred_element_type=F32)
            hid.append((_silu(gate) * up).astype(BF16))
        hid = jnp.concatenate(hid, axis=1)
        n_out = EXPERT_PIECES - 4
        width = D_MODEL // n_out
        for n in range(n_out):
            cols = slice(n * width, (n + 1) * width)
            lookahead(4 + n)
            o_ref[:, cols] = jnp.dot(hid, wdb[:, cols], preferred_element_type=F32)

    @pl.when(b >= n_used)
    def _():
        @pl.when(b == n_used)
        def _():
            _wait_rows(MOE_ROWS, h_hbm, xbuf.at[slot], sem.at[slot])

        o_ref[...] = jnp.zeros_like(o_ref)


def _experts(blk_expert, blk_first, n_used, tok_of_slot, h2, exp_gate, exp_up, exp_down):
    wspec_in = pl.BlockSpec((None, D_MODEL, D_EXPERT), lambda b, be, fi, nu, tk: (be[b], 0, 0))
    wspec_out = pl.BlockSpec((None, D_EXPERT, D_MODEL), lambda b, be, fi, nu, tk: (be[b], 0, 0))
    grid_spec = pltpu.PrefetchScalarGridSpec(
        num_scalar_prefetch=4,
        grid=(MOE_BLOCKS,),
        in_specs=[pl.BlockSpec(memory_space=pl.ANY), wspec_in, wspec_in, wspec_out],
        out_specs=pl.BlockSpec((MOE_ROWS, D_MODEL), lambda b, be, fi, nu, tk: (b, 0)),
        scratch_shapes=[pltpu.VMEM((2, MOE_ROWS, D_MODEL), F32),
                        pltpu.VMEM((D_MODEL, D_EXPERT), BF16),
                        pltpu.VMEM((D_MODEL, D_EXPERT), BF16),
                        pltpu.VMEM((D_EXPERT, D_MODEL), BF16),
                        pltpu.SemaphoreType.DMA((2,))],
    )
    return pl.pallas_call(
        _expert_kernel,
        grid_spec=grid_spec,
        out_shape=jax.ShapeDtypeStruct((MOE_BLOCKS * MOE_ROWS, D_MODEL), F32),
        compiler_params=_params(("arbitrary",)),
        name="experts",
    )(blk_expert, blk_first, n_used, tok_of_slot, h2, exp_gate, exp_up, exp_down)


def _combine_kernel(pos_ref, y_hbm, wt_ref, h_ref, x1_ref, g2_ref, npost_ref,
                    sg_ref, su_ref, sd_ref, os_ref, op_ref, buf, sem, *, tiles):
    i = pl.program_id(0) * pl.num_programs(1) + pl.program_id(1)
    n_steps = pl.num_programs(0) * pl.num_programs(1)
    slot = i % 2
    tm = tiles.tt
    rows = TOP_K * tm

    @pl.when(i == 0)
    def _():
        _gather_rows(pos_ref, 0, rows, y_hbm, buf.at[0], sem.at[0])

    @pl.when(i + 1 < n_steps)
    def _():
        _gather_rows(pos_ref, (i + 1) * rows, rows, y_hbm, buf.at[1 - slot], sem.at[1 - slot])

    hb = h_ref[...].astype(BF16)
    gate = jnp.dot(hb, sg_ref[...], preferred_element_type=F32)
    up = jnp.dot(hb, su_ref[...], preferred_element_type=F32)
    hid = (_silu(gate) * up).astype(BF16)
    f = jnp.dot(hid, sd_ref[...], preferred_element_type=F32)

    _wait_rows(rows, y_hbm, buf.at[slot], sem.at[slot])
    wt = wt_ref[...]
    for j in range(TOP_K):
        f = f + buf[slot, pl.ds(j * tm, tm), :] * wt[:, j:j + 1]
    out = x1_ref[...] + g2_ref[...] * (_rms(f) * npost_ref[...])

    def store(is_sample):
        (os_ref if is_sample else op_ref)[...] = out

    _on_owner(tiles.is_sample(), store)


def _combine(pos_tiles, y_sorted, wsel_tiles, h2, x1, mod3, n_post, sg_bf, su_bf, sd_bf):
    tiles = _ColumnTiles(COMBINE_ROWS)
    tm = tiles.tt
    row = tiles.tokens(D_MODEL)
    grid_spec = pltpu.PrefetchScalarGridSpec(
        num_scalar_prefetch=1,
        grid=tiles.grid,
        in_specs=[pl.BlockSpec(memory_space=pl.ANY),
                  pl.BlockSpec((tm, TOP_K), lambda j, b, p: (j * CHAIN_BATCH + b, 0)),
                  row, row, tiles.mod(5),
                  tiles.const((1, D_MODEL)), tiles.const((D_MODEL, D_EXPERT)),
                  tiles.const((D_MODEL, D_EXPERT)), tiles.const((D_EXPERT, D_MODEL))],
        out_specs=[tiles.sample_major(D_MODEL), tiles.prompt_major(D_MODEL)],
        scratch_shapes=[pltpu.VMEM((2, TOP_K * tm, D_MODEL), F32), pltpu.SemaphoreType.DMA((2,))],
    )
    return pl.pallas_call(
        functools.partial(_combine_kernel, tiles=tiles),
        grid_spec=grid_spec,
        out_shape=[jax.ShapeDtypeStruct((N_SAMPLE // SEQ_SAMPLE, SEQ_SAMPLE, D_MODEL), F32),
                   jax.ShapeDtypeStruct((N_PROMPT // SEQ_PROMPT, SEQ_PROMPT, D_MODEL), F32)],
        compiler_params=_params(("arbitrary", "arbitrary")),
        name="combine",
    )(pos_tiles, y_sorted, wsel_tiles, h2, x1, mod3, n_post, sg_bf, su_bf, sd_bf)


def _pad_rows(w, rows):
    return jnp.pad(w, ((0, rows - w.shape[0]), (0, 0)))


def kernel(x_prompt, x_sample, state_fwd, state_bwd, c, c_ctx, ada_w, ada_b, norm_pre_mix, norm_post_mix, norm_pre_ffn, norm_post_ffn, w_in, conv_w, w_out_conv, decay_w0, decay_w2, iclr_a0, iclr_a2, gate_g2, k_k, k_a, r_k, lnx_w, lnx_b, w_out_rwkv, w_o, router_w, router_bias, exp_gate, exp_up, exp_down, sh_gate, sh_up, sh_down):
    cond = jnp.concatenate([c_ctx[None, :], c, jnp.zeros((N_COND - 1 - c.shape[0], D_MODEL), F32)], axis=0)
    mod3 = _ada_table(cond, ada_w[0], ada_b).reshape(N_COND, 1, 6 * D_MODEL)

    h = _prenorm(x_sample, x_prompt, norm_pre_mix, mod3).reshape(N_TOK, D_MODEL)
    w_in_bf = w_in[0].astype(BF16)
    z = _conv_branch(h, w_in_bf, conv_w[0])
    rkv = _matmul(h, w_in_bf, col0=3 * W_BRANCH, n_cols=3 * W_BRANCH, tn=512, out_dtype=F32, name="proj_rkv")
    c0 = 6 * W_BRANCH
    pad_cols = lambda lo: jnp.pad(w_in_bf[:, lo:lo + 96], ((0, 0), (0, LORA_PAD - 96)))
    w_lora = jnp.concatenate([pad_cols(c0), pad_cols(c0 + 96), pad_cols(c0 + 192), pad_cols(c0 + 288),
                              w_in_bf[:, c0 + 384:c0 + 640]], axis=1)
    lora = _matmul(h, w_lora, col0=0, n_cols=LORA_W, tn=LORA_W, out_dtype=F32, name="proj_lora")
    gates = _matmul(h, w_in_bf[:, c0 + 640:], col0=0, n_cols=2 * D_MODEL, tn=512,
                    out_dtype=BF16, act="sigmoid", name="proj_gates")

    wd2 = jnp.stack([_pad_rows(decay_w2[0, 0], LORA_PAD), _pad_rows(decay_w2[0, 1], LORA_PAD)]).astype(BF16)
    wa2 = jnp.stack([_pad_rows(iclr_a2[0, 0], LORA_PAD), _pad_rows(iclr_a2[0, 1], LORA_PAD)]).astype(BF16)
    wa, g = _lora_stage(lora, wd2, wa2, gate_g2[0].astype(BF16), decay_w0[0], iclr_a0[0])

    kk_l = _head_param_to_chain_layout(k_k[0])
    ka_l = _head_param_to_chain_layout(k_a[0])
    rk_l = _head_param_to_chain_layout(r_k[0].reshape(-1))
    lw_l = _head_param_to_chain_layout(lnx_w[0])
    lb_l = _head_param_to_chain_layout(lnx_b[0])

    rkv3 = rkv.reshape(TIME_ROWS, CHAIN_BATCH, 3 * W_BRANCH)
    wa3 = wa.reshape(TIME_ROWS, CHAIN_BATCH, 4 * W_BRANCH)

    def run_scan(row0, seq, s0_f, s0_b):
        s0 = jnp.stack([_state_to_chain_layout(s0_f), _state_to_chain_layout(s0_b)])
        y, bon, s_fin = _scan(rkv3, wa3, row0, seq, s0, kk_l, ka_l, rk_l)
        return _scan_post(y, bon, lw_l, lb_l).reshape(-1, W_BRANCH), s_fin

    zero_state = jnp.zeros((N_PROMPT // SEQ_PROMPT, N_HEADS, HEAD, HEAD), F32)
    yb_s, _ = run_scan(0, SEQ_SAMPLE, state_fwd[:, 0], state_bwd[:, 0])
    yb_p, s_fin = run_scan(SEQ_SAMPLE, SEQ_PROMPT, zero_state, zero_state)

    merged = _merge(z, yb_s, yb_p, g, gates, w_out_conv[0].astype(BF16), w_out_rwkv[0].astype(BF16))
    x1, h2 = _outproj(merged.reshape(TIME_ROWS, CHAIN_BATCH * D_MODEL), w_o[0].astype(BF16),
                      x_sample, x_prompt, mod3, norm_post_mix, norm_pre_ffn)

    h2_flat = h2.reshape(N_TOK, D_MODEL)
    eidx, wsel, rank, counts = _router(h2_flat, router_w[0].T, router_bias[0][:, None])
    counts = counts[:, 0].astype(I32)
    padded = (counts + MOE_ROWS - 1) // MOE_ROWS * MOE_ROWS
    pad_end = jnp.cumsum(padded)
    pad_start = pad_end - padded
    pos = _slots(pad_start, eidx, rank)
    tok_ids = jnp.broadcast_to(jnp.arange(N_TOK, dtype=I32)[None, :], (TOP_K, N_TOK))
    tok_of_slot = jnp.zeros((MOE_BLOCKS * MOE_ROWS,), I32).at[pos.reshape(-1)].set(
        tok_ids.reshape(-1), unique_indices=True, mode="promise_in_bounds")
    blk_start = jnp.arange(MOE_BLOCKS, dtype=I32) * MOE_ROWS
    blk_expert = jnp.minimum(jnp.sum(pad_end[None, :] <= blk_start[:, None], axis=1), N_EXPERTS - 1).astype(I32)
    n_used = (pad_end[-1] // MOE_ROWS).astype(I32).reshape(1)
    blk_first = (blk_start == pad_start[blk_expert]).astype(I32)
    y_sorted = _experts(blk_expert, blk_first, n_used, tok_of_slot, h2_flat, exp_gate[0], exp_up[0], exp_down[0])

    def tile_order(a, perm):
        a = a.reshape(TOP_K, TIME_ROWS // COMBINE_ROWS, COMBINE_ROWS, CHAIN_BATCH)
        return jnp.transpose(a, perm)

    pos_tiles = tile_order(pos, (1, 3, 0, 2)).reshape(-1)
    wsel_tiles = tile_order(wsel, (1, 3, 2, 0)).reshape(N_TOK, TOP_K)
    out_s, out_p = _combine(pos_tiles, y_sorted, wsel_tiles, h2, x1, mod3, norm_post_ffn,
                            sh_gate[0].astype(BF16), sh_up[0].astype(BF16), sh_down[0].astype(BF16))

    new_f = _state_from_chain_layout(s_fin[0])[:, None]
    new_b = _state_from_chain_layout(s_fin[1])[:, None]
    return (out_p, out_s, new_f, new_b)
```

```python
import functools
import math

import jax
import jax.numpy as jnp
from jax import lax
from jax.experimental import pallas as pl
from jax.experimental.pallas import tpu as pltpu

F32 = jnp.float32
BF16 = jnp.bfloat16
I32 = jnp.int32

D_MODEL = 2048
N_SAMPLE = 8 * 1024
SEQ_SAMPLE = 1024
N_PROMPT = 16 * 256
SEQ_PROMPT = 256
N_TOK = N_SAMPLE + N_PROMPT
GRID_W = 64
W_BRANCH = 1024
HEAD = 64
N_HEADS = 16
N_COND = 16
LORA_PAD = 128
LORA_W = 4 * LORA_PAD + 256
N_EXPERTS = 64
TOP_K = 8
N_GROUPS = 8
TOPK_GROUPS = 4
D_EXPERT = 512
ROUTED_SCALE = 2.5
NORM_EPS = 1e-6
GN_EPS = 64e-5
LANES = 128
CHAIN_BATCH = 8
TIME_ROWS = N_TOK // CHAIN_BATCH
CONV_HALO = 16
MOE_ROWS = 512
MOE_BLOCKS = N_TOK * TOP_K // MOE_ROWS + N_EXPERTS
EXPERT_PIECES = 8
COMBINE_ROWS = 128
DMA_UNROLL = 8
VMEM_LIMIT = 56 * 1024 * 1024


def _params(sem, vmem=VMEM_LIMIT):
    return pltpu.CompilerParams(dimension_semantics=sem, vmem_limit_bytes=vmem)


def _split_specs(tm, width):
    n_s = N_SAMPLE // tm
    return (pl.BlockSpec((tm, width), lambda i, *_: (jnp.minimum(i, n_s - 1), 0)),
            pl.BlockSpec((tm, width), lambda i, *_: (jnp.maximum(i - n_s, 0), 0)))


def _on_owner(is_sample, fn):
    pl.when(is_sample)(lambda: fn(True))
    pl.when(jnp.logical_not(is_sample))(lambda: fn(False))


class _ColumnTiles:
    def __init__(self, tt):
        self.tt = tt
        self.n_s = SEQ_SAMPLE // tt
        self.per_group = SEQ_PROMPT // tt
        self.grid = (TIME_ROWS // tt, CHAIN_BATCH)

    def is_sample(self):
        return pl.program_id(0) < self.n_s

    def tokens(self, width):
        return pl.BlockSpec((self.tt, width), lambda j, b, *_: (j, b))

    def sample_major(self, width):
        n_s = self.n_s
        return pl.BlockSpec((None, self.tt, width), lambda j, b, *_: (
            jnp.where(j < n_s, b, CHAIN_BATCH - 1), jnp.minimum(j, n_s - 1), 0))

    def prompt_major(self, width):
        n_s, per = self.n_s, self.per_group
        return pl.BlockSpec((None, self.tt, width), lambda j, b, *_: (
            jnp.where(j < n_s, 0, (j - n_s) // per * CHAIN_BATCH + b),
            jnp.where(j < n_s, 0, (j - n_s) % per), 0))

    def mod(self, chunk):
        n_s = self.n_s
        return pl.BlockSpec((None, 1, D_MODEL), lambda j, b, *_: (jnp.where(j < n_s, 1 + b, 0), 0, chunk))

    def const(self, shape):
        return pl.BlockSpec(shape, lambda j, b, *_: (0,) * len(shape))


def _rms(x):
    return x * lax.rsqrt(jnp.mean(x * x, axis=-1, keepdims=True) + NORM_EPS)


def _silu(x):
    return x * jax.nn.sigmoid(x)


def _ada_kernel(c_ref, w_ref, b_ref, o_ref):
    s = _silu(c_ref[...]).astype(BF16)
    o_ref[...] = jnp.dot(s, w_ref[...].astype(BF16), preferred_element_type=F32) + b_ref[...]


def _ada_table(cond, ada_w, ada_b):
    tn = 1536
    return pl.pallas_call(
        _ada_kernel,
        grid=(6 * D_MODEL // tn,),
        in_specs=[pl.BlockSpec((N_COND, D_MODEL), lambda j: (0, 0)),
                  pl.BlockSpec((D_MODEL, tn), lambda j: (0, j)),
                  pl.BlockSpec((1, tn), lambda j: (0, j))],
        out_specs=pl.BlockSpec((N_COND, tn), lambda j: (0, j)),
        out_shape=jax.ShapeDtypeStruct((N_COND, 6 * D_MODEL), F32),
        compiler_params=_params(("arbitrary",)),
        name="ada_table",
    )(cond, ada_w, ada_b)


def _prenorm_kernel(xs_ref, xp_ref, g_ref, sh_ref, sc_ref, o_ref, *, tiles):
    def run(is_sample):
        y = _rms((xs_ref if is_sample else xp_ref)[...]) * g_ref[...]
        o_ref[...] = (y * (1.0 + sc_ref[...]) + sh_ref[...]).astype(o_ref.dtype)

    _on_owner(tiles.is_sample(), run)


def _prenorm(x_sample, x_prompt, gain, mod3):
    tiles = _ColumnTiles(256)
    return pl.pallas_call(
        functools.partial(_prenorm_kernel, tiles=tiles),
        grid=tiles.grid,
        in_specs=[tiles.sample_major(D_MODEL), tiles.prompt_major(D_MODEL),
                  tiles.const((1, D_MODEL)), tiles.mod(0), tiles.mod(1)],
        out_specs=tiles.tokens(D_MODEL),
        out_shape=jax.ShapeDtypeStruct((TIME_ROWS, CHAIN_BATCH * D_MODEL), BF16),
        compiler_params=_params(("arbitrary", "arbitrary")),
        name="prenorm",
    )(x_sample, x_prompt, gain, mod3, mod3)


def _mm_kernel(a_ref, w_ref, o_ref, *, act):
    acc = jnp.dot(a_ref[...], w_ref[...], preferred_element_type=F32)
    if act == "sigmoid":
        acc = jax.nn.sigmoid(acc)
    o_ref[...] = acc.astype(o_ref.dtype)


def _matmul(a, w, *, col0, n_cols, tn, out_dtype, act=None, tm=1024, name="matmul"):
    m, k = a.shape
    off = col0 // tn
    return pl.pallas_call(
        functools.partial(_mm_kernel, act=act),
        grid=(m // tm, n_cols // tn),
        in_specs=[pl.BlockSpec((tm, k), lambda i, j: (i, 0)),
                  pl.BlockSpec((k, tn), lambda i, j: (0, j + off))],
        out_specs=pl.BlockSpec((tm, tn), lambda i, j: (i, j)),
        out_shape=jax.ShapeDtypeStruct((m, n_cols), out_dtype),
        compiler_params=_params(("arbitrary", "arbitrary")),
        name=name,
    )(a, w)


def _convproj_kernel(hp_ref, h_ref, hn_ref, wb_ref, wc_ref, wx_ref, cw_ref, o_ref, *, tm):
    h = h_ref[...]
    h_ext = jnp.concatenate([hp_ref[...], h, hn_ref[...]], axis=0)
    cb = jnp.dot(h, wb_ref[...], preferred_element_type=F32)
    u_ext = (jnp.dot(h_ext, wc_ref[...], preferred_element_type=F32)
             * jnp.dot(h_ext, wx_ref[...], preferred_element_type=F32))
    u = u_ext[CONV_HALO:CONV_HALO + tm]
    u_prev = u_ext[CONV_HALO - CHAIN_BATCH:CONV_HALO - CHAIN_BATCH + tm]
    u_next = u_ext[CONV_HALO + CHAIN_BATCH:CONV_HALO + CHAIN_BATCH + tm]
    seg = jnp.where(pl.program_id(0) < N_SAMPLE // tm, GRID_W, SEQ_PROMPT)
    row = lax.broadcasted_iota(I32, u.shape, 0)
    pos = (pl.program_id(0) * (tm // CHAIN_BATCH) + row // CHAIN_BATCH) & (seg - 1)
    u_prev = jnp.where(pos == 0, 0.0, u_prev)
    u_next = jnp.where(pos == seg - 1, 0.0, u_next)
    cw = cw_ref[...]
    conv = cw[0:1, :] * u_prev + cw[1:2, :] * u + cw[2:3, :] * u_next
    o_ref[...] = (cb * conv).astype(o_ref.dtype)


def _conv_branch(h, w_in_bf, conv_w):
    tm, tn = 1024, 256
    nb = W_BRANCH // tn
    per = tm // CONV_HALO
    return pl.pallas_call(
        functools.partial(_convproj_kernel, tm=tm),
        grid=(N_TOK // tm, nb),
        in_specs=[pl.BlockSpec((CONV_HALO, D_MODEL), lambda i, j: (jnp.maximum(i * per - 1, 0), 0)),
                  pl.BlockSpec((tm, D_MODEL), lambda i, j: (i, 0)),
                  pl.BlockSpec((CONV_HALO, D_MODEL),
                               lambda i, j: (jnp.minimum((i + 1) * per, N_TOK // CONV_HALO - 1), 0)),
                  pl.BlockSpec((D_MODEL, tn), lambda i, j: (0, j)),
                  pl.BlockSpec((D_MODEL, tn), lambda i, j: (0, j + nb)),
                  pl.BlockSpec((D_MODEL, tn), lambda i, j: (0, j + 2 * nb)),
                  pl.BlockSpec((3, tn), lambda i, j: (0, j))],
        out_specs=pl.BlockSpec((tm, tn), lambda i, j: (i, j)),
        out_shape=jax.ShapeDtypeStruct((N_TOK, W_BRANCH), BF16),
        compiler_params=_params(("arbitrary", "arbitrary")),
        name="conv_branch",
    )(h, h, h, w_in_bf, w_in_bf, w_in_bf, conv_w)


def _lora_kernel(x_ref, wd_ref, wa_ref, wg_ref, w0_ref, a0_ref, wa_out_ref, g_ref):
    def group(i):
        return x_ref[:, i * LORA_PAD:(i + 1) * LORA_PAD]

    def decay(wl):
        return jnp.exp(-jax.nn.sigmoid(wl) * math.exp(-0.5))

    for d in range(2):
        dl = jnp.dot(jnp.tanh(group(d)).astype(BF16), wd_ref[d], preferred_element_type=F32)
        wa_out_ref[:, d * W_BRANCH:(d + 1) * W_BRANCH] = decay(w0_ref[d:d + 1, :] + dl)
        al = jnp.dot(group(2 + d).astype(BF16), wa_ref[d], preferred_element_type=F32)
        wa_out_ref[:, (2 + d) * W_BRANCH:(3 + d) * W_BRANCH] = jax.nn.sigmoid(a0_ref[d:d + 1, :] + al)
    gl = jax.nn.sigmoid(x_ref[:, 4 * LORA_PAD:]).astype(BF16)
    g_ref[...] = jnp.dot(gl, wg_ref[...], preferred_element_type=F32)


def _lora_stage(lora, wd2, wa2, wg2, w0, a0):
    tm = 512
    return pl.pallas_call(
        _lora_kernel,
        grid=(N_TOK // tm,),
        in_specs=[pl.BlockSpec((tm, LORA_W), lambda i: (i, 0)),
                  pl.BlockSpec((2, LORA_PAD, W_BRANCH), lambda i: (0, 0, 0)),
                  pl.BlockSpec((2, LORA_PAD, W_BRANCH), lambda i: (0, 0, 0)),
                  pl.BlockSpec((256, W_BRANCH), lambda i: (0, 0)),
                  pl.BlockSpec((2, W_BRANCH), lambda i: (0, 0)),
                  pl.BlockSpec((2, W_BRANCH), lambda i: (0, 0))],
        out_specs=[pl.BlockSpec((tm, 4 * W_BRANCH), lambda i: (i, 0)),
                   pl.BlockSpec((tm, W_BRANCH), lambda i: (i, 0))],
        out_shape=[jax.ShapeDtypeStruct((N_TOK, 4 * W_BRANCH), F32),
                   jax.ShapeDtypeStruct((N_TOK, W_BRANCH), F32)],
        compiler_params=_params(("arbitrary",)),
        name="lora_stage",
    )(lora, wd2, wa2, wg2, w0, a0)


def _chain_tiles(x_ref, t0):
    both = (x_ref[t0], x_ref[t0 + 1])
    rows = [both[tl][:, p * LANES:(p + 1) * LANES] for tl in range(2) for p in range(CHAIN_BATCH)]
    sq = jnp.concatenate(rows, axis=0).T
    top, bot = sq[:HEAD], sq[HEAD:]
    low = lax.broadcasted_iota(I32, (HEAD, LANES), 1) < HEAD
    return (jnp.where(low, top, pltpu.roll(bot, HEAD, 1)),
            jnp.where(low, pltpu.roll(top, HEAD, 1), bot))


def _scan_kernel(r_ref, k_ref, v_ref, w_ref, a_ref, s0_ref, kk_ref, ka_ref, rk_ref,
                 y_ref, bon_ref, sf_ref, s_ref, p_ref, ops_a, ops_b, *, tc):
    d = pl.program_id(0)
    c = pl.program_id(2)
    n_pairs = tc // 2

    @pl.when(c == 0)
    def _():
        s_ref[...] = s0_ref[...]

    p_ref[...] = jnp.ones_like(p_ref)

    def first_row(pair_idx):
        return 2 * jnp.where(d == 0, pair_idx, n_pairs - 1 - pair_idx)

    def prepare(pair_idx, ops_ref):
        t0 = first_row(pair_idx)
        r2, k2, v2 = _chain_tiles(r_ref, t0), _chain_tiles(k_ref, t0), _chain_tiles(v_ref, t0)
        w2, a2 = _chain_tiles(w_ref, t0), _chain_tiles(a_ref, t0)
        fwd = d == 0
        p = p_ref[...]
        for s in range(2):
            r, k, v, w, a = (jnp.where(fwd, x[s], x[1 - s]) for x in (r2, k2, v2, w2, a2))
            kk = k * kk_ref[...]
            kk = kk * lax.rsqrt(jnp.sum(kk * kk, axis=0, keepdims=True) + 1e-12)
            kd = k * (1.0 + (a - 1.0) * ka_ref[...])
            bon_ref[t0 + jnp.where(fwd, s, 1 - s)] = jnp.sum(r * kd * rk_ref[...], axis=0, keepdims=True) * v
            p_new = p * w
            inv = 1.0 / p_new
            for q, val in enumerate((kk * p, kk * a * inv, kd * inv, r * p_new, v)):
                ops_ref[s, q] = val
            p = p_new
        p_ref[...] = p

    def recur(pair_idx, ops_ref):
        t0 = first_row(pair_idx)
        for s in range(2):
            vt = ops_ref[s, 4]
            sa = jnp.zeros((HEAD, LANES), F32)
            for j in range(HEAD):
                sa = sa + s_ref[j] * ops_ref[s, 0, pl.ds(j, 1), :]
            y = jnp.zeros((HEAD, LANES), F32)
            for j in range(HEAD):
                un = s_ref[j] - sa * ops_ref[s, 1, pl.ds(j, 1), :] + vt * ops_ref[s, 2, pl.ds(j, 1), :]
                s_ref[j] = un
                y = y + un * ops_ref[s, 3, pl.ds(j, 1), :]
            y_ref[t0 + jnp.where(d == 0, s, 1 - s)] = y

    prepare(0, ops_a)

    def two_pairs(i, carry):
        prepare(2 * i + 1, ops_b)
        recur(2 * i, ops_a)
        prepare(2 * i + 2, ops_a)
        recur(2 * i + 1, ops_b)
        return carry

    lax.fori_loop(0, n_pairs // 2 - 1, two_pairs, 0)
    prepare(n_pairs - 1, ops_b)
    recur(n_pairs - 2, ops_a)
    recur(n_pairs - 1, ops_b)

    for j in range(HEAD):
        s_ref[j] = s_ref[j] * p_ref[pl.ds(j, 1), :]

    @pl.when(c == pl.num_programs(2) - 1)
    def _():
        sf_ref[...] = s_ref[...]


def _scan(rkv3, wa3, row0, seq, s0, kk_l, ka_l, rk_l, *, tc=64):
    g_n = s0.shape[1]
    nc = seq // tc

    def tchunk(d, c):
        return jnp.where(d == 0, c, nc - 1 - c)

    def tok(col):
        return pl.BlockSpec((tc, CHAIN_BATCH, W_BRANCH),
                            lambda d, g, c: (row0 // tc + g * nc + tchunk(d, c), 0, col))

    def tok_dir(col):
        return pl.BlockSpec((tc, CHAIN_BATCH, W_BRANCH),
                            lambda d, g, c: (row0 // tc + g * nc + tchunk(d, c), 0, col + d))

    chain = pl.BlockSpec((None, None, tc, HEAD, LANES), lambda d, g, c: (d, g, tchunk(d, c), 0, 0))
    state = pl.BlockSpec((None, None, HEAD, HEAD, LANES), lambda d, g, c: (d, g, 0, 0, 0))
    par = pl.BlockSpec((HEAD, LANES), lambda d, g, c: (0, 0))
    seq_shape = jax.ShapeDtypeStruct((2, g_n, seq, HEAD, LANES), F32)
    ops = pltpu.VMEM((2, 5, HEAD, LANES), F32)
    return pl.pallas_call(
        functools.partial(_scan_kernel, tc=tc),
        grid=(2, g_n, nc),
        in_specs=[tok(0), tok(1), tok(2), tok_dir(0), tok_dir(2), state, par, par, par],
        out_specs=[chain, chain, state],
        out_shape=[seq_shape, seq_shape, jax.ShapeDtypeStruct((2, g_n, HEAD, HEAD, LANES), F32)],
        scratch_shapes=[pltpu.VMEM((HEAD, HEAD, LANES), F32), pltpu.VMEM((HEAD, LANES), F32), ops, ops],
        compiler_params=_params(("arbitrary", "arbitrary", "arbitrary")),
        name="wkv7_scan",
    )(rkv3, rkv3, rkv3, wa3, wa3, s0, kk_l, ka_l, rk_l)


def _scan_post_kernel(y_ref, bon_ref, lw_ref, lb_ref, o_ref, *, tc):
    low = lax.broadcasted_iota(I32, (HEAD, LANES), 1) < HEAD

    def pair(i, carry):
        t0 = 2 * i
        tiles = []
        for tl in range(2):
            ys = y_ref[0, t0 + tl] + y_ref[1, t0 + tl]
            dev = ys - jnp.mean(ys, axis=0, keepdims=True)
            var = jnp.mean(dev * dev, axis=0, keepdims=True)
            yn = dev * lax.rsqrt(var + GN_EPS) * lw_ref[...] + lb_ref[...]
            tiles.append(yn + bon_ref[0, t0 + tl] + bon_ref[1, t0 + tl])
        top = jnp.where(low, tiles[0], pltpu.roll(tiles[1], HEAD, 1))
        bot = jnp.where(low, pltpu.roll(tiles[0], HEAD, 1), tiles[1])
        sq = jnp.concatenate([top, bot], axis=0).T
        for tl in range(2):
            for p in range(CHAIN_BATCH):
                r0 = tl * HEAD + p * CHAIN_BATCH
                o_ref[t0 + tl, :, pl.ds(p * LANES, LANES)] = sq[r0:r0 + CHAIN_BATCH, :]
        return carry

    lax.fori_loop(0, tc // 2, pair, 0)


def _scan_post(y, bon, lw_l, lb_l):
    _, g_n, seq = y.shape[:3]
    tc = 64
    nc = seq // tc
    both = pl.BlockSpec((2, None, tc, HEAD, LANES), lambda g, c: (0, g, c, 0, 0))
    par = pl.BlockSpec((HEAD, LANES), lambda g, c: (0, 0))
    return pl.pallas_call(
        functools.partial(_scan_post_kernel, tc=tc),
        grid=(g_n, nc),
        in_specs=[both, both, par, par],
        out_specs=pl.BlockSpec((tc, CHAIN_BATCH, W_BRANCH), lambda g, c: (g * nc + c, 0, 0)),
        out_shape=jax.ShapeDtypeStruct((g_n * seq, CHAIN_BATCH, W_BRANCH), F32),
        compiler_params=_params(("arbitrary", "arbitrary")),
        name="scan_post",
    )(y, bon, lw_l, lb_l)


def _state_to_chain_layout(s):
    g_n = s.shape[0] // CHAIN_BATCH
    s = s.reshape(g_n, CHAIN_BATCH, N_HEADS // 2, 2, HEAD, HEAD)
    return jnp.transpose(s, (0, 5, 4, 3, 2, 1)).reshape(g_n, HEAD, HEAD, LANES)


def _state_from_chain_layout(s):
    g_n = s.shape[0]
    s = s.reshape(g_n, HEAD, HEAD, 2, N_HEADS // 2, CHAIN_BATCH)
    return jnp.transpose(s, (0, 5, 4, 3, 2, 1)).reshape(g_n * CHAIN_BATCH, N_HEADS, HEAD, HEAD)


def _head_param_to_chain_layout(p):
    p = jnp.transpose(p.reshape(N_HEADS // 2, 2, HEAD), (1, 0, 2))
    p = jnp.broadcast_to(p[:, :, None, :], (2, N_HEADS // 2, CHAIN_BATCH, HEAD))
    return p.reshape(LANES, HEAD).T


def _merge_kernel(z_ref, ybs_ref, ybp_ref, g_ref, ga_ref, gb_ref, wc_ref, wr_ref, o_ref, *, tm):
    def run(is_sample):
        y_a = jnp.dot(z_ref[...], wc_ref[...], preferred_element_type=F32)
        yb = ((ybs_ref if is_sample else ybp_ref)[...] * g_ref[...]).astype(BF16)
        y_b = jnp.dot(yb, wr_ref[...], preferred_element_type=F32)
        o_ref[...] = (ga_ref[...].astype(F32) * y_a + gb_ref[...].astype(F32) * y_b).astype(o_ref.dtype)

    _on_owner(pl.program_id(0) < N_SAMPLE // tm, run)


def _merge(z, yb_s, yb_p, g, gates, w_conv_bf, w_rwkv_bf):
    tm = 512
    row = lambda w: pl.BlockSpec((tm, w), lambda i: (i, 0))
    return pl.pallas_call(
        functools.partial(_merge_kernel, tm=tm),
        grid=(N_TOK // tm,),
        in_specs=[row(W_BRANCH), *_split_specs(tm, W_BRANCH), row(W_BRANCH),
                  pl.BlockSpec((tm, D_MODEL), lambda i: (i, 0)),
                  pl.BlockSpec((tm, D_MODEL), lambda i: (i, 1)),
                  pl.BlockSpec((W_BRANCH, D_MODEL), lambda i: (0, 0)),
                  pl.BlockSpec((W_BRANCH, D_MODEL), lambda i: (0, 0))],
        out_specs=row(D_MODEL),
        out_shape=jax.ShapeDtypeStruct((N_TOK, D_MODEL), BF16),
        compiler_params=_params(("arbitrary",)),
        name="merge",
    )(z, yb_s, yb_p, g, gates, gates, w_conv_bf, w_rwkv_bf)


def _outproj_kernel(m_ref, wo_ref, xs_ref, xp_ref, g1_ref, sh2_ref, sc2_ref, npost_ref, npre_ref,
                    x1_ref, h2_ref, *, tiles):
    def run(is_sample):
        out = jnp.dot(m_ref[...], wo_ref[...], preferred_element_type=F32)
        x1 = (xs_ref if is_sample else xp_ref)[...] + g1_ref[...] * (_rms(out) * npost_ref[...])
        x1_ref[...] = x1
        h2_ref[...] = (_rms(x1) * npre_ref[...]) * (1.0 + sc2_ref[...]) + sh2_ref[...]

    _on_owner(tiles.is_sample(), run)


def _outproj(merged, w_o_bf, x_sample, x_prompt, mod3, n_post, n_pre):
    tiles = _ColumnTiles(256)
    row = tiles.tokens(D_MODEL)
    vec = tiles.const((1, D_MODEL))
    out = jax.ShapeDtypeStruct((TIME_ROWS, CHAIN_BATCH * D_MODEL), F32)
    return pl.pallas_call(
        functools.partial(_outproj_kernel, tiles=tiles),
        grid=tiles.grid,
        in_specs=[row, tiles.const((D_MODEL, D_MODEL)),
                  tiles.sample_major(D_MODEL), tiles.prompt_major(D_MODEL),
                  tiles.mod(2), tiles.mod(3), tiles.mod(4), vec, vec],
        out_specs=[row, row],
        out_shape=[out, out],
        compiler_params=_params(("arbitrary", "arbitrary")),
        name="outproj",
    )(merged, w_o_bf, x_sample, x_prompt, mod3, mod3, mod3, n_post, n_pre)


def _first_index_of_max(x, axis, n):
    m = jnp.max(x, axis=axis, keepdims=True)
    idx = lax.broadcasted_iota(I32, x.shape, axis).astype(F32)
    first = jnp.min(jnp.where(x == m, idx, float(n)), axis=axis, keepdims=True)
    return m, idx, first


def _router_kernel(h_ref, rw_ref, rb_ref, eidx_ref, wsel_ref, rank_ref, cnt_ref, base_ref, *, tm):
    @pl.when(pl.program_id(0) == 0)
    def _():
        base_ref[...] = jnp.zeros_like(base_ref)

    logits = lax.dot_general(rw_ref[...], h_ref[...], (((1,), (1,)), ((), ())),
                             precision=lax.Precision.HIGHEST, preferred_element_type=F32)
    scores = jax.nn.sigmoid(logits)
    biased = scores + rb_ref[...]
    neg = -jnp.inf

    per_group = N_EXPERTS // N_GROUPS
    grp = biased.reshape(N_GROUPS, per_group, tm)
    m1, idx, first = _first_index_of_max(grp, 1, per_group)
    m2 = jnp.max(jnp.where(idx == first, neg, grp), axis=1, keepdims=True)
    gscore = (m1 + m2).reshape(N_GROUPS, tm)

    gsel = jnp.zeros((N_GROUPS, tm), F32)
    for _ in range(TOPK_GROUPS):
        _, gidx, gfirst = _first_index_of_max(gscore, 0, N_GROUPS)
        hit = gidx == gfirst
        gsel = jnp.where(hit, 1.0, gsel)
        gscore = jnp.where(hit, neg, gscore)
    emask = jnp.broadcast_to(gsel[:, None, :], (N_GROUPS, per_group, tm)).reshape(N_EXPERTS, tm)

    cand = jnp.where(emask > 0.5, biased, neg)
    mem = jnp.zeros((N_EXPERTS, tm), F32)
    picks = []
    for _ in range(TOP_K):
        _, eidx, efirst = _first_index_of_max(cand, 0, N_EXPERTS)
        hit = eidx == efirst
        mem = jnp.where(hit, 1.0, mem)
        cand = jnp.where(hit, neg, cand)
        picks.append((efirst, hit))

    s_i = lax.broadcasted_iota(I32, (tm, tm), 0)
    t_i = lax.broadcasted_iota(I32, (tm, tm), 1)
    upper = (s_i <= t_i).astype(BF16)
    incl = jnp.dot(mem.astype(BF16), upper, preferred_element_type=F32)
    rank = base_ref[...] + incl - mem
    base_ref[...] = base_ref[...] + jnp.sum(mem, axis=1, keepdims=True)
    cnt_ref[...] = base_ref[...]

    wsum = jnp.zeros((1, tm), F32)
    wrows = []
    for j, (efirst, hit) in enumerate(picks):
        wj = jnp.sum(jnp.where(hit, scores, 0.0), axis=0, keepdims=True)
        wrows.append(wj)
        wsum = wsum + wj
        eidx_ref[pl.ds(j, 1), :] = efirst.astype(I32)
        rank_ref[pl.ds(j, 1), :] = jnp.sum(jnp.where(hit, rank, 0.0), axis=0, keepdims=True).astype(I32)
    for j, wj in enumerate(wrows):
        wsel_ref[pl.ds(j, 1), :] = wj / wsum * ROUTED_SCALE


def _router(h2, rw_t, rb_col):
    tm = 256
    tok = lambda dt: jax.ShapeDtypeStruct((TOP_K, N_TOK), dt)
    tspec = pl.BlockSpec((TOP_K, tm), lambda i: (0, i))
    return pl.pallas_call(
        functools.partial(_router_kernel, tm=tm),
        grid=(N_TOK // tm,),
        in_specs=[pl.BlockSpec((tm, D_MODEL), lambda i: (i, 0)),
                  pl.BlockSpec((N_EXPERTS, D_MODEL), lambda i: (0, 0)),
                  pl.BlockSpec((N_EXPERTS, 1), lambda i: (0, 0))],
        out_specs=[tspec, tspec, tspec, pl.BlockSpec((N_EXPERTS, 1), lambda i: (0, 0))],
        out_shape=[tok(I32), tok(F32), tok(I32), jax.ShapeDtypeStruct((N_EXPERTS, 1), F32)],
        scratch_shapes=[pltpu.VMEM((N_EXPERTS, 1), F32)],
        compiler_params=_params(("arbitrary",)),
        name="router",
    )(h2, rw_t, rb_col)


def _slot_kernel(start_ref, eidx_ref, rank_ref, pos_ref):
    e = eidx_ref[...]
    pos = rank_ref[...]
    for j in range(N_EXPERTS):
        pos = pos + jnp.where(e == j, start_ref[j], 0)
    pos_ref[...] = pos


def _slots(pad_start, eidx, rank):
    full = pl.BlockSpec((TOP_K, N_TOK), lambda i, s: (0, 0))
    return pl.pallas_call(
        _slot_kernel,
        grid_spec=pltpu.PrefetchScalarGridSpec(num_scalar_prefetch=1, grid=(1,),
                                               in_specs=[full, full], out_specs=full),
        out_shape=jax.ShapeDtypeStruct((TOP_K, N_TOK), I32),
        compiler_params=_params(("arbitrary",)),
        name="slots",
    )(pad_start, eidx, rank)


def _gather_rows(idx_ref, base, n_rows, src_hbm, dst, sem, first_row=None):
    def issue(i, carry):
        pltpu.make_async_copy(src_hbm.at[pl.ds(idx_ref[base + i], 1), :], dst.at[pl.ds(i, 1), :], sem).start()
        return carry

    if first_row is not None:
        for i in range(first_row, first_row + n_rows):
            issue(i, 0)
    else:
        lax.fori_loop(0, n_rows, issue, 0, unroll=DMA_UNROLL)


def _wait_rows(n_rows, src_hbm, dst, sem):
    pltpu.make_async_copy(src_hbm.at[pl.ds(0, n_rows), :], dst, sem).wait()


def _expert_kernel(be_ref, first_ref, nused_ref, tok_ref,
                   h_hbm, wg_ref, wu_ref, wd_ref, o_ref, xbuf, wgb, wub, wdb, sem):
    b = pl.program_id(0)
    n_used = nused_ref[0]
    slot = b % 2

    @pl.when(b == 0)
    def _():
        _gather_rows(tok_ref, 0, MOE_ROWS, h_hbm, xbuf.at[0], sem.at[0])

    @pl.when(b < n_used)
    def _():
        @pl.when(first_ref[b] == 1)
        def _():
            wgb[...] = wg_ref[...].astype(BF16)
            wub[...] = wu_ref[...].astype(BF16)
            wdb[...] = wd_ref[...].astype(BF16)

        _wait_rows(MOE_ROWS, h_hbm, xbuf.at[slot], sem.at[slot])
        x = xbuf[slot].astype(BF16)

        rows_per_group = MOE_ROWS // EXPERT_PIECES

        def lookahead(piece):
            _gather_rows(tok_ref, (b + 1) * MOE_ROWS, rows_per_group, h_hbm, xbuf.at[1 - slot],
                         sem.at[1 - slot], first_row=piece * rows_per_group)

        half = D_EXPERT // 2
        hid = []
        for n in range(2):
            cols = slice(n * half, (n + 1) * half)
            lookahead(2 * n)
            gate = jnp.dot(x, wgb[:, cols], preferred_element_type=F32)
            lookahead(2 * n + 1)
            up = jnp.dot(x, wub[:, cols], preferred_element_type=F32)
            hid.append((_silu(gate) * up).astype(BF16))
        hid = jnp.concatenate(hid, axis=1)
        n_out = EXPERT_PIECES - 4
        width = D_MODEL // n_out
        for n in range(n_out):
            cols = slice(n * width, (n + 1) * width)
            lookahead(4 + n)
            o_ref[:, cols] = jnp.dot(hid, wdb[:, cols], preferred_element_type=F32)

    @pl.when(b >= n_used)
    def _():
        @pl.when(b == n_used)
        def _():
            _wait_rows(MOE_ROWS, h_hbm, xbuf.at[slot], sem.at[slot])

        o_ref[...] = jnp.zeros_like(o_ref)


def _experts(blk_expert, blk_first, n_used, tok_of_slot, h2, exp_gate, exp_up, exp_down):
    wspec_in = pl.BlockSpec((None, D_MODEL, D_EXPERT), lambda b, be, fi, nu, tk: (be[b], 0, 0))
    wspec_out = pl.BlockSpec((None, D_EXPERT, D_MODEL), lambda b, be, fi, nu, tk: (be[b], 0, 0))
    grid_spec = pltpu.PrefetchScalarGridSpec(
        num_scalar_prefetch=4,
        grid=(MOE_BLOCKS,),
        in_specs=[pl.BlockSpec(memory_space=pl.ANY), wspec_in, wspec_in, wspec_out],
        out_specs=pl.BlockSpec((MOE_ROWS, D_MODEL), lambda b, be, fi, nu, tk: (b, 0)),
        scratch_shapes=[pltpu.VMEM((2, MOE_ROWS, D_MODEL), F32),
                        pltpu.VMEM((D_MODEL, D_EXPERT), BF16),
                        pltpu.VMEM((D_MODEL, D_EXPERT), BF16),
                        pltpu.VMEM((D_EXPERT, D_MODEL), BF16),
                        pltpu.SemaphoreType.DMA((2,))],
    )
    return pl.pallas_call(
        _expert_kernel,
        grid_spec=grid_spec,
        out_shape=jax.ShapeDtypeStruct((MOE_BLOCKS * MOE_ROWS, D_MODEL), F32),
        compiler_params=_params(("arbitrary",)),
        name="experts",
    )(blk_expert, blk_first, n_used, tok_of_slot, h2, exp_gate, exp_up, exp_down)


def _combine_kernel(pos_ref, y_hbm, wt_ref, h_ref, x1_ref, g2_ref, npost_ref,
                    sg_ref, su_ref, sd_ref, os_ref, op_ref, buf, sem, *, tiles):
    i = pl.program_id(0) * pl.num_programs(1) + pl.program_id(1)
    n_steps = pl.num_programs(0) * pl.num_programs(1)
    slot = i % 2
    tm = tiles.tt
    rows = TOP_K * tm

    @pl.when(i == 0)
    def _():
        _gather_rows(pos_ref, 0, rows, y_hbm, buf.at[0], sem.at[0])

    @pl.when(i + 1 < n_steps)
    def _():
        _gather_rows(pos_ref, (i + 1) * rows, rows, y_hbm, buf.at[1 - slot], sem.at[1 - slot])

    hb = h_ref[...].astype(BF16)
    gate = jnp.dot(hb, sg_ref[...], preferred_element_type=F32)
    up = jnp.dot(hb, su_ref[...], preferred_element_type=F32)
    hid = (_silu(gate) * up).astype(BF16)
    f = jnp.dot(hid, sd_ref[...], preferred_element_type=F32)

    _wait_rows(rows, y_hbm, buf.at[slot], sem.at[slot])
    wt = wt_ref[...]
    for j in range(TOP_K):
        f = f + buf[slot, pl.ds(j * tm, tm), :] * wt[:, j:j + 1]
    out = x1_ref[...] + g2_ref[...] * (_rms(f) * npost_ref[...])

    def store(is_sample):
        (os_ref if is_sample else op_ref)[...] = out

    _on_owner(tiles.is_sample(), store)


def _combine(pos_tiles, y_sorted, wsel_tiles, h2, x1, mod3, n_post, sg_bf, su_bf, sd_bf):
    tiles = _ColumnTiles(COMBINE_ROWS)
    tm = tiles.tt
    row = tiles.tokens(D_MODEL)
    grid_spec = pltpu.PrefetchScalarGridSpec(
        num_scalar_prefetch=1,
        grid=tiles.grid,
        in_specs=[pl.BlockSpec(memory_space=pl.ANY),
                  pl.BlockSpec((tm, TOP_K), lambda j, b, p: (j * CHAIN_BATCH + b, 0)),
                  row, row, tiles.mod(5),
                  tiles.const((1, D_MODEL)), tiles.const((D_MODEL, D_EXPERT)),
                  tiles.const((D_MODEL, D_EXPERT)), tiles.const((D_EXPERT, D_MODEL))],
        out_specs=[tiles.sample_major(D_MODEL), tiles.prompt_major(D_MODEL)],
        scratch_shapes=[pltpu.VMEM((2, TOP_K * tm, D_MODEL), F32), pltpu.SemaphoreType.DMA((2,))],
    )
    return pl.pallas_call(
        functools.partial(_combine_kernel, tiles=tiles),
        grid_spec=grid_spec,
        out_shape=[jax.ShapeDtypeStruct((N_SAMPLE // SEQ_SAMPLE, SEQ_SAMPLE, D_MODEL), F32),
                   jax.ShapeDtypeStruct((N_PROMPT // SEQ_PROMPT, SEQ_PROMPT, D_MODEL), F32)],
        compiler_params=_params(("arbitrary", "arbitrary")),
        name="combine",
    )(pos_tiles, y_sorted, wsel_tiles, h2, x1, mod3, n_post, sg_bf, su_bf, sd_bf)


def _pad_rows(w, rows):
    return jnp.pad(w, ((0, rows - w.shape[0]), (0, 0)))


def kernel(x_prompt, x_sample, state_fwd, state_bwd, c, c_ctx, ada_w, ada_b, norm_pre_mix, norm_post_mix, norm_pre_ffn, norm_post_ffn, w_in, conv_w, w_out_conv, decay_w0, decay_w2, iclr_a0, iclr_a2, gate_g2, k_k, k_a, r_k, lnx_w, lnx_b, w_out_rwkv, w_o, router_w, router_bias, exp_gate, exp_up, exp_down, sh_gate, sh_up, sh_down):
    cond = jnp.concatenate([c_ctx[None, :], c, jnp.zeros((N_COND - 1 - c.shape[0], D_MODEL), F32)], axis=0)
    mod3 = _ada_table(cond, ada_w[0], ada_b).reshape(N_COND, 1, 6 * D_MODEL)

    h = _prenorm(x_sample, x_prompt, norm_pre_mix, mod3).reshape(N_TOK, D_MODEL)
    w_in_bf = w_in[0].astype(BF16)
    z = _conv_branch(h, w_in_bf, conv_w[0])
    rkv = _matmul(h, w_in_bf, col0=3 * W_BRANCH, n_cols=3 * W_BRANCH, tn=1024, out_dtype=F32, name="proj_rkv")
    c0 = 6 * W_BRANCH
    pad_cols = lambda lo: jnp.pad(w_in_bf[:, lo:lo + 96], ((0, 0), (0, LORA_PAD - 96)))
    w_lora = jnp.concatenate([pad_cols(c0), pad_cols(c0 + 96), pad_cols(c0 + 192), pad_cols(c0 + 288),
                              w_in_bf[:, c0 + 384:c0 + 640]], axis=1)
    lora = _matmul(h, w_lora, col0=0, n_cols=LORA_W, tn=LORA_W, out_dtype=F32, name="proj_lora")
    gates = _matmul(h, w_in_bf[:, c0 + 640:], col0=0, n_cols=2 * D_MODEL, tn=1024,
                    out_dtype=BF16, act="sigmoid", name="proj_gates")

    wd2 = jnp.stack([_pad_rows(decay_w2[0, 0], LORA_PAD), _pad_rows(decay_w2[0, 1], LORA_PAD)]).astype(BF16)
    wa2 = jnp.stack([_pad_rows(iclr_a2[0, 0], LORA_PAD), _pad_rows(iclr_a2[0, 1], LORA_PAD)]).astype(BF16)
    wa, g = _lora_stage(lora, wd2, wa2, gate_g2[0].astype(BF16), decay_w0[0], iclr_a0[0])

    kk_l = _head_param_to_chain_layout(k_k[0])
    ka_l = _head_param_to_chain_layout(k_a[0])
    rk_l = _head_param_to_chain_layout(r_k[0].reshape(-1))
    lw_l = _head_param_to_chain_layout(lnx_w[0])
    lb_l = _head_param_to_chain_layout(lnx_b[0])

    rkv3 = rkv.reshape(TIME_ROWS, CHAIN_BATCH, 3 * W_BRANCH)
    wa3 = wa.reshape(TIME_ROWS, CHAIN_BATCH, 4 * W_BRANCH)

    def run_scan(row0, seq, s0_f, s0_b):
        s0 = jnp.stack([_state_to_chain_layout(s0_f), _state_to_chain_layout(s0_b)])
        y, bon, s_fin = _scan(rkv3, wa3, row0, seq, s0, kk_l, ka_l, rk_l)
        return _scan_post(y, bon, lw_l, lb_l).reshape(-1, W_BRANCH), s_fin

    zero_state = jnp.zeros((N_PROMPT // SEQ_PROMPT, N_HEADS, HEAD, HEAD), F32)
    yb_s, _ = run_scan(0, SEQ_SAMPLE, state_fwd[:, 0], state_bwd[:, 0])
    yb_p, s_fin = run_scan(SEQ_SAMPLE, SEQ_PROMPT, zero_state, zero_state)

    merged = _merge(z, yb_s, yb_p, g, gates, w_out_conv[0].astype(BF16), w_out_rwkv[0].astype(BF16))
    x1, h2 = _outproj(merged.reshape(TIME_ROWS, CHAIN_BATCH * D_MODEL), w_o[0].astype(BF16),
                      x_sample, x_prompt, mod3, norm_post_mix, norm_pre_ffn)

    h2_flat = h2.reshape(N_TOK, D_MODEL)
    eidx, wsel, rank, counts = _router(h2_flat, router_w[0].T, router_bias[0][:, None])
    counts = counts[:, 0].astype(I32)
    padded = (counts + MOE_ROWS - 1) // MOE_ROWS * MOE_ROWS
    pad_end = jnp.cumsum(padded)
    pad_start = pad_end - padded
    pos = _slots(pad_start, eidx, rank)
    tok_ids = jnp.broadcast_to(jnp.arange(N_TOK, dtype=I32)[None, :], (TOP_K, N_TOK))
    tok_of_slot = jnp.zeros((MOE_BLOCKS * MOE_ROWS,), I32).at[pos.reshape(-1)].set(
        tok_ids.reshape(-1), unique_indices=True, mode="promise_in_bounds")
    blk_start = jnp.arange(MOE_BLOCKS, dtype=I32) * MOE_ROWS
    blk_expert = jnp.minimum(jnp.sum(pad_end[None, :] <= blk_start[:, None], axis=1), N_EXPERTS - 1).astype(I32)
    n_used = (pad_end[-1] // MOE_ROWS).astype(I32).reshape(1)
    blk_first = (blk_start == pad_start[blk_expert]).astype(I32)
    y_sorted = _experts(blk_expert, blk_first, n_used, tok_of_slot, h2_flat, exp_gate[0], exp_up[0], exp_down[0])

    def tile_order(a, perm):
        a = a.reshape(TOP_K, TIME_ROWS // COMBINE_ROWS, COMBINE_ROWS, CHAIN_BATCH)
        return jnp.transpose(a, perm)

    pos_tiles = tile_order(pos, (1, 3, 0, 2)).reshape(-1)
    wsel_tiles = tile_order(wsel, (1, 3, 2, 0)).reshape(N_TOK, TOP_K)
    out_s, out_p = _combine(pos_tiles, y_sorted, wsel_tiles, h2, x1, mod3, norm_post_ffn,
                            sh_gate[0].astype(BF16), sh_up[0].astype(BF16), sh_down[0].astype(BF16))

    new_f = _state_from_chain_layout(s_fin[0])[:, None]
    new_b = _state_from_chain_layout(s_fin[1])[:, None]
    return (out_p, out_s, new_f, new_b)
```

```python
import functools
import math

import jax
import jax.numpy as jnp
from jax import lax
from jax.experimental import pallas as pl
from jax.experimental.pallas import tpu as pltpu

F32 = jnp.float32
BF16 = jnp.bfloat16
I32 = jnp.int32

D_MODEL = 2048
N_SAMPLE = 8 * 1024
SEQ_SAMPLE = 1024
N_PROMPT = 16 * 256
SEQ_PROMPT = 256
N_TOK = N_SAMPLE + N_PROMPT
GRID_W = 64
W_BRANCH = 1024
HEAD = 64
N_HEADS = 16
N_COND = 16
LORA_PAD = 128
LORA_W = 4 * LORA_PAD + 256
N_EXPERTS = 64
TOP_K = 8
N_GROUPS = 8
TOPK_GROUPS = 4
D_EXPERT = 512
ROUTED_SCALE = 2.5
NORM_EPS = 1e-6
GN_EPS = 64e-5
LANES = 128
CHAIN_BATCH = 8
TIME_ROWS = N_TOK // CHAIN_BATCH
CONV_HALO = 16
MOE_ROWS = 512
MOE_BLOCKS = N_TOK * TOP_K // MOE_ROWS + N_EXPERTS
EXPERT_PIECES = 8
COMBINE_ROWS = 128
DMA_UNROLL = 8
VMEM_LIMIT = 56 * 1024 * 1024


def _params(sem, vmem=VMEM_LIMIT):
    return pltpu.CompilerParams(dimension_semantics=sem, vmem_limit_bytes=vmem)


def _split_specs(tm, width):
    n_s = N_SAMPLE // tm
    return (pl.BlockSpec((tm, width), lambda i, *_: (jnp.minimum(i, n_s - 1), 0)),
            pl.BlockSpec((tm, width), lambda i, *_: (jnp.maximum(i - n_s, 0), 0)))


def _on_owner(is_sample, fn):
    pl.when(is_sample)(lambda: fn(True))
    pl.when(jnp.logical_not(is_sample))(lambda: fn(False))


class _ColumnTiles:
    def __init__(self, tt):
        self.tt = tt
        self.n_s = SEQ_SAMPLE // tt
        self.per_group = SEQ_PROMPT // tt
        self.grid = (TIME_ROWS // tt, CHAIN_BATCH)

    def is_sample(self):
        return pl.program_id(0) < self.n_s

    def tokens(self, width):
        return pl.BlockSpec((self.tt, width), lambda j, b, *_: (j, b))

    def sample_major(self, width):
        n_s = self.n_s
        return pl.BlockSpec((None, self.tt, width), lambda j, b, *_: (
            jnp.where(j < n_s, b, CHAIN_BATCH - 1), jnp.minimum(j, n_s - 1), 0))

    def prompt_major(self, width):
        n_s, per = self.n_s, self.per_group
        return pl.BlockSpec((None, self.tt, width), lambda j, b, *_: (
            jnp.where(j < n_s, 0, (j - n_s) // per * CHAIN_BATCH + b),
            jnp.where(j < n_s, 0, (j - n_s) % per), 0))

    def mod(self, chunk):
        n_s = self.n_s
        return pl.BlockSpec((None, 1, D_MODEL), lambda j, b, *_: (jnp.where(j < n_s, 1 + b, 0), 0, chunk))

    def const(self, shape):
        return pl.BlockSpec(shape, lambda j, b, *_: (0,) * len(shape))


def _rms(x):
    return x * lax.rsqrt(jnp.mean(x * x, axis=-1, keepdims=True) + NORM_EPS)


def _silu(x):
    return x * jax.nn.sigmoid(x)


def _ada_kernel(c_ref, w_ref, b_ref, o_ref):
    s = _silu(c_ref[...]).astype(BF16)
    o_ref[...] = jnp.dot(s, w_ref[...].astype(BF16), preferred_element_type=F32) + b_ref[...]


def _ada_table(cond, ada_w, ada_b):
    tn = 1536
    return pl.pallas_call(
        _ada_kernel,
        grid=(6 * D_MODEL // tn,),
        in_specs=[pl.BlockSpec((N_COND, D_MODEL), lambda j: (0, 0)),
                  pl.BlockSpec((D_MODEL, tn), lambda j: (0, j)),
                  pl.BlockSpec((1, tn), lambda j: (0, j))],
        out_specs=pl.BlockSpec((N_COND, tn), lambda j: (0, j)),
        out_shape=jax.ShapeDtypeStruct((N_COND, 6 * D_MODEL), F32),
        compiler_params=_params(("arbitrary",)),
        name="ada_table",
    )(cond, ada_w, ada_b)


def _prenorm_kernel(xs_ref, xp_ref, g_ref, sh_ref, sc_ref, o_ref, *, tiles):
    def run(is_sample):
        y = _rms((xs_ref if is_sample else xp_ref)[...]) * g_ref[...]
        o_ref[...] = (y * (1.0 + sc_ref[...]) + sh_ref[...]).astype(o_ref.dtype)

    _on_owner(tiles.is_sample(), run)


def _prenorm(x_sample, x_prompt, gain, mod3):
    tiles = _ColumnTiles(256)
    return pl.pallas_call(
        functools.partial(_prenorm_kernel, tiles=tiles),
        grid=tiles.grid,
        in_specs=[tiles.sample_major(D_MODEL), tiles.prompt_major(D_MODEL),
                  tiles.const((1, D_MODEL)), tiles.mod(0), tiles.mod(1)],
        out_specs=tiles.tokens(D_MODEL),
        out_shape=jax.ShapeDtypeStruct((TIME_ROWS, CHAIN_BATCH * D_MODEL), BF16),
        compiler_params=_params(("arbitrary", "arbitrary")),
        name="prenorm",
    )(x_sample, x_prompt, gain, mod3, mod3)


def _mm_kernel(a_ref, w_ref, o_ref, *, act):
    acc = jnp.dot(a_ref[...], w_ref[...], preferred_element_type=F32)
    if act == "sigmoid":
        acc = jax.nn.sigmoid(acc)
    o_ref[...] = acc.astype(o_ref.dtype)


def _matmul(a, w, *, col0, n_cols, tn, out_dtype, act=None, tm=1024, name="matmul"):
    m, k = a.shape
    off = col0 // tn
    return pl.pallas_call(
        functools.partial(_mm_kernel, act=act),
        grid=(m // tm, n_cols // tn),
        in_specs=[pl.BlockSpec((tm, k), lambda i, j: (i, 0)),
                  pl.BlockSpec((k, tn), lambda i, j: (0, j + off))],
        out_specs=pl.BlockSpec((tm, tn), lambda i, j: (i, j)),
        out_shape=jax.ShapeDtypeStruct((m, n_cols), out_dtype),
        compiler_params=_params(("arbitrary", "arbitrary")),
        name=name,
    )(a, w)


def _convproj_kernel(hp_ref, h_ref, hn_ref, wb_ref, wc_ref, wx_ref, cw_ref, o_ref, *, tm):
    h = h_ref[...]
    h_ext = jnp.concatenate([hp_ref[...], h, hn_ref[...]], axis=0)
    cb = jnp.dot(h, wb_ref[...], preferred_element_type=F32)
    u_ext = (jnp.dot(h_ext, wc_ref[...], preferred_element_type=F32)
             * jnp.dot(h_ext, wx_ref[...], preferred_element_type=F32))
    u = u_ext[CONV_HALO:CONV_HALO + tm]
    u_prev = u_ext[CONV_HALO - CHAIN_BATCH:CONV_HALO - CHAIN_BATCH + tm]
    u_next = u_ext[CONV_HALO + CHAIN_BATCH:CONV_HALO + CHAIN_BATCH + tm]
    seg = jnp.where(pl.program_id(0) < N_SAMPLE // tm, GRID_W, SEQ_PROMPT)
    row = lax.broadcasted_iota(I32, u.shape, 0)
    pos = (pl.program_id(0) * (tm // CHAIN_BATCH) + row // CHAIN_BATCH) & (seg - 1)
    u_prev = jnp.where(pos == 0, 0.0, u_prev)
    u_next = jnp.where(pos == seg - 1, 0.0, u_next)
    cw = cw_ref[...]
    conv = cw[0:1, :] * u_prev + cw[1:2, :] * u + cw[2:3, :] * u_next
    o_ref[...] = (cb * conv).astype(o_ref.dtype)


def _conv_branch(h, w_in_bf, conv_w):
    tm, tn = 1024, 256
    nb = W_BRANCH // tn
    per = tm // CONV_HALO
    return pl.pallas_call(
        functools.partial(_convproj_kernel, tm=tm),
        grid=(N_TOK // tm, nb),
        in_specs=[pl.BlockSpec((CONV_HALO, D_MODEL), lambda i, j: (jnp.maximum(i * per - 1, 0), 0)),
                  pl.BlockSpec((tm, D_MODEL), lambda i, j: (i, 0)),
                  pl.BlockSpec((CONV_HALO, D_MODEL),
                               lambda i, j: (jnp.minimum((i + 1) * per, N_TOK // CONV_HALO - 1), 0)),
                  pl.BlockSpec((D_MODEL, tn), lambda i, j: (0, j)),
                  pl.BlockSpec((D_MODEL, tn), lambda i, j: (0, j + nb)),
                  pl.BlockSpec((D_MODEL, tn), lambda i, j: (0, j + 2 * nb)),
                  pl.BlockSpec((3, tn), lambda i, j: (0, j))],
        out_specs=pl.BlockSpec((tm, tn), lambda i, j: (i, j)),
        out_shape=jax.ShapeDtypeStruct((N_TOK, W_BRANCH), BF16),
        compiler_params=_params(("arbitrary", "arbitrary")),
        name="conv_branch",
    )(h, h, h, w_in_bf, w_in_bf, w_in_bf, conv_w)


def _lora_kernel(x_ref, wd_ref, wa_ref, wg_ref, w0_ref, a0_ref, wa_out_ref, g_ref):
    def group(i):
        return x_ref[:, i * LORA_PAD:(i + 1) * LORA_PAD]

    def decay(wl):
        return jnp.exp(-jax.nn.sigmoid(wl) * math.exp(-0.5))

    for d in range(2):
        dl = jnp.dot(jnp.tanh(group(d)).astype(BF16), wd_ref[d], preferred_element_type=F32)
        wa_out_ref[:, d * W_BRANCH:(d + 1) * W_BRANCH] = decay(w0_ref[d:d + 1, :] + dl)
        al = jnp.dot(group(2 + d).astype(BF16), wa_ref[d], preferred_element_type=F32)
        wa_out_ref[:, (2 + d) * W_BRANCH:(3 + d) * W_BRANCH] = jax.nn.sigmoid(a0_ref[d:d + 1, :] + al)
    gl = jax.nn.sigmoid(x_ref[:, 4 * LORA_PAD:]).astype(BF16)
    g_ref[...] = jnp.dot(gl, wg_ref[...], preferred_element_type=F32)


def _lora_stage(lora, wd2, wa2, wg2, w0, a0):
    tm = 512
    return pl.pallas_call(
        _lora_kernel,
        grid=(N_TOK // tm,),
        in_specs=[pl.BlockSpec((tm, LORA_W), lambda i: (i, 0)),
                  pl.BlockSpec((2, LORA_PAD, W_BRANCH), lambda i: (0, 0, 0)),
                  pl.BlockSpec((2, LORA_PAD, W_BRANCH), lambda i: (0, 0, 0)),
                  pl.BlockSpec((256, W_BRANCH), lambda i: (0, 0)),
                  pl.BlockSpec((2, W_BRANCH), lambda i: (0, 0)),
                  pl.BlockSpec((2, W_BRANCH), lambda i: (0, 0))],
        out_specs=[pl.BlockSpec((tm, 4 * W_BRANCH), lambda i: (i, 0)),
                   pl.BlockSpec((tm, W_BRANCH), lambda i: (i, 0))],
        out_shape=[jax.ShapeDtypeStruct((N_TOK, 4 * W_BRANCH), F32),
                   jax.ShapeDtypeStruct((N_TOK, W_BRANCH), F32)],
        compiler_params=_params(("arbitrary",)),
        name="lora_stage",
    )(lora, wd2, wa2, wg2, w0, a0)


def _chain_tiles(x_ref, t0):
    both = (x_ref[t0], x_ref[t0 + 1])
    rows = [both[tl][:, p * LANES:(p + 1) * LANES] for tl in range(2) for p in range(CHAIN_BATCH)]
    sq = jnp.concatenate(rows, axis=0).T
    top, bot = sq[:HEAD], sq[HEAD:]
    low = lax.broadcasted_iota(I32, (HEAD, LANES), 1) < HEAD
    return (jnp.where(low, top, pltpu.roll(bot, HEAD, 1)),
            jnp.where(low, pltpu.roll(top, HEAD, 1), bot))


def _to_token_rows(tile0, tile1):
    low = lax.broadcasted_iota(I32, (HEAD, LANES), 1) < HEAD
    top = jnp.where(low, tile0, pltpu.roll(tile1, HEAD, 1))
    bot = jnp.where(low, pltpu.roll(tile0, HEAD, 1), tile1)
    return jnp.concatenate([top, bot], axis=0).T


def _scan_kernel(*refs, tc, reverse):
    r_ref, k_ref, v_ref, w_ref, a_ref, s0_ref, kk_ref, ka_ref, rk_ref = refs[:9]
    if reverse:
        yf_ref, bonf_ref, lw_ref, lb_ref, o_ref, sf_ref, s_ref, p_ref, ops_a, ops_b = refs[9:]
    else:
        y_ref, bon_ref, sf_ref, s_ref, p_ref, ops_a, ops_b = refs[9:]
    c = pl.program_id(1)
    n_pairs = tc // 2

    @pl.when(c == 0)
    def _():
        s_ref[...] = s0_ref[...]

    p_ref[...] = jnp.ones_like(p_ref)

    def first_row(pair_idx):
        return 2 * ((n_pairs - 1 - pair_idx) if reverse else pair_idx)

    def row_of(s):
        return 1 - s if reverse else s

    def prepare(pair_idx, ops_ref):
        t0 = first_row(pair_idx)
        r2, k2, v2 = _chain_tiles(r_ref, t0), _chain_tiles(k_ref, t0), _chain_tiles(v_ref, t0)
        w2, a2 = _chain_tiles(w_ref, t0), _chain_tiles(a_ref, t0)
        p = p_ref[...]
        for s in range(2):
            r, k, v, w, a = (x[row_of(s)] for x in (r2, k2, v2, w2, a2))
            kk = k * kk_ref[...]
            kk = kk * lax.rsqrt(jnp.sum(kk * kk, axis=0, keepdims=True) + 1e-12)
            kd = k * (1.0 + (a - 1.0) * ka_ref[...])
            bonus = jnp.sum(r * kd * rk_ref[...], axis=0, keepdims=True) * v
            if reverse:
                ops_ref[s, 5] = bonus
            else:
                bon_ref[t0 + row_of(s)] = bonus
            p_new = p * w
            inv = 1.0 / p_new
            for q, val in enumerate((kk * p, kk * a * inv, kd * inv, r * p_new, v)):
                ops_ref[s, q] = val
            p = p_new
        p_ref[...] = p

    def recur(pair_idx, ops_ref):
        t0 = first_row(pair_idx)
        ys = [None, None]
        for s in range(2):
            vt = ops_ref[s, 4]
            sa = jnp.zeros((HEAD, LANES), F32)
            for j in range(HEAD):
                sa = sa + s_ref[j] * ops_ref[s, 0, pl.ds(j, 1), :]
            y = jnp.zeros((HEAD, LANES), F32)
            for j in range(HEAD):
                un = s_ref[j] - sa * ops_ref[s, 1, pl.ds(j, 1), :] + vt * ops_ref[s, 2, pl.ds(j, 1), :]
                s_ref[j] = un
                y = y + un * ops_ref[s, 3, pl.ds(j, 1), :]
            if reverse:
                ys[row_of(s)] = (y, s)
            else:
                y_ref[t0 + row_of(s)] = y
        if reverse:
            tiles = []
            for tl in range(2):
                y_rev, s = ys[tl]
                y_sum = y_rev + yf_ref[t0 + tl]
                dev = y_sum - jnp.mean(y_sum, axis=0, keepdims=True)
                var = jnp.mean(dev * dev, axis=0, keepdims=True)
                yn = dev * lax.rsqrt(var + GN_EPS) * lw_ref[...] + lb_ref[...]
                tiles.append(yn + ops_ref[s, 5] + bonf_ref[t0 + tl])
            sq = _to_token_rows(tiles[0], tiles[1])
            for tl in range(2):
                for p in range(CHAIN_BATCH):
                    r0 = tl * HEAD + p * CHAIN_BATCH
                    o_ref[t0 + tl, :, pl.ds(p * LANES, LANES)] = sq[r0:r0 + CHAIN_BATCH, :]

    prepare(0, ops_a)

    def two_pairs(i, carry):
        prepare(2 * i + 1, ops_b)
        recur(2 * i, ops_a)
        prepare(2 * i + 2, ops_a)
        recur(2 * i + 1, ops_b)
        return carry

    lax.fori_loop(0, n_pairs // 2 - 1, two_pairs, 0)
    prepare(n_pairs - 1, ops_b)
    recur(n_pairs - 2, ops_a)
    recur(n_pairs - 1, ops_b)

    for j in range(HEAD):
        s_ref[j] = s_ref[j] * p_ref[pl.ds(j, 1), :]

    @pl.when(c == pl.num_programs(1) - 1)
    def _():
        sf_ref[...] = s_ref[...]


def _scan(rkv3, wa3, row0, seq, s0_f, s0_b, kk_l, ka_l, rk_l, lw_l, lb_l, *, tc=64):
    g_n = s0_f.shape[0]
    nc = seq // tc
    scratch = [pltpu.VMEM((HEAD, HEAD, LANES), F32), pltpu.VMEM((HEAD, LANES), F32),
               pltpu.VMEM((2, 6, HEAD, LANES), F32), pltpu.VMEM((2, 6, HEAD, LANES), F32)]
    state = pl.BlockSpec((None, HEAD, HEAD, LANES), lambda g, c: (g, 0, 0, 0))
    par = pl.BlockSpec((HEAD, LANES), lambda g, c: (0, 0))
    state_shape = jax.ShapeDtypeStruct((g_n, HEAD, HEAD, LANES), F32)
    seq_shape = jax.ShapeDtypeStruct((g_n, seq, HEAD, LANES), F32)

    def specs(reverse):
        chunk = (lambda c: nc - 1 - c) if reverse else (lambda c: c)
        tok = lambda col: pl.BlockSpec((tc, CHAIN_BATCH, W_BRANCH),
                                       lambda g, c: (row0 // tc + g * nc + chunk(c), 0, col))
        chain = pl.BlockSpec((None, tc, HEAD, LANES), lambda g, c: (g, chunk(c), 0, 0))
        rows = pl.BlockSpec((tc, CHAIN_BATCH, W_BRANCH), lambda g, c: (g * nc + chunk(c), 0, 0))
        d = int(reverse)
        return [tok(0), tok(1), tok(2), tok(d), tok(2 + d), state, par, par, par], chain, rows

    ins, chain, _ = specs(False)
    y_f, bon_f, sfin_f = pl.pallas_call(
        functools.partial(_scan_kernel, tc=tc, reverse=False),
        grid=(g_n, nc),
        in_specs=ins,
        out_specs=[chain, chain, state],
        out_shape=[seq_shape, seq_shape, state_shape],
        scratch_shapes=scratch,
        compiler_params=_params(("arbitrary", "arbitrary")),
        name="wkv7_scan_fwd",
    )(rkv3, rkv3, rkv3, wa3, wa3, s0_f, kk_l, ka_l, rk_l)

    ins, chain, rows = specs(True)
    out, sfin_b = pl.pallas_call(
        functools.partial(_scan_kernel, tc=tc, reverse=True),
        grid=(g_n, nc),
        in_specs=ins + [chain, chain, par, par],
        out_specs=[rows, state],
        out_shape=[jax.ShapeDtypeStruct((g_n * seq, CHAIN_BATCH, W_BRANCH), F32), state_shape],
        scratch_shapes=scratch,
        compiler_params=_params(("arbitrary", "arbitrary")),
        name="wkv7_scan_rev",
    )(rkv3, rkv3, rkv3, wa3, wa3, s0_b, kk_l, ka_l, rk_l, y_f, bon_f, lw_l, lb_l)
    return out, sfin_f, sfin_b


def _state_to_chain_layout(s):
    g_n = s.shape[0] // CHAIN_BATCH
    s = s.reshape(g_n, CHAIN_BATCH, N_HEADS // 2, 2, HEAD, HEAD)
    return jnp.transpose(s, (0, 5, 4, 3, 2, 1)).reshape(g_n, HEAD, HEAD, LANES)


def _state_from_chain_layout(s):
    g_n = s.shape[0]
    s = s.reshape(g_n, HEAD, HEAD, 2, N_HEADS // 2, CHAIN_BATCH)
    return jnp.transpose(s, (0, 5, 4, 3, 2, 1)).reshape(g_n * CHAIN_BATCH, N_HEADS, HEAD, HEAD)


def _head_param_to_chain_layout(p):
    p = jnp.transpose(p.reshape(N_HEADS // 2, 2, HEAD), (1, 0, 2))
    p = jnp.broadcast_to(p[:, :, None, :], (2, N_HEADS // 2, CHAIN_BATCH, HEAD))
    return p.reshape(LANES, HEAD).T


def _merge_kernel(z_ref, ybs_ref, ybp_ref, g_ref, ga_ref, gb_ref, wc_ref, wr_ref, o_ref, *, tm):
    def run(is_sample):
        y_a = jnp.dot(z_ref[...], wc_ref[...], preferred_element_type=F32)
        yb = ((ybs_ref if is_sample else ybp_ref)[...] * g_ref[...]).astype(BF16)
        y_b = jnp.dot(yb, wr_ref[...], preferred_element_type=F32)
        o_ref[...] = (ga_ref[...].astype(F32) * y_a + gb_ref[...].astype(F32) * y_b).astype(o_ref.dtype)

    _on_owner(pl.program_id(0) < N_SAMPLE // tm, run)


def _merge(z, yb_s, yb_p, g, gates, w_conv_bf, w_rwkv_bf):
    tm = 512
    row = lambda w: pl.BlockSpec((tm, w), lambda i: (i, 0))
    return pl.pallas_call(
        functools.partial(_merge_kernel, tm=tm),
        grid=(N_TOK // tm,),
        in_specs=[row(W_BRANCH), *_split_specs(tm, W_BRANCH), row(W_BRANCH),
                  pl.BlockSpec((tm, D_MODEL), lambda i: (i, 0)),
                  pl.BlockSpec((tm, D_MODEL), lambda i: (i, 1)),
                  pl.BlockSpec((W_BRANCH, D_MODEL), lambda i: (0, 0)),
                  pl.BlockSpec((W_BRANCH, D_MODEL), lambda i: (0, 0))],
        out_specs=row(D_MODEL),
        out_shape=jax.ShapeDtypeStruct((N_TOK, D_MODEL), BF16),
        compiler_params=_params(("arbitrary",)),
        name="merge",
    )(z, yb_s, yb_p, g, gates, gates, w_conv_bf, w_rwkv_bf)


def _outproj_kernel(m_ref, wo_ref, xs_ref, xp_ref, g1_ref, sh2_ref, sc2_ref, npost_ref, npre_ref,
                    x1_ref, h2_ref, *, tiles):
    def run(is_sample):
        out = jnp.dot(m_ref[...], wo_ref[...], preferred_element_type=F32)
        x1 = (xs_ref if is_sample else xp_ref)[...] + g1_ref[...] * (_rms(out) * npost_ref[...])
        x1_ref[...] = x1
        h2_ref[...] = (_rms(x1) * npre_ref[...]) * (1.0 + sc2_ref[...]) + sh2_ref[...]

    _on_owner(tiles.is_sample(), run)


def _outproj(merged, w_o_bf, x_sample, x_prompt, mod3, n_post, n_pre):
    tiles = _ColumnTiles(256)
    row = tiles.tokens(D_MODEL)
    vec = tiles.const((1, D_MODEL))
    out = jax.ShapeDtypeStruct((TIME_ROWS, CHAIN_BATCH * D_MODEL), F32)
    return pl.pallas_call(
        functools.partial(_outproj_kernel, tiles=tiles),
        grid=tiles.grid,
        in_specs=[row, tiles.const((D_MODEL, D_MODEL)),
                  tiles.sample_major(D_MODEL), tiles.prompt_major(D_MODEL),
                  tiles.mod(2), tiles.mod(3), tiles.mod(4), vec, vec],
        out_specs=[row, row],
        out_shape=[out, out],
        compiler_params=_params(("arbitrary", "arbitrary")),
        name="outproj",
    )(merged, w_o_bf, x_sample, x_prompt, mod3, mod3, mod3, n_post, n_pre)


def _first_index_of_max(x, axis, n):
    m = jnp.max(x, axis=axis, keepdims=True)
    idx = lax.broadcasted_iota(I32, x.shape, axis).astype(F32)
    first = jnp.min(jnp.where(x == m, idx, float(n)), axis=axis, keepdims=True)
    return m, idx, first


def _router_kernel(h_ref, rw_ref, rb_ref, eidx_ref, wsel_ref, rank_ref, cnt_ref, base_ref, *, tm):
    @pl.when(pl.program_id(0) == 0)
    def _():
        base_ref[...] = jnp.zeros_like(base_ref)

    logits = lax.dot_general(rw_ref[...], h_ref[...], (((1,), (1,)), ((), ())),
                             precision=lax.Precision.HIGHEST, preferred_element_type=F32)
    scores = jax.nn.sigmoid(logits)
    biased = scores + rb_ref[...]
    neg = -jnp.inf

    per_group = N_EXPERTS // N_GROUPS
    grp = biased.reshape(N_GROUPS, per_group, tm)
    m1, idx, first = _first_index_of_max(grp, 1, per_group)
    m2 = jnp.max(jnp.where(idx == first, neg, grp), axis=1, keepdims=True)
    gscore = (m1 + m2).reshape(N_GROUPS, tm)

    gsel = jnp.zeros((N_GROUPS, tm), F32)
    for _ in range(TOPK_GROUPS):
        _, gidx, gfirst = _first_index_of_max(gscore, 0, N_GROUPS)
        hit = gidx == gfirst
        gsel = jnp.where(hit, 1.0, gsel)
        gscore = jnp.where(hit, neg, gscore)
    emask = jnp.broadcast_to(gsel[:, None, :], (N_GROUPS, per_group, tm)).reshape(N_EXPERTS, tm)

    cand = jnp.where(emask > 0.5, biased, neg)
    mem = jnp.zeros((N_EXPERTS, tm), F32)
    picks = []
    for _ in range(TOP_K):
        _, eidx, efirst = _first_index_of_max(cand, 0, N_EXPERTS)
        hit = eidx == efirst
        mem = jnp.where(hit, 1.0, mem)
        cand = jnp.where(hit, neg, cand)
        picks.append((efirst, hit))

    s_i = lax.broadcasted_iota(I32, (tm, tm), 0)
    t_i = lax.broadcasted_iota(I32, (tm, tm), 1)
    upper = (s_i <= t_i).astype(BF16)
    incl = jnp.dot(mem.astype(BF16), upper, preferred_element_type=F32)
    rank = base_ref[...] + incl - mem
    base_ref[...] = base_ref[...] + jnp.sum(mem, axis=1, keepdims=True)
    cnt_ref[...] = base_ref[...]

    wsum = jnp.zeros((1, tm), F32)
    wrows = []
    for j, (efirst, hit) in enumerate(picks):
        wj = jnp.sum(jnp.where(hit, scores, 0.0), axis=0, keepdims=True)
        wrows.append(wj)
        wsum = wsum + wj
        eidx_ref[pl.ds(j, 1), :] = efirst.astype(I32)
        rank_ref[pl.ds(j, 1), :] = jnp.sum(jnp.where(hit, rank, 0.0), axis=0, keepdims=True).astype(I32)
    for j, wj in enumerate(wrows):
        wsel_ref[pl.ds(j, 1), :] = wj / wsum * ROUTED_SCALE


def _router(h2, rw_t, rb_col):
    tm = 256
    tok = lambda dt: jax.ShapeDtypeStruct((TOP_K, N_TOK), dt)
    tspec = pl.BlockSpec((TOP_K, tm), lambda i: (0, i))
    return pl.pallas_call(
        functools.partial(_router_kernel, tm=tm),
        grid=(N_TOK // tm,),
        in_specs=[pl.BlockSpec((tm, D_MODEL), lambda i: (i, 0)),
                  pl.BlockSpec((N_EXPERTS, D_MODEL), lambda i: (0, 0)),
                  pl.BlockSpec((N_EXPERTS, 1), lambda i: (0, 0))],
        out_specs=[tspec, tspec, tspec, pl.BlockSpec((N_EXPERTS, 1), lambda i: (0, 0))],
        out_shape=[tok(I32), tok(F32), tok(I32), jax.ShapeDtypeStruct((N_EXPERTS, 1), F32)],
        scratch_shapes=[pltpu.VMEM((N_EXPERTS, 1), F32)],
        compiler_params=_params(("arbitrary",)),
        name="router",
    )(h2, rw_t, rb_col)


def _slot_kernel(start_ref, eidx_ref, rank_ref, pos_ref):
    e = eidx_ref[...]
    pos = rank_ref[...]
    for j in range(N_EXPERTS):
        pos = pos + jnp.where(e == j, start_ref[j], 0)
    pos_ref[...] = pos


def _slots(pad_start, eidx, rank):
    full = pl.BlockSpec((TOP_K, N_TOK), lambda i, s: (0, 0))
    return pl.pallas_call(
        _slot_kernel,
        grid_spec=pltpu.PrefetchScalarGridSpec(num_scalar_prefetch=1, grid=(1,),
                                               in_specs=[full, full], out_specs=full),
        out_shape=jax.ShapeDtypeStruct((TOP_K, N_TOK), I32),
        compiler_params=_params(("arbitrary",)),
        name="slots",
    )(pad_start, eidx, rank)


def _gather_rows(idx_ref, base, n_rows, src_hbm, dst, sem, first_row=None):
    def issue(i, carry):
        pltpu.make_async_copy(src_hbm.at[pl.ds(idx_ref[base + i], 1), :], dst.at[pl.ds(i, 1), :], sem).start()
        return carry

    if first_row is not None:
        for i in range(first_row, first_row + n_rows):
            issue(i, 0)
    else:
        lax.fori_loop(0, n_rows, issue, 0, unroll=DMA_UNROLL)


def _wait_rows(n_rows, src_hbm, dst, sem):
    pltpu.make_async_copy(src_hbm.at[pl.ds(0, n_rows), :], dst, sem).wait()


def _expert_kernel(be_ref, first_ref, nused_ref, tok_ref,
                   h_hbm, wg_ref, wu_ref, wd_ref, o_ref, xbuf, wgb, wub, wdb, sem):
    b = pl.program_id(0)
    n_used = nused_ref[0]
    slot = b % 2

    @pl.when(b == 0)
    def _():
        _gather_rows(tok_ref, 0, MOE_ROWS, h_hbm, xbuf.at[0], sem.at[0])

    @pl.when(b < n_used)
    def _():
        @pl.when(first_ref[b] == 1)
        def _():
            wgb[...] = wg_ref[...].astype(BF16)
            wub[...] = wu_ref[...].astype(BF16)
            wdb[...] = wd_ref[...].astype(BF16)

        _wait_rows(MOE_ROWS, h_hbm, xbuf.at[slot], sem.at[slot])
        x = xbuf[slot].astype(BF16)

        rows_per_group = MOE_ROWS // EXPERT_PIECES

        def lookahead(piece):
            _gather_rows(tok_ref, (b + 1) * MOE_ROWS, rows_per_group, h_hbm, xbuf.at[1 - slot],
                         sem.at[1 - slot], first_row=piece * rows_per_group)

        half = D_EXPERT // 2
        hid = []
        for n in range(2):
            cols = slice(n * half, (n + 1) * half)
            lookahead(2 * n)
            gate = jnp.dot(x, wgb[:, cols], preferred_element_type=F32)
            lookahead(2 * n + 1)
            up = jnp.dot(x, wub[:, cols], preferred_element_type=F32)
            hid.append((_silu(gate) * up).astype(BF16))
        hid = jnp.concatenate(hid, axis=1)
        n_out = EXPERT_PIECES - 4
        width = D_MODEL // n_out
        for n in range(n_out):
            cols = slice(n * width, (n + 1) * width)
            lookahead(4 + n)
            o_ref[:, cols] = jnp.dot(hid, wdb[:, cols], preferred_element_type=F32)

    @pl.when(b >= n_used)
    def _():
        @pl.when(b == n_used)
        def _():
            _wait_rows(MOE_ROWS, h_hbm, xbuf.at[slot], sem.at[slot])

        o_ref[...] = jnp.zeros_like(o_ref)


def _experts(blk_expert, blk_first, n_used, tok_of_slot, h2, exp_gate, exp_up, exp_down):
    wspec_in = pl.BlockSpec((None, D_MODEL, D_EXPERT), lambda b, be, fi, nu, tk: (be[b], 0, 0))
    wspec_out = pl.BlockSpec((None, D_EXPERT, D_MODEL), lambda b, be, fi, nu, tk: (be[b], 0, 0))
    grid_spec = pltpu.PrefetchScalarGridSpec(
        num_scalar_prefetch=4,
        grid=(MOE_BLOCKS,),
        in_specs=[pl.BlockSpec(memory_space=pl.ANY), wspec_in, wspec_in, wspec_out],
        out_specs=pl.BlockSpec((MOE_ROWS, D_MODEL), lambda b, be, fi, nu, tk: (b, 0)),
        scratch_shapes=[pltpu.VMEM((2, MOE_ROWS, D_MODEL), F32),
                        pltpu.VMEM((D_MODEL, D_EXPERT), BF16),
                        pltpu.VMEM((D_MODEL, D_EXPERT), BF16),
                        pltpu.VMEM((D_EXPERT, D_MODEL), BF16),
                        pltpu.SemaphoreType.DMA((2,))],
    )
    return pl.pallas_call(
        _expert_kernel,
        grid_spec=grid_spec,
        out_shape=jax.ShapeDtypeStruct((MOE_BLOCKS * MOE_ROWS, D_MODEL), F32),
        compiler_params=_params(("arbitrary",)),
        name="experts",
    )(blk_expert, blk_first, n_used, tok_of_slot, h2, exp_gate, exp_up, exp_down)


def _combine_kernel(pos_ref, y_hbm, wt_ref, h_ref, x1_ref, g2_ref, npost_ref,
                    sg_ref, su_ref, sd_ref, os_ref, op_ref, buf, sem, *, tiles):
    i = pl.program_id(0) * pl.num_programs(1) + pl.program_id(1)
    n_steps = pl.num_programs(0) * pl.num_programs(1)
    slot = i % 2
    tm = tiles.tt
    rows = TOP_K * tm

    @pl.when(i == 0)
    def _():
        _gather_rows(pos_ref, 0, rows, y_hbm, buf.at[0], sem.at[0])

    @pl.when(i + 1 < n_steps)
    def _():
        _gather_rows(pos_ref, (i + 1) * rows, rows, y_hbm, buf.at[1 - slot], sem.at[1 - slot])

    hb = h_ref[...].astype(BF16)
    gate = jnp.dot(hb, sg_ref[...], preferred_element_type=F32)
    up = jnp.dot(hb, su_ref[...], preferred_element_type=F32)
    hid = (_silu(gate) * up).astype(BF16)
    f = jnp.dot(hid, sd_ref[...], preferred_element_type=F32)

    _wait_rows(rows, y_hbm, buf.at[slot], sem.at[slot])
    wt = wt_ref[...]
    for j in range(TOP_K):
        f = f + buf[slot, pl.ds(j * tm, tm), :] * wt[:, j:j + 1]
    out = x1_ref[...] + g2_ref[...] * (_rms(f) * npost_ref[...])

    def store(is_sample):
        (os_ref if is_sample else op_ref)[...] = out

    _on_owner(tiles.is_sample(), store)


def _combine(pos_tiles, y_sorted, wsel_tiles, h2, x1, mod3, n_post, sg_bf, su_bf, sd_bf):
    tiles = _ColumnTiles(COMBINE_ROWS)
    tm = tiles.tt
    row = tiles.tokens(D_MODEL)
    grid_spec = pltpu.PrefetchScalarGridSpec(
        num_scalar_prefetch=1,
        grid=tiles.grid,
        in_specs=[pl.BlockSpec(memory_space=pl.ANY),
                  pl.BlockSpec((tm, TOP_K), lambda j, b, p: (j * CHAIN_BATCH + b, 0)),
                  row, row, tiles.mod(5),
                  tiles.const((1, D_MODEL)), tiles.const((D_MODEL, D_EXPERT)),
                  tiles.const((D_MODEL, D_EXPERT)), tiles.const((D_EXPERT, D_MODEL))],
        out_specs=[tiles.sample_major(D_MODEL), tiles.prompt_major(D_MODEL)],
        scratch_shapes=[pltpu.VMEM((2, TOP_K * tm, D_MODEL), F32), pltpu.SemaphoreType.DMA((2,))],
    )
    return pl.pallas_call(
        functools.partial(_combine_kernel, tiles=tiles),
        grid_spec=grid_spec,
        out_shape=[jax.ShapeDtypeStruct((N_SAMPLE // SEQ_SAMPLE, SEQ_SAMPLE, D_MODEL), F32),
                   jax.ShapeDtypeStruct((N_PROMPT // SEQ_PROMPT, SEQ_PROMPT, D_MODEL), F32)],
        compiler_params=_params(("arbitrary", "arbitrary")),
        name="combine",
    )(pos_tiles, y_sorted, wsel_tiles, h2, x1, mod3, n_post, sg_bf, su_bf, sd_bf)


def _pad_rows(w, rows):
    return jnp.pad(w, ((0, rows - w.shape[0]), (0, 0)))


def kernel(x_prompt, x_sample, state_fwd, state_bwd, c, c_ctx, ada_w, ada_b, norm_pre_mix, norm_post_mix, norm_pre_ffn, norm_post_ffn, w_in, conv_w, w_out_conv, decay_w0, decay_w2, iclr_a0, iclr_a2, gate_g2, k_k, k_a, r_k, lnx_w, lnx_b, w_out_rwkv, w_o, router_w, router_bias, exp_gate, exp_up, exp_down, sh_gate, sh_up, sh_down):
    cond = jnp.concatenate([c_ctx[None, :], c, jnp.zeros((N_COND - 1 - c.shape[0], D_MODEL), F32)], axis=0)
    mod3 = _ada_table(cond, ada_w[0], ada_b).reshape(N_COND, 1, 6 * D_MODEL)

    h = _prenorm(x_sample, x_prompt, norm_pre_mix, mod3).reshape(N_TOK, D_MODEL)
    w_in_bf = w_in[0].astype(BF16)
    z = _conv_branch(h, w_in_bf, conv_w[0])
    rkv = _matmul(h, w_in_bf, col0=3 * W_BRANCH, n_cols=3 * W_BRANCH, tn=1024, out_dtype=F32, name="proj_rkv")
    c0 = 6 * W_BRANCH
    pad_cols = lambda lo: jnp.pad(w_in_bf[:, lo:lo + 96], ((0, 0), (0, LORA_PAD - 96)))
    w_lora = jnp.concatenate([pad_cols(c0), pad_cols(c0 + 96), pad_cols(c0 + 192), pad_cols(c0 + 288),
                              w_in_bf[:, c0 + 384:c0 + 640]], axis=1)
    lora = _matmul(h, w_lora, col0=0, n_cols=LORA_W, tn=LORA_W, out_dtype=F32, name="proj_lora")
    gates = _matmul(h, w_in_bf[:, c0 + 640:], col0=0, n_cols=2 * D_MODEL, tn=1024,
                    out_dtype=BF16, act="sigmoid", name="proj_gates")

    wd2 = jnp.stack([_pad_rows(decay_w2[0, 0], LORA_PAD), _pad_rows(decay_w2[0, 1], LORA_PAD)]).astype(BF16)
    wa2 = jnp.stack([_pad_rows(iclr_a2[0, 0], LORA_PAD), _pad_rows(iclr_a2[0, 1], LORA_PAD)]).astype(BF16)
    wa, g = _lora_stage(lora, wd2, wa2, gate_g2[0].astype(BF16), decay_w0[0], iclr_a0[0])

    kk_l = _head_param_to_chain_layout(k_k[0])
    ka_l = _head_param_to_chain_layout(k_a[0])
    rk_l = _head_param_to_chain_layout(r_k[0].reshape(-1))
    lw_l = _head_param_to_chain_layout(lnx_w[0])
    lb_l = _head_param_to_chain_layout(lnx_b[0])

    rkv3 = rkv.reshape(TIME_ROWS, CHAIN_BATCH, 3 * W_BRANCH)
    wa3 = wa.reshape(TIME_ROWS, CHAIN_BATCH, 4 * W_BRANCH)

    def run_scan(row0, seq, s0_f, s0_b):
        out, fin_f, fin_b = _scan(rkv3, wa3, row0, seq, _state_to_chain_layout(s0_f),
                                  _state_to_chain_layout(s0_b), kk_l, ka_l, rk_l, lw_l, lb_l)
        return out.reshape(-1, W_BRANCH), fin_f, fin_b

    zero_state = jnp.zeros((N_PROMPT // SEQ_PROMPT, N_HEADS, HEAD, HEAD), F32)
    yb_s, _, _ = run_scan(0, SEQ_SAMPLE, state_fwd[:, 0], state_bwd[:, 0])
    yb_p, fin_f, fin_b = run_scan(SEQ_SAMPLE, SEQ_PROMPT, zero_state, zero_state)

    merged = _merge(z, yb_s, yb_p, g, gates, w_out_conv[0].astype(BF16), w_out_rwkv[0].astype(BF16))
    x1, h2 = _outproj(merged.reshape(TIME_ROWS, CHAIN_BATCH * D_MODEL), w_o[0].astype(BF16),
                      x_sample, x_prompt, mod3, norm_post_mix, norm_pre_ffn)

    h2_flat = h2.reshape(N_TOK, D_MODEL)
    eidx, wsel, rank, counts = _router(h2_flat, router_w[0].T, router_bias[0][:, None])
    counts = counts[:, 0].astype(I32)
    padded = (counts + MOE_ROWS - 1) // MOE_ROWS * MOE_ROWS
    pad_end = jnp.cumsum(padded)
    pad_start = pad_end - padded
    pos = _slots(pad_start, eidx, rank)
    tok_ids = jnp.broadcast_to(jnp.arange(N_TOK, dtype=I32)[None, :], (TOP_K, N_TOK))
    tok_of_slot = jnp.zeros((MOE_BLOCKS * MOE_ROWS,), I32).at[pos.reshape(-1)].set(
        tok_ids.reshape(-1), unique_indices=True, mode="promise_in_bounds")
    blk_start = jnp.arange(MOE_BLOCKS, dtype=I32) * MOE_ROWS
    blk_expert = jnp.minimum(jnp.sum(pad_end[None, :] <= blk_start[:, None], axis=1), N_EXPERTS - 1).astype(I32)
    n_used = (pad_end[-1] // MOE_ROWS).astype(I32).reshape(1)
    blk_first = (blk_start == pad_start[blk_expert]).astype(I32)
    y_sorted = _experts(blk_expert, blk_first, n_used, tok_of_slot, h2_flat, exp_gate[0], exp_up[0], exp_down[0])

    def tile_order(a, perm):
        a = a.reshape(TOP_K, TIME_ROWS // COMBINE_ROWS, COMBINE_ROWS, CHAIN_BATCH)
        return jnp.transpose(a, perm)

    pos_tiles = tile_order(pos, (1, 3, 0, 2)).reshape(-1)
    wsel_tiles = tile_order(wsel, (1, 3, 2, 0)).reshape(N_TOK, TOP_K)
    out_s, out_p = _combine(pos_tiles, y_sorted, wsel_tiles, h2, x1, mod3, norm_post_ffn,
                            sh_gate[0].astype(BF16), sh_up[0].astype(BF16), sh_down[0].astype(BF16))

    new_f = _state_from_chain_layout(fin_f)[:, None]
    new_b = _state_from_chain_layout(fin_b)[:, None]
    return (out_p, out_s, new_f, new_b)
```

```python
import functools
import math

import jax
import jax.numpy as jnp
from jax import lax
from jax.experimental import pallas as pl
from jax.experimental.pallas import tpu as pltpu

F32 = jnp.float32
BF16 = jnp.bfloat16
I32 = jnp.int32

D_MODEL = 2048
N_SAMPLE = 8 * 1024
SEQ_SAMPLE = 1024
N_PROMPT = 16 * 256
SEQ_PROMPT = 256
N_TOK = N_SAMPLE + N_PROMPT
GRID_W = 64
W_BRANCH = 1024
HEAD = 64
N_HEADS = 16
N_COND = 16
LORA_PAD = 128
LORA_W = 4 * LORA_PAD + 256
N_EXPERTS = 64
TOP_K = 8
N_GROUPS = 8
TOPK_GROUPS = 4
D_EXPERT = 512
ROUTED_SCALE = 2.5
NORM_EPS = 1e-6
GN_EPS = 64e-5
LANES = 128
CHAIN_BATCH = 8
TIME_ROWS = N_TOK // CHAIN_BATCH
CONV_HALO = 16
MOE_ROWS = 512
MOE_BLOCKS = N_TOK * TOP_K // MOE_ROWS + N_EXPERTS
EXPERT_PIECES = 8
COMBINE_ROWS = 128
DMA_UNROLL = 8
VMEM_LIMIT = 56 * 1024 * 1024


def _params(sem, vmem=VMEM_LIMIT):
    return pltpu.CompilerParams(dimension_semantics=sem, vmem_limit_bytes=vmem)


def _split_specs(tm, width):
    n_s = N_SAMPLE // tm
    return (pl.BlockSpec((tm, width), lambda i, *_: (jnp.minimum(i, n_s - 1), 0)),
            pl.BlockSpec((tm, width), lambda i, *_: (jnp.maximum(i - n_s, 0), 0)))


def _on_owner(is_sample, fn):
    pl.when(is_sample)(lambda: fn(True))
    pl.when(jnp.logical_not(is_sample))(lambda: fn(False))


class _ColumnTiles:
    def __init__(self, tt):
        self.tt = tt
        self.n_s = SEQ_SAMPLE // tt
        self.per_group = SEQ_PROMPT // tt
        self.grid = (TIME_ROWS // tt, CHAIN_BATCH)

    def is_sample(self):
        return pl.program_id(0) < self.n_s

    def tokens(self, width):
        return pl.BlockSpec((self.tt, width), lambda j, b, *_: (j, b))

    def sample_major(self, width):
        n_s = self.n_s
        return pl.BlockSpec((None, self.tt, width), lambda j, b, *_: (
            jnp.where(j < n_s, b, CHAIN_BATCH - 1), jnp.minimum(j, n_s - 1), 0))

    def prompt_major(self, width):
        n_s, per = self.n_s, self.per_group
        return pl.BlockSpec((None, self.tt, width), lambda j, b, *_: (
            jnp.where(j < n_s, 0, (j - n_s) // per * CHAIN_BATCH + b),
            jnp.where(j < n_s, 0, (j - n_s) % per), 0))

    def mod(self, chunk):
        n_s = self.n_s
        return pl.BlockSpec((None, 1, D_MODEL), lambda j, b, *_: (jnp.where(j < n_s, 1 + b, 0), 0, chunk))

    def const(self, shape):
        return pl.BlockSpec(shape, lambda j, b, *_: (0,) * len(shape))


def _rms(x):
    return x * lax.rsqrt(jnp.mean(x * x, axis=-1, keepdims=True) + NORM_EPS)


def _silu(x):
    return x * jax.nn.sigmoid(x)


def _ada_kernel(c_ref, w_ref, b_ref, o_ref):
    s = _silu(c_ref[...]).astype(BF16)
    o_ref[...] = jnp.dot(s, w_ref[...].astype(BF16), preferred_element_type=F32) + b_ref[...]


def _ada_table(cond, ada_w, ada_b):
    tn = 1536
    return pl.pallas_call(
        _ada_kernel,
        grid=(6 * D_MODEL // tn,),
        in_specs=[pl.BlockSpec((N_COND, D_MODEL), lambda j: (0, 0)),
                  pl.BlockSpec((D_MODEL, tn), lambda j: (0, j)),
                  pl.BlockSpec((1, tn), lambda j: (0, j))],
        out_specs=pl.BlockSpec((N_COND, tn), lambda j: (0, j)),
        out_shape=jax.ShapeDtypeStruct((N_COND, 6 * D_MODEL), F32),
        compiler_params=_params(("arbitrary",)),
        name="ada_table",
    )(cond, ada_w, ada_b)


def _prenorm_kernel(xs_ref, xp_ref, g_ref, sh_ref, sc_ref, o_ref, *, tiles):
    def run(is_sample):
        y = _rms((xs_ref if is_sample else xp_ref)[...]) * g_ref[...]
        o_ref[...] = (y * (1.0 + sc_ref[...]) + sh_ref[...]).astype(o_ref.dtype)

    _on_owner(tiles.is_sample(), run)


def _prenorm(x_sample, x_prompt, gain, mod3):
    tiles = _ColumnTiles(256)
    return pl.pallas_call(
        functools.partial(_prenorm_kernel, tiles=tiles),
        grid=tiles.grid,
        in_specs=[tiles.sample_major(D_MODEL), tiles.prompt_major(D_MODEL),
                  tiles.const((1, D_MODEL)), tiles.mod(0), tiles.mod(1)],
        out_specs=tiles.tokens(D_MODEL),
        out_shape=jax.ShapeDtypeStruct((TIME_ROWS, CHAIN_BATCH * D_MODEL), BF16),
        compiler_params=_params(("arbitrary", "arbitrary")),
        name="prenorm",
    )(x_sample, x_prompt, gain, mod3, mod3)


def _mm_kernel(a_ref, w_ref, o_ref, *, act):
    acc = jnp.dot(a_ref[...], w_ref[...], preferred_element_type=F32)
    if act == "sigmoid":
        acc = jax.nn.sigmoid(acc)
    o_ref[...] = acc.astype(o_ref.dtype)


def _matmul(a, w, *, col0, n_cols, tn, out_dtype, act=None, tm=1024, name="matmul"):
    m, k = a.shape
    off = col0 // tn
    return pl.pallas_call(
        functools.partial(_mm_kernel, act=act),
        grid=(m // tm, n_cols // tn),
        in_specs=[pl.BlockSpec((tm, k), lambda i, j: (i, 0)),
                  pl.BlockSpec((k, tn), lambda i, j: (0, j + off))],
        out_specs=pl.BlockSpec((tm, tn), lambda i, j: (i, j)),
        out_shape=jax.ShapeDtypeStruct((m, n_cols), out_dtype),
        compiler_params=_params(("arbitrary", "arbitrary")),
        name=name,
    )(a, w)


def _convproj_kernel(hp_ref, h_ref, hn_ref, wb_ref, wc_ref, wx_ref, cw_ref, o_ref, *, tm):
    h = h_ref[...]
    h_ext = jnp.concatenate([hp_ref[...], h, hn_ref[...]], axis=0)
    cb = jnp.dot(h, wb_ref[...], preferred_element_type=F32)
    u_ext = (jnp.dot(h_ext, wc_ref[...], preferred_element_type=F32)
             * jnp.dot(h_ext, wx_ref[...], preferred_element_type=F32))
    u = u_ext[CONV_HALO:CONV_HALO + tm]
    u_prev = u_ext[CONV_HALO - CHAIN_BATCH:CONV_HALO - CHAIN_BATCH + tm]
    u_next = u_ext[CONV_HALO + CHAIN_BATCH:CONV_HALO + CHAIN_BATCH + tm]
    seg = jnp.where(pl.program_id(0) < N_SAMPLE // tm, GRID_W, SEQ_PROMPT)
    row = lax.broadcasted_iota(I32, u.shape, 0)
    pos = (pl.program_id(0) * (tm // CHAIN_BATCH) + row // CHAIN_BATCH) & (seg - 1)
    u_prev = jnp.where(pos == 0, 0.0, u_prev)
    u_next = jnp.where(pos == seg - 1, 0.0, u_next)
    cw = cw_ref[...]
    conv = cw[0:1, :] * u_prev + cw[1:2, :] * u + cw[2:3, :] * u_next
    o_ref[...] = (cb * conv).astype(o_ref.dtype)


def _conv_branch(h, w_in_bf, conv_w):
    tm, tn = 1024, 256
    nb = W_BRANCH // tn
    per = tm // CONV_HALO
    return pl.pallas_call(
        functools.partial(_convproj_kernel, tm=tm),
        grid=(N_TOK // tm, nb),
        in_specs=[pl.BlockSpec((CONV_HALO, D_MODEL), lambda i, j: (jnp.maximum(i * per - 1, 0), 0)),
                  pl.BlockSpec((tm, D_MODEL), lambda i, j: (i, 0)),
                  pl.BlockSpec((CONV_HALO, D_MODEL),
                               lambda i, j: (jnp.minimum((i + 1) * per, N_TOK // CONV_HALO - 1), 0)),
                  pl.BlockSpec((D_MODEL, tn), lambda i, j: (0, j)),
                  pl.BlockSpec((D_MODEL, tn), lambda i, j: (0, j + nb)),
                  pl.BlockSpec((D_MODEL, tn), lambda i, j: (0, j + 2 * nb)),
                  pl.BlockSpec((3, tn), lambda i, j: (0, j))],
        out_specs=pl.BlockSpec((tm, tn), lambda i, j: (i, j)),
        out_shape=jax.ShapeDtypeStruct((N_TOK, W_BRANCH), BF16),
        compiler_params=_params(("arbitrary", "arbitrary")),
        name="conv_branch",
    )(h, h, h, w_in_bf, w_in_bf, w_in_bf, conv_w)


def _lora_kernel(x_ref, wd_ref, wa_ref, wg_ref, w0_ref, a0_ref, wa_out_ref, g_ref):
    def group(i):
        return x_ref[:, i * LORA_PAD:(i + 1) * LORA_PAD]

    def decay(wl):
        return jnp.exp(-jax.nn.sigmoid(wl) * math.exp(-0.5))

    for d in range(2):
        dl = jnp.dot(jnp.tanh(group(d)).astype(BF16), wd_ref[d], preferred_element_type=F32)
        wa_out_ref[:, d * W_BRANCH:(d + 1) * W_BRANCH] = decay(w0_ref[d:d + 1, :] + dl)
        al = jnp.dot(group(2 + d).astype(BF16), wa_ref[d], preferred_element_type=F32)
        wa_out_ref[:, (2 + d) * W_BRANCH:(3 + d) * W_BRANCH] = jax.nn.sigmoid(a0_ref[d:d + 1, :] + al)
    gl = jax.nn.sigmoid(x_ref[:, 4 * LORA_PAD:]).astype(BF16)
    g_ref[...] = jnp.dot(gl, wg_ref[...], preferred_element_type=F32)


def _lora_stage(lora, wd2, wa2, wg2, w0, a0):
    tm = 512
    return pl.pallas_call(
        _lora_kernel,
        grid=(N_TOK // tm,),
        in_specs=[pl.BlockSpec((tm, LORA_W), lambda i: (i, 0)),
                  pl.BlockSpec((2, LORA_PAD, W_BRANCH), lambda i: (0, 0, 0)),
                  pl.BlockSpec((2, LORA_PAD, W_BRANCH), lambda i: (0, 0, 0)),
                  pl.BlockSpec((256, W_BRANCH), lambda i: (0, 0)),
                  pl.BlockSpec((2, W_BRANCH), lambda i: (0, 0)),
                  pl.BlockSpec((2, W_BRANCH), lambda i: (0, 0))],
        out_specs=[pl.BlockSpec((tm, 4 * W_BRANCH), lambda i: (i, 0)),
                   pl.BlockSpec((tm, W_BRANCH), lambda i: (i, 0))],
        out_shape=[jax.ShapeDtypeStruct((N_TOK, 4 * W_BRANCH), F32),
                   jax.ShapeDtypeStruct((N_TOK, W_BRANCH), F32)],
        compiler_params=_params(("arbitrary",)),
        name="lora_stage",
    )(lora, wd2, wa2, wg2, w0, a0)


def _chain_tiles(x_ref, t0):
    both = (x_ref[t0], x_ref[t0 + 1])
    rows = [both[tl][:, p * LANES:(p + 1) * LANES] for tl in range(2) for p in range(CHAIN_BATCH)]
    sq = jnp.concatenate(rows, axis=0).T
    top, bot = sq[:HEAD], sq[HEAD:]
    low = lax.broadcasted_iota(I32, (HEAD, LANES), 1) < HEAD
    return (jnp.where(low, top, pltpu.roll(bot, HEAD, 1)),
            jnp.where(low, pltpu.roll(top, HEAD, 1), bot))


def _to_token_rows(tile0, tile1):
    low = lax.broadcasted_iota(I32, (HEAD, LANES), 1) < HEAD
    top = jnp.where(low, tile0, pltpu.roll(tile1, HEAD, 1))
    bot = jnp.where(low, pltpu.roll(tile0, HEAD, 1), tile1)
    return jnp.concatenate([top, bot], axis=0).T


def _scan_kernel(*refs, tc, reverse):
    r_ref, k_ref, v_ref, w_ref, a_ref, s0_ref, kk_ref, ka_ref, rk_ref = refs[:9]
    if reverse:
        yf_ref, bonf_ref, lw_ref, lb_ref, o_ref, sf_ref, s_ref, p_ref, ops_a, ops_b = refs[9:]
    else:
        y_ref, bon_ref, sf_ref, s_ref, p_ref, ops_a, ops_b = refs[9:]
    c = pl.program_id(1)
    n_pairs = tc // 2

    @pl.when(c == 0)
    def _():
        s_ref[...] = s0_ref[...]

    p_ref[...] = jnp.ones_like(p_ref)

    def first_row(pair_idx):
        return 2 * ((n_pairs - 1 - pair_idx) if reverse else pair_idx)

    def row_of(s):
        return 1 - s if reverse else s

    def prepare(pair_idx, ops_ref):
        t0 = first_row(pair_idx)
        r2, k2, v2 = _chain_tiles(r_ref, t0), _chain_tiles(k_ref, t0), _chain_tiles(v_ref, t0)
        w2, a2 = _chain_tiles(w_ref, t0), _chain_tiles(a_ref, t0)
        p = p_ref[...]
        for s in range(2):
            r, k, v, w, a = (x[row_of(s)] for x in (r2, k2, v2, w2, a2))
            kk = k * kk_ref[...]
            kk = kk * lax.rsqrt(jnp.sum(kk * kk, axis=0, keepdims=True) + 1e-12)
            kd = k * (1.0 + (a - 1.0) * ka_ref[...])
            bonus = jnp.sum(r * kd * rk_ref[...], axis=0, keepdims=True) * v
            if reverse:
                ops_ref[s, 5] = bonus
            else:
                bon_ref[t0 + row_of(s)] = bonus
            p_new = p * w
            inv = 1.0 / p_new
            for q, val in enumerate((kk * p, kk * a * inv, kd * inv, r * p_new, v)):
                ops_ref[s, q] = val
            p = p_new
        p_ref[...] = p

    def recur(pair_idx, ops_ref):
        t0 = first_row(pair_idx)
        ys = [None, None]
        for s in range(2):
            vt = ops_ref[s, 4]
            sa = jnp.zeros((HEAD, LANES), F32)
            for j in range(HEAD):
                sa = sa + s_ref[j] * ops_ref[s, 0, pl.ds(j, 1), :]
            y = jnp.zeros((HEAD, LANES), F32)
            for j in range(HEAD):
                un = s_ref[j] - sa * ops_ref[s, 1, pl.ds(j, 1), :] + vt * ops_ref[s, 2, pl.ds(j, 1), :]
                s_ref[j] = un
                y = y + un * ops_ref[s, 3, pl.ds(j, 1), :]
            if reverse:
                ys[row_of(s)] = (y, s)
            else:
                y_ref[t0 + row_of(s)] = y
        if reverse:
            tiles = []
            for tl in range(2):
                y_rev, s = ys[tl]
                y_sum = y_rev + yf_ref[t0 + tl]
                dev = y_sum - jnp.mean(y_sum, axis=0, keepdims=True)
                var = jnp.mean(dev * dev, axis=0, keepdims=True)
                yn = dev * lax.rsqrt(var + GN_EPS) * lw_ref[...] + lb_ref[...]
                tiles.append(yn + ops_ref[s, 5] + bonf_ref[t0 + tl])
            sq = _to_token_rows(tiles[0], tiles[1])
            for tl in range(2):
                for p in range(CHAIN_BATCH):
                    r0 = tl * HEAD + p * CHAIN_BATCH
                    o_ref[t0 + tl, :, pl.ds(p * LANES, LANES)] = sq[r0:r0 + CHAIN_BATCH, :]

    prepare(0, ops_a)

    def two_pairs(i, carry):
        prepare(2 * i + 1, ops_b)
        recur(2 * i, ops_a)
        prepare(2 * i + 2, ops_a)
        recur(2 * i + 1, ops_b)
        return carry

    lax.fori_loop(0, n_pairs // 2 - 1, two_pairs, 0)
    prepare(n_pairs - 1, ops_b)
    recur(n_pairs - 2, ops_a)
    recur(n_pairs - 1, ops_b)

    for j in range(HEAD):
        s_ref[j] = s_ref[j] * p_ref[pl.ds(j, 1), :]

    @pl.when(c == pl.num_programs(1) - 1)
    def _():
        sf_ref[...] = s_ref[...]


def _scan(rkv3, wa3, row0, seq, s0_f, s0_b, kk_l, ka_l, rk_l, lw_l, lb_l, *, tc=64):
    g_n = s0_f.shape[0]
    nc = seq // tc
    scratch = [pltpu.VMEM((HEAD, HEAD, LANES), F32), pltpu.VMEM((HEAD, LANES), F32),
               pltpu.VMEM((2, 6, HEAD, LANES), F32), pltpu.VMEM((2, 6, HEAD, LANES), F32)]
    state = pl.BlockSpec((None, HEAD, HEAD, LANES), lambda g, c: (g, 0, 0, 0))
    par = pl.BlockSpec((HEAD, LANES), lambda g, c: (0, 0))
    state_shape = jax.ShapeDtypeStruct((g_n, HEAD, HEAD, LANES), F32)
    seq_shape = jax.ShapeDtypeStruct((g_n, seq, HEAD, LANES), F32)

    def specs(reverse):
        chunk = (lambda c: nc - 1 - c) if reverse else (lambda c: c)
        tok = lambda col: pl.BlockSpec((tc, CHAIN_BATCH, W_BRANCH),
                                       lambda g, c: (row0 // tc + g * nc + chunk(c), 0, col))
        chain = pl.BlockSpec((None, tc, HEAD, LANES), lambda g, c: (g, chunk(c), 0, 0))
        rows = pl.BlockSpec((tc, CHAIN_BATCH, W_BRANCH), lambda g, c: (g * nc + chunk(c), 0, 0))
        d = int(reverse)
        return [tok(0), tok(1), tok(2), tok(d), tok(2 + d), state, par, par, par], chain, rows

    ins, chain, _ = specs(False)
    y_f, bon_f, sfin_f = pl.pallas_call(
        functools.partial(_scan_kernel, tc=tc, reverse=False),
        grid=(g_n, nc),
        in_specs=ins,
        out_specs=[chain, chain, state],
        out_shape=[seq_shape, seq_shape, state_shape],
        scratch_shapes=scratch,
        compiler_params=_params(("arbitrary", "arbitrary")),
        name="wkv7_scan_fwd",
    )(rkv3, rkv3, rkv3, wa3, wa3, s0_f, kk_l, ka_l, rk_l)

    ins, chain, rows = specs(True)
    out, sfin_b = pl.pallas_call(
        functools.partial(_scan_kernel, tc=tc, reverse=True),
        grid=(g_n, nc),
        in_specs=ins + [chain, chain, par, par],
        out_specs=[rows, state],
        out_shape=[jax.ShapeDtypeStruct((g_n * seq, CHAIN_BATCH, W_BRANCH), F32), state_shape],
        scratch_shapes=scratch,
        compiler_params=_params(("arbitrary", "arbitrary")),
        name="wkv7_scan_rev",
    )(rkv3, rkv3, rkv3, wa3, wa3, s0_b, kk_l, ka_l, rk_l, y_f, bon_f, lw_l, lb_l)
    return out, sfin_f, sfin_b


def _state_to_chain_layout(s):
    g_n = s.shape[0] // CHAIN_BATCH
    s = s.reshape(g_n, CHAIN_BATCH, N_HEADS // 2, 2, HEAD, HEAD)
    return jnp.transpose(s, (0, 5, 4, 3, 2, 1)).reshape(g_n, HEAD, HEAD, LANES)


def _state_from_chain_layout(s):
    g_n = s.shape[0]
    s = s.reshape(g_n, HEAD, HEAD, 2, N_HEADS // 2, CHAIN_BATCH)
    return jnp.transpose(s, (0, 5, 4, 3, 2, 1)).reshape(g_n * CHAIN_BATCH, N_HEADS, HEAD, HEAD)


def _head_param_to_chain_layout(p):
    p = jnp.transpose(p.reshape(N_HEADS // 2, 2, HEAD), (1, 0, 2))
    p = jnp.broadcast_to(p[:, :, None, :], (2, N_HEADS // 2, CHAIN_BATCH, HEAD))
    return p.reshape(LANES, HEAD).T


def _merge_kernel(z_ref, ybs_ref, ybp_ref, g_ref, ga_ref, gb_ref, wc_ref, wr_ref, o_ref, *, tm):
    def run(is_sample):
        y_a = jnp.dot(z_ref[...], wc_ref[...], preferred_element_type=F32)
        yb = ((ybs_ref if is_sample else ybp_ref)[...] * g_ref[...]).astype(BF16)
        y_b = jnp.dot(yb, wr_ref[...], preferred_element_type=F32)
        o_ref[...] = (ga_ref[...].astype(F32) * y_a + gb_ref[...].astype(F32) * y_b).astype(o_ref.dtype)

    _on_owner(pl.program_id(0) < N_SAMPLE // tm, run)


def _merge(z, yb_s, yb_p, g, gates, w_conv_bf, w_rwkv_bf):
    tm = 512
    row = lambda w: pl.BlockSpec((tm, w), lambda i: (i, 0))
    return pl.pallas_call(
        functools.partial(_merge_kernel, tm=tm),
        grid=(N_TOK // tm,),
        in_specs=[row(W_BRANCH), *_split_specs(tm, W_BRANCH), row(W_BRANCH),
                  pl.BlockSpec((tm, D_MODEL), lambda i: (i, 0)),
                  pl.BlockSpec((tm, D_MODEL), lambda i: (i, 1)),
                  pl.BlockSpec((W_BRANCH, D_MODEL), lambda i: (0, 0)),
                  pl.BlockSpec((W_BRANCH, D_MODEL), lambda i: (0, 0))],
        out_specs=row(D_MODEL),
        out_shape=jax.ShapeDtypeStruct((N_TOK, D_MODEL), BF16),
        compiler_params=_params(("arbitrary",)),
        name="merge",
    )(z, yb_s, yb_p, g, gates, gates, w_conv_bf, w_rwkv_bf)


def _outproj_kernel(m_ref, wo_ref, xs_ref, xp_ref, g1_ref, sh2_ref, sc2_ref, npost_ref, npre_ref,
                    x1_ref, h2_ref, *, tiles):
    def run(is_sample):
        out = jnp.dot(m_ref[...], wo_ref[...], preferred_element_type=F32)
        x1 = (xs_ref if is_sample else xp_ref)[...] + g1_ref[...] * (_rms(out) * npost_ref[...])
        x1_ref[...] = x1
        h2_ref[...] = (_rms(x1) * npre_ref[...]) * (1.0 + sc2_ref[...]) + sh2_ref[...]

    _on_owner(tiles.is_sample(), run)


def _outproj(merged, w_o_bf, x_sample, x_prompt, mod3, n_post, n_pre):
    tiles = _ColumnTiles(256)
    row = tiles.tokens(D_MODEL)
    vec = tiles.const((1, D_MODEL))
    out = jax.ShapeDtypeStruct((TIME_ROWS, CHAIN_BATCH * D_MODEL), F32)
    return pl.pallas_call(
        functools.partial(_outproj_kernel, tiles=tiles),
        grid=tiles.grid,
        in_specs=[row, tiles.const((D_MODEL, D_MODEL)),
                  tiles.sample_major(D_MODEL), tiles.prompt_major(D_MODEL),
                  tiles.mod(2), tiles.mod(3), tiles.mod(4), vec, vec],
        out_specs=[row, row],
        out_shape=[out, out],
        compiler_params=_params(("arbitrary", "arbitrary")),
        name="outproj",
    )(merged, w_o_bf, x_sample, x_prompt, mod3, mod3, mod3, n_post, n_pre)


def _first_index_of_max(x, axis, n):
    m = jnp.max(x, axis=axis, keepdims=True)
    idx = lax.broadcasted_iota(I32, x.shape, axis).astype(F32)
    first = jnp.min(jnp.where(x == m, idx, float(n)), axis=axis, keepdims=True)
    return m, idx, first


def _router_kernel(h_ref, rw_ref, rb_ref, eidx_ref, wsel_ref, rank_ref, cnt_ref, base_ref, *, tm):
    @pl.when(pl.program_id(0) == 0)
    def _():
        base_ref[...] = jnp.zeros_like(base_ref)

    logits = lax.dot_general(rw_ref[...], h_ref[...], (((1,), (1,)), ((), ())),
                             precision=lax.Precision.HIGHEST, preferred_element_type=F32)
    scores = jax.nn.sigmoid(logits)
    biased = scores + rb_ref[...]
    neg = -jnp.inf

    per_group = N_EXPERTS // N_GROUPS
    grp = biased.reshape(N_GROUPS, per_group, tm)
    m1, idx, first = _first_index_of_max(grp, 1, per_group)
    m2 = jnp.max(jnp.where(idx == first, neg, grp), axis=1, keepdims=True)
    gscore = (m1 + m2).reshape(N_GROUPS, tm)

    gsel = jnp.zeros((N_GROUPS, tm), F32)
    for _ in range(TOPK_GROUPS):
        _, gidx, gfirst = _first_index_of_max(gscore, 0, N_GROUPS)
        hit = gidx == gfirst
        gsel = jnp.where(hit, 1.0, gsel)
        gscore = jnp.where(hit, neg, gscore)
    emask = jnp.broadcast_to(gsel[:, None, :], (N_GROUPS, per_group, tm)).reshape(N_EXPERTS, tm)

    cand = jnp.where(emask > 0.5, biased, neg)
    mem = jnp.zeros((N_EXPERTS, tm), F32)
    picks = []
    for _ in range(TOP_K):
        _, eidx, efirst = _first_index_of_max(cand, 0, N_EXPERTS)
        hit = eidx == efirst
        mem = jnp.where(hit, 1.0, mem)
        cand = jnp.where(hit, neg, cand)
        picks.append((efirst, hit))

    s_i = lax.broadcasted_iota(I32, (tm, tm), 0)
    t_i = lax.broadcasted_iota(I32, (tm, tm), 1)
    upper = (s_i <= t_i).astype(BF16)
    incl = jnp.dot(mem.astype(BF16), upper, preferred_element_type=F32)
    rank = base_ref[...] + incl - mem
    base_ref[...] = base_ref[...] + jnp.sum(mem, axis=1, keepdims=True)
    cnt_ref[...] = base_ref[...]

    wsum = jnp.zeros((1, tm), F32)
    wrows = []
    for j, (efirst, hit) in enumerate(picks):
        wj = jnp.sum(jnp.where(hit, scores, 0.0), axis=0, keepdims=True)
        wrows.append(wj)
        wsum = wsum + wj
        eidx_ref[pl.ds(j, 1), :] = efirst.astype(I32)
        rank_ref[pl.ds(j, 1), :] = jnp.sum(jnp.where(hit, rank, 0.0), axis=0, keepdims=True).astype(I32)
    for j, wj in enumerate(wrows):
        wsel_ref[pl.ds(j, 1), :] = wj / wsum * ROUTED_SCALE


def _router(h2, rw_t, rb_col):
    tm = 256
    tok = lambda dt: jax.ShapeDtypeStruct((TOP_K, N_TOK), dt)
    tspec = pl.BlockSpec((TOP_K, tm), lambda i: (0, i))
    return pl.pallas_call(
        functools.partial(_router_kernel, tm=tm),
        grid=(N_TOK // tm,),
        in_specs=[pl.BlockSpec((tm, D_MODEL), lambda i: (i, 0)),
                  pl.BlockSpec((N_EXPERTS, D_MODEL), lambda i: (0, 0)),
                  pl.BlockSpec((N_EXPERTS, 1), lambda i: (0, 0))],
        out_specs=[tspec, tspec, tspec, pl.BlockSpec((N_EXPERTS, 1), lambda i: (0, 0))],
        out_shape=[tok(I32), tok(F32), tok(I32), jax.ShapeDtypeStruct((N_EXPERTS, 1), F32)],
        scratch_shapes=[pltpu.VMEM((N_EXPERTS, 1), F32)],
        compiler_params=_params(("arbitrary",)),
        name="router",
    )(h2, rw_t, rb_col)


def _slot_kernel(start_ref, eidx_ref, rank_ref, pos_ref):
    e = eidx_ref[...]
    pos = rank_ref[...]
    for j in range(N_EXPERTS):
        pos = pos + jnp.where(e == j, start_ref[j], 0)
    pos_ref[...] = pos


def _slots(pad_start, eidx, rank):
    full = pl.BlockSpec((TOP_K, N_TOK), lambda i, s: (0, 0))
    return pl.pallas_call(
        _slot_kernel,
        grid_spec=pltpu.PrefetchScalarGridSpec(num_scalar_prefetch=1, grid=(1,),
                                               in_specs=[full, full], out_specs=full),
        out_shape=jax.ShapeDtypeStruct((TOP_K, N_TOK), I32),
        compiler_params=_params(("arbitrary",)),
        name="slots",
    )(pad_start, eidx, rank)


def _gather_rows(idx_ref, base, n_rows, src_hbm, dst, sem, first_row=None):
    def issue(i, carry):
        pltpu.make_async_copy(src_hbm.at[pl.ds(idx_ref[base + i], 1), :], dst.at[pl.ds(i, 1), :], sem).start()
        return carry

    if first_row is not None:
        for i in range(first_row, first_row + n_rows):
            issue(i, 0)
    else:
        lax.fori_loop(0, n_rows, issue, 0, unroll=DMA_UNROLL)


def _wait_rows(n_rows, src_hbm, dst, sem):
    pltpu.make_async_copy(src_hbm.at[pl.ds(0, n_rows), :], dst, sem).wait()


def _expert_kernel(be_ref, first_ref, nused_ref, base_ref, tok_ref,
                   h_hbm, wg_ref, wu_ref, wd_ref, o_ref, xbuf, wgb, wub, wdb, sem):
    b = pl.program_id(0)
    n_used = nused_ref[0]
    slot = b % 2

    @pl.when(b == 0)
    def _():
        _gather_rows(tok_ref, base_ref[0], MOE_ROWS, h_hbm, xbuf.at[0], sem.at[0])

    @pl.when(b < n_used)
    def _():
        @pl.when(first_ref[b] == 1)
        def _():
            wgb[...] = wg_ref[...].astype(BF16)
            wub[...] = wu_ref[...].astype(BF16)
            wdb[...] = wd_ref[...].astype(BF16)

        _wait_rows(MOE_ROWS, h_hbm, xbuf.at[slot], sem.at[slot])
        x = xbuf[slot].astype(BF16)

        rows_per_group = MOE_ROWS // EXPERT_PIECES

        def lookahead(piece):
            _gather_rows(tok_ref, base_ref[b + 1], rows_per_group, h_hbm, xbuf.at[1 - slot],
                         sem.at[1 - slot], first_row=piece * rows_per_group)

        half = D_EXPERT // 2
        hid = []
        for n in range(2):
            cols = slice(n * half, (n + 1) * half)
            lookahead(2 * n)
            gate = jnp.dot(x, wgb[:, cols], preferred_element_type=F32)
            lookahead(2 * n + 1)
            up = jnp.dot(x, wub[:, cols], preferred_element_type=F32)
            hid.append((_silu(gate) * up).astype(BF16))
        hid = jnp.concatenate(hid, axis=1)
        n_out = EXPERT_PIECES - 4
        width = D_MODEL // n_out
        for n in range(n_out):
            cols = slice(n * width, (n + 1) * width)
            lookahead(4 + n)
            o_ref[:, cols] = jnp.dot(hid, wdb[:, cols], preferred_element_type=F32)

    @pl.when(b >= n_used)
    def _():
        @pl.when(b == n_used)
        def _():
            _wait_rows(MOE_ROWS, h_hbm, xbuf.at[slot], sem.at[slot])

        o_ref[...] = jnp.zeros_like(o_ref)


def _experts(blk_expert, blk_first, n_used, blk_base, tok_dense, h2, exp_gate, exp_up, exp_down):
    wspec_in = pl.BlockSpec((None, D_MODEL, D_EXPERT), lambda b, be, *_: (be[b], 0, 0))
    wspec_out = pl.BlockSpec((None, D_EXPERT, D_MODEL), lambda b, be, *_: (be[b], 0, 0))
    grid_spec = pltpu.PrefetchScalarGridSpec(
        num_scalar_prefetch=5,
        grid=(MOE_BLOCKS,),
        in_specs=[pl.BlockSpec(memory_space=pl.ANY), wspec_in, wspec_in, wspec_out],
        out_specs=pl.BlockSpec((MOE_ROWS, D_MODEL), lambda b, *_: (b, 0)),
        scratch_shapes=[pltpu.VMEM((2, MOE_ROWS, D_MODEL), F32),
                        pltpu.VMEM((D_MODEL, D_EXPERT), BF16),
                        pltpu.VMEM((D_MODEL, D_EXPERT), BF16),
                        pltpu.VMEM((D_EXPERT, D_MODEL), BF16),
                        pltpu.SemaphoreType.DMA((2,))],
    )
    return pl.pallas_call(
        _expert_kernel,
        grid_spec=grid_spec,
        out_shape=jax.ShapeDtypeStruct((MOE_BLOCKS * MOE_ROWS, D_MODEL), F32),
        compiler_params=_params(("arbitrary",)),
        name="experts",
    )(blk_expert, blk_first, n_used, blk_base, tok_dense, h2, exp_gate, exp_up, exp_down)


def _combine_kernel(pos_ref, y_hbm, wt_ref, h_ref, x1_ref, g2_ref, npost_ref,
                    sg_ref, su_ref, sd_ref, os_ref, op_ref, buf, sem, *, tiles):
    i = pl.program_id(0) * pl.num_programs(1) + pl.program_id(1)
    n_steps = pl.num_programs(0) * pl.num_programs(1)
    slot = i % 2
    tm = tiles.tt
    rows = TOP_K * tm

    @pl.when(i == 0)
    def _():
        _gather_rows(pos_ref, 0, rows, y_hbm, buf.at[0], sem.at[0])

    @pl.when(i + 1 < n_steps)
    def _():
        _gather_rows(pos_ref, (i + 1) * rows, rows, y_hbm, buf.at[1 - slot], sem.at[1 - slot])

    hb = h_ref[...].astype(BF16)
    gate = jnp.dot(hb, sg_ref[...], preferred_element_type=F32)
    up = jnp.dot(hb, su_ref[...], preferred_element_type=F32)
    hid = (_silu(gate) * up).astype(BF16)
    f = jnp.dot(hid, sd_ref[...], preferred_element_type=F32)

    _wait_rows(rows, y_hbm, buf.at[slot], sem.at[slot])
    wt = wt_ref[...]
    for j in range(TOP_K):
        f = f + buf[slot, pl.ds(j * tm, tm), :] * wt[:, j:j + 1]
    out = x1_ref[...] + g2_ref[...] * (_rms(f) * npost_ref[...])

    def store(is_sample):
        (os_ref if is_sample else op_ref)[...] = out

    _on_owner(tiles.is_sample(), store)


def _combine(pos_tiles, y_sorted, wsel_tiles, h2, x1, mod3, n_post, sg_bf, su_bf, sd_bf):
    tiles = _ColumnTiles(COMBINE_ROWS)
    tm = tiles.tt
    row = tiles.tokens(D_MODEL)
    grid_spec = pltpu.PrefetchScalarGridSpec(
        num_scalar_prefetch=1,
        grid=tiles.grid,
        in_specs=[pl.BlockSpec(memory_space=pl.ANY),
                  pl.BlockSpec((tm, TOP_K), lambda j, b, p: (j * CHAIN_BATCH + b, 0)),
                  row, row, tiles.mod(5),
                  tiles.const((1, D_MODEL)), tiles.const((D_MODEL, D_EXPERT)),
                  tiles.const((D_MODEL, D_EXPERT)), tiles.const((D_EXPERT, D_MODEL))],
        out_specs=[tiles.sample_major(D_MODEL), tiles.prompt_major(D_MODEL)],
        scratch_shapes=[pltpu.VMEM((2, TOP_K * tm, D_MODEL), F32), pltpu.SemaphoreType.DMA((2,))],
    )
    return pl.pallas_call(
        functools.partial(_combine_kernel, tiles=tiles),
        grid_spec=grid_spec,
        out_shape=[jax.ShapeDtypeStruct((N_SAMPLE // SEQ_SAMPLE, SEQ_SAMPLE, D_MODEL), F32),
                   jax.ShapeDtypeStruct((N_PROMPT // SEQ_PROMPT, SEQ_PROMPT, D_MODEL), F32)],
        compiler_params=_params(("arbitrary", "arbitrary")),
        name="combine",
    )(pos_tiles, y_sorted, wsel_tiles, h2, x1, mod3, n_post, sg_bf, su_bf, sd_bf)


def _pad_rows(w, rows):
    return jnp.pad(w, ((0, rows - w.shape[0]), (0, 0)))


def kernel(x_prompt, x_sample, state_fwd, state_bwd, c, c_ctx, ada_w, ada_b, norm_pre_mix, norm_post_mix, norm_pre_ffn, norm_post_ffn, w_in, conv_w, w_out_conv, decay_w0, decay_w2, iclr_a0, iclr_a2, gate_g2, k_k, k_a, r_k, lnx_w, lnx_b, w_out_rwkv, w_o, router_w, router_bias, exp_gate, exp_up, exp_down, sh_gate, sh_up, sh_down):
    cond = jnp.concatenate([c_ctx[None, :], c, jnp.zeros((N_COND - 1 - c.shape[0], D_MODEL), F32)], axis=0)
    mod3 = _ada_table(cond, ada_w[0], ada_b).reshape(N_COND, 1, 6 * D_MODEL)

    h = _prenorm(x_sample, x_prompt, norm_pre_mix, mod3).reshape(N_TOK, D_MODEL)
    w_in_bf = w_in[0].astype(BF16)
    z = _conv_branch(h, w_in_bf, conv_w[0])
    rkv = _matmul(h, w_in_bf, col0=3 * W_BRANCH, n_cols=3 * W_BRANCH, tn=1024, out_dtype=F32, name="proj_rkv")
    c0 = 6 * W_BRANCH
    pad_cols = lambda lo: jnp.pad(w_in_bf[:, lo:lo + 96], ((0, 0), (0, LORA_PAD - 96)))
    w_lora = jnp.concatenate([pad_cols(c0), pad_cols(c0 + 96), pad_cols(c0 + 192), pad_cols(c0 + 288),
                              w_in_bf[:, c0 + 384:c0 + 640]], axis=1)
    lora = _matmul(h, w_lora, col0=0, n_cols=LORA_W, tn=LORA_W, out_dtype=F32, name="proj_lora")
    gates = _matmul(h, w_in_bf[:, c0 + 640:], col0=0, n_cols=2 * D_MODEL, tn=1024,
                    out_dtype=BF16, act="sigmoid", name="proj_gates")

    wd2 = jnp.stack([_pad_rows(decay_w2[0, 0], LORA_PAD), _pad_rows(decay_w2[0, 1], LORA_PAD)]).astype(BF16)
    wa2 = jnp.stack([_pad_rows(iclr_a2[0, 0], LORA_PAD), _pad_rows(iclr_a2[0, 1], LORA_PAD)]).astype(BF16)
    wa, g = _lora_stage(lora, wd2, wa2, gate_g2[0].astype(BF16), decay_w0[0], iclr_a0[0])

    kk_l = _head_param_to_chain_layout(k_k[0])
    ka_l = _head_param_to_chain_layout(k_a[0])
    rk_l = _head_param_to_chain_layout(r_k[0].reshape(-1))
    lw_l = _head_param_to_chain_layout(lnx_w[0])
    lb_l = _head_param_to_chain_layout(lnx_b[0])

    rkv3 = rkv.reshape(TIME_ROWS, CHAIN_BATCH, 3 * W_BRANCH)
    wa3 = wa.reshape(TIME_ROWS, CHAIN_BATCH, 4 * W_BRANCH)

    def run_scan(row0, seq, s0_f, s0_b):
        out, fin_f, fin_b = _scan(rkv3, wa3, row0, seq, _state_to_chain_layout(s0_f),
                                  _state_to_chain_layout(s0_b), kk_l, ka_l, rk_l, lw_l, lb_l)
        return out.reshape(-1, W_BRANCH), fin_f, fin_b

    zero_state = jnp.zeros((N_PROMPT // SEQ_PROMPT, N_HEADS, HEAD, HEAD), F32)
    yb_s, _, _ = run_scan(0, SEQ_SAMPLE, state_fwd[:, 0], state_bwd[:, 0])
    yb_p, fin_f, fin_b = run_scan(SEQ_SAMPLE, SEQ_PROMPT, zero_state, zero_state)

    merged = _merge(z, yb_s, yb_p, g, gates, w_out_conv[0].astype(BF16), w_out_rwkv[0].astype(BF16))
    x1, h2 = _outproj(merged.reshape(TIME_ROWS, CHAIN_BATCH * D_MODEL), w_o[0].astype(BF16),
                      x_sample, x_prompt, mod3, norm_post_mix, norm_pre_ffn)

    h2_flat = h2.reshape(N_TOK, D_MODEL)
    eidx, wsel, rank, counts = _router(h2_flat, router_w[0].T, router_bias[0][:, None])
    counts = counts[:, 0].astype(I32)
    padded = (counts + MOE_ROWS - 1) // MOE_ROWS * MOE_ROWS
    pad_end = jnp.cumsum(padded)
    pad_start = pad_end - padded
    pos = _slots(pad_start, eidx, rank)
    by_slot = jnp.argsort(pos.reshape(-1)).astype(I32) % N_TOK
    tok_dense = jnp.concatenate([by_slot, jnp.zeros((MOE_ROWS,), I32)])
    blk_start = jnp.arange(MOE_BLOCKS, dtype=I32) * MOE_ROWS
    blk_expert = jnp.minimum(jnp.sum(pad_end[None, :] <= blk_start[:, None], axis=1), N_EXPERTS - 1).astype(I32)
    n_blocks = pad_end[-1] // MOE_ROWS
    n_used = n_blocks.astype(I32).reshape(1)
    blk_first = (blk_start == pad_start[blk_expert]).astype(I32)
    seg_start = jnp.cumsum(counts) - counts
    blk_base = jnp.where(jnp.arange(MOE_BLOCKS) < n_blocks,
                         seg_start[blk_expert] + blk_start - pad_start[blk_expert], 0).astype(I32)
    y_sorted = _experts(blk_expert, blk_first, n_used, blk_base, tok_dense, h2_flat,
                        exp_gate[0], exp_up[0], exp_down[0])

    def tile_order(a, perm):
        a = a.reshape(TOP_K, TIME_ROWS // COMBINE_ROWS, COMBINE_ROWS, CHAIN_BATCH)
        return jnp.transpose(a, perm)

    pos_tiles = tile_order(pos, (1, 3, 0, 2)).reshape(-1)
    wsel_tiles = tile_order(wsel, (1, 3, 2, 0)).reshape(N_TOK, TOP_K)
    out_s, out_p = _combine(pos_tiles, y_sorted, wsel_tiles, h2, x1, mod3, norm_post_ffn,
                            sh_gate[0].astype(BF16), sh_up[0].astype(BF16), sh_down[0].astype(BF16))

    new_f = _state_from_chain_layout(fin_f)[:, None]
    new_b = _state_from_chain_layout(fin_b)[:, None]
    return (out_p, out_s, new_f, new_b)
```

```python
import functools
import math

import jax
import jax.numpy as jnp
from jax import lax
from jax.experimental import pallas as pl
from jax.experimental.pallas import tpu as pltpu

F32 = jnp.float32
BF16 = jnp.bfloat16
I32 = jnp.int32

D_MODEL = 2048
N_SAMPLE = 8 * 1024
SEQ_SAMPLE = 1024
N_PROMPT = 16 * 256
SEQ_PROMPT = 256
N_TOK = N_SAMPLE + N_PROMPT
GRID_W = 64
W_BRANCH = 1024
HEAD = 64
N_HEADS = 16
N_COND = 16
LORA_PAD = 128
LORA_W = 4 * LORA_PAD + 256
N_EXPERTS = 64
TOP_K = 8
N_GROUPS = 8
TOPK_GROUPS = 4
D_EXPERT = 512
ROUTED_SCALE = 2.5
NORM_EPS = 1e-6
GN_EPS = 64e-5
LANES = 128
CHAIN_BATCH = 8
TIME_ROWS = N_TOK // CHAIN_BATCH
CONV_HALO = 16
MOE_ROWS = 512
MOE_BLOCKS = N_TOK * TOP_K // MOE_ROWS + N_EXPERTS
EXPERT_PIECES = 8
COMBINE_ROWS = 128
COMBINE_PIECES = 8
DMA_UNROLL = 8
VMEM_LIMIT = 56 * 1024 * 1024


def _params(sem, vmem=VMEM_LIMIT):
    return pltpu.CompilerParams(dimension_semantics=sem, vmem_limit_bytes=vmem)


def _split_specs(tm, width):
    n_s = N_SAMPLE // tm
    return (pl.BlockSpec((tm, width), lambda i, *_: (jnp.minimum(i, n_s - 1), 0)),
            pl.BlockSpec((tm, width), lambda i, *_: (jnp.maximum(i - n_s, 0), 0)))


def _on_owner(is_sample, fn):
    pl.when(is_sample)(lambda: fn(True))
    pl.when(jnp.logical_not(is_sample))(lambda: fn(False))


class _ColumnTiles:
    def __init__(self, tt):
        self.tt = tt
        self.n_s = SEQ_SAMPLE // tt
        self.per_group = SEQ_PROMPT // tt
        self.grid = (TIME_ROWS // tt, CHAIN_BATCH)

    def is_sample(self):
        return pl.program_id(0) < self.n_s

    def tokens(self, width):
        return pl.BlockSpec((self.tt, width), lambda j, b, *_: (j, b))

    def sample_major(self, width):
        n_s = self.n_s
        return pl.BlockSpec((None, self.tt, width), lambda j, b, *_: (
            jnp.where(j < n_s, b, CHAIN_BATCH - 1), jnp.minimum(j, n_s - 1), 0))

    def prompt_major(self, width):
        n_s, per = self.n_s, self.per_group
        return pl.BlockSpec((None, self.tt, width), lambda j, b, *_: (
            jnp.where(j < n_s, 0, (j - n_s) // per * CHAIN_BATCH + b),
            jnp.where(j < n_s, 0, (j - n_s) % per), 0))

    def mod(self, chunk):
        n_s = self.n_s
        return pl.BlockSpec((None, 1, D_MODEL), lambda j, b, *_: (jnp.where(j < n_s, 1 + b, 0), 0, chunk))

    def const(self, shape):
        return pl.BlockSpec(shape, lambda j, b, *_: (0,) * len(shape))


def _rms(x):
    return x * lax.rsqrt(jnp.mean(x * x, axis=-1, keepdims=True) + NORM_EPS)


def _silu(x):
    return x * jax.nn.sigmoid(x)


def _ada_kernel(c_ref, w_ref, b_ref, o_ref):
    s = _silu(c_ref[...]).astype(BF16)
    o_ref[...] = jnp.dot(s, w_ref[...].astype(BF16), preferred_element_type=F32) + b_ref[...]


def _ada_table(cond, ada_w, ada_b):
    tn = 1536
    return pl.pallas_call(
        _ada_kernel,
        grid=(6 * D_MODEL // tn,),
        in_specs=[pl.BlockSpec((N_COND, D_MODEL), lambda j: (0, 0)),
                  pl.BlockSpec((D_MODEL, tn), lambda j: (0, j)),
                  pl.BlockSpec((1, tn), lambda j: (0, j))],
        out_specs=pl.BlockSpec((N_COND, tn), lambda j: (0, j)),
        out_shape=jax.ShapeDtypeStruct((N_COND, 6 * D_MODEL), F32),
        compiler_params=_params(("arbitrary",)),
        name="ada_table",
    )(cond, ada_w, ada_b)


def _prenorm_kernel(xs_ref, xp_ref, g_ref, sh_ref, sc_ref, o_ref, *, tiles):
    def run(is_sample):
        y = _rms((xs_ref if is_sample else xp_ref)[...]) * g_ref[...]
        o_ref[...] = (y * (1.0 + sc_ref[...]) + sh_ref[...]).astype(o_ref.dtype)

    _on_owner(tiles.is_sample(), run)


def _prenorm(x_sample, x_prompt, gain, mod3):
    tiles = _ColumnTiles(256)
    return pl.pallas_call(
        functools.partial(_prenorm_kernel, tiles=tiles),
        grid=tiles.grid,
        in_specs=[tiles.sample_major(D_MODEL), tiles.prompt_major(D_MODEL),
                  tiles.const((1, D_MODEL)), tiles.mod(0), tiles.mod(1)],
        out_specs=tiles.tokens(D_MODEL),
        out_shape=jax.ShapeDtypeStruct((TIME_ROWS, CHAIN_BATCH * D_MODEL), BF16),
        compiler_params=_params(("arbitrary", "arbitrary")),
        name="prenorm",
    )(x_sample, x_prompt, gain, mod3, mod3)


def _mm_kernel(a_ref, w_ref, o_ref, *, act):
    acc = jnp.dot(a_ref[...], w_ref[...], preferred_element_type=F32)
    if act == "sigmoid":
        acc = jax.nn.sigmoid(acc)
    o_ref[...] = acc.astype(o_ref.dtype)


def _matmul(a, w, *, col0, n_cols, tn, out_dtype, act=None, tm=1024, name="matmul"):
    m, k = a.shape
    off = col0 // tn
    return pl.pallas_call(
        functools.partial(_mm_kernel, act=act),
        grid=(m // tm, n_cols // tn),
        in_specs=[pl.BlockSpec((tm, k), lambda i, j: (i, 0)),
                  pl.BlockSpec((k, tn), lambda i, j: (0, j + off))],
        out_specs=pl.BlockSpec((tm, tn), lambda i, j: (i, j)),
        out_shape=jax.ShapeDtypeStruct((m, n_cols), out_dtype),
        compiler_params=_params(("arbitrary", "arbitrary")),
        name=name,
    )(a, w)


def _convproj_kernel(hp_ref, h_ref, hn_ref, wb_ref, wc_ref, wx_ref, cw_ref, o_ref, *, tm):
    h = h_ref[...]
    h_ext = jnp.concatenate([hp_ref[...], h, hn_ref[...]], axis=0)
    cb = jnp.dot(h, wb_ref[...], preferred_element_type=F32)
    u_ext = (jnp.dot(h_ext, wc_ref[...], preferred_element_type=F32)
             * jnp.dot(h_ext, wx_ref[...], preferred_element_type=F32))
    u = u_ext[CONV_HALO:CONV_HALO + tm]
    u_prev = u_ext[CONV_HALO - CHAIN_BATCH:CONV_HALO - CHAIN_BATCH + tm]
    u_next = u_ext[CONV_HALO + CHAIN_BATCH:CONV_HALO + CHAIN_BATCH + tm]
    seg = jnp.where(pl.program_id(0) < N_SAMPLE // tm, GRID_W, SEQ_PROMPT)
    row = lax.broadcasted_iota(I32, u.shape, 0)
    pos = (pl.program_id(0) * (tm // CHAIN_BATCH) + row // CHAIN_BATCH) & (seg - 1)
    u_prev = jnp.where(pos == 0, 0.0, u_prev)
    u_next = jnp.where(pos == seg - 1, 0.0, u_next)
    cw = cw_ref[...]
    conv = cw[0:1, :] * u_prev + cw[1:2, :] * u + cw[2:3, :] * u_next
    o_ref[...] = (cb * conv).astype(o_ref.dtype)


def _conv_branch(h, w_in_bf, conv_w):
    tm, tn = 1024, 256
    nb = W_BRANCH // tn
    per = tm // CONV_HALO
    return pl.pallas_call(
        functools.partial(_convproj_kernel, tm=tm),
        grid=(N_TOK // tm, nb),
        in_specs=[pl.BlockSpec((CONV_HALO, D_MODEL), lambda i, j: (jnp.maximum(i * per - 1, 0), 0)),
                  pl.BlockSpec((tm, D_MODEL), lambda i, j: (i, 0)),
                  pl.BlockSpec((CONV_HALO, D_MODEL),
                               lambda i, j: (jnp.minimum((i + 1) * per, N_TOK // CONV_HALO - 1), 0)),
                  pl.BlockSpec((D_MODEL, tn), lambda i, j: (0, j)),
                  pl.BlockSpec((D_MODEL, tn), lambda i, j: (0, j + nb)),
                  pl.BlockSpec((D_MODEL, tn), lambda i, j: (0, j + 2 * nb)),
                  pl.BlockSpec((3, tn), lambda i, j: (0, j))],
        out_specs=pl.BlockSpec((tm, tn), lambda i, j: (i, j)),
        out_shape=jax.ShapeDtypeStruct((N_TOK, W_BRANCH), BF16),
        compiler_params=_params(("arbitrary", "arbitrary")),
        name="conv_branch",
    )(h, h, h, w_in_bf, w_in_bf, w_in_bf, conv_w)


def _lora_kernel(x_ref, wd_ref, wa_ref, wg_ref, w0_ref, a0_ref, wa_out_ref, g_ref):
    def group(i):
        return x_ref[:, i * LORA_PAD:(i + 1) * LORA_PAD]

    def decay(wl):
        return jnp.exp(-jax.nn.sigmoid(wl) * math.exp(-0.5))

    for d in range(2):
        dl = jnp.dot(jnp.tanh(group(d)).astype(BF16), wd_ref[d], preferred_element_type=F32)
        wa_out_ref[:, d * W_BRANCH:(d + 1) * W_BRANCH] = decay(w0_ref[d:d + 1, :] + dl)
        al = jnp.dot(group(2 + d).astype(BF16), wa_ref[d], preferred_element_type=F32)
        wa_out_ref[:, (2 + d) * W_BRANCH:(3 + d) * W_BRANCH] = jax.nn.sigmoid(a0_ref[d:d + 1, :] + al)
    gl = jax.nn.sigmoid(x_ref[:, 4 * LORA_PAD:]).astype(BF16)
    g_ref[...] = jnp.dot(gl, wg_ref[...], preferred_element_type=F32)


def _lora_stage(lora, wd2, wa2, wg2, w0, a0):
    tm = 512
    return pl.pallas_call(
        _lora_kernel,
        grid=(N_TOK // tm,),
        in_specs=[pl.BlockSpec((tm, LORA_W), lambda i: (i, 0)),
                  pl.BlockSpec((2, LORA_PAD, W_BRANCH), lambda i: (0, 0, 0)),
                  pl.BlockSpec((2, LORA_PAD, W_BRANCH), lambda i: (0, 0, 0)),
                  pl.BlockSpec((256, W_BRANCH), lambda i: (0, 0)),
                  pl.BlockSpec((2, W_BRANCH), lambda i: (0, 0)),
                  pl.BlockSpec((2, W_BRANCH), lambda i: (0, 0))],
        out_specs=[pl.BlockSpec((tm, 4 * W_BRANCH), lambda i: (i, 0)),
                   pl.BlockSpec((tm, W_BRANCH), lambda i: (i, 0))],
        out_shape=[jax.ShapeDtypeStruct((N_TOK, 4 * W_BRANCH), F32),
                   jax.ShapeDtypeStruct((N_TOK, W_BRANCH), F32)],
        compiler_params=_params(("arbitrary",)),
        name="lora_stage",
    )(lora, wd2, wa2, wg2, w0, a0)


def _chain_tiles(x_ref, t0):
    both = (x_ref[t0], x_ref[t0 + 1])
    rows = [both[tl][:, p * LANES:(p + 1) * LANES] for tl in range(2) for p in range(CHAIN_BATCH)]
    sq = jnp.concatenate(rows, axis=0).T
    top, bot = sq[:HEAD], sq[HEAD:]
    low = lax.broadcasted_iota(I32, (HEAD, LANES), 1) < HEAD
    return (jnp.where(low, top, pltpu.roll(bot, HEAD, 1)),
            jnp.where(low, pltpu.roll(top, HEAD, 1), bot))


def _to_token_rows(tile0, tile1):
    low = lax.broadcasted_iota(I32, (HEAD, LANES), 1) < HEAD
    top = jnp.where(low, tile0, pltpu.roll(tile1, HEAD, 1))
    bot = jnp.where(low, pltpu.roll(tile0, HEAD, 1), tile1)
    return jnp.concatenate([top, bot], axis=0).T


def _scan_kernel(*refs, tc, reverse):
    r_ref, k_ref, v_ref, w_ref, a_ref, s0_ref, kk_ref, ka_ref, rk_ref = refs[:9]
    if reverse:
        yf_ref, bonf_ref, lw_ref, lb_ref, o_ref, sf_ref, s_ref, p_ref, ops_a, ops_b = refs[9:]
    else:
        y_ref, bon_ref, sf_ref, s_ref, p_ref, ops_a, ops_b = refs[9:]
    c = pl.program_id(1)
    n_pairs = tc // 2

    @pl.when(c == 0)
    def _():
        s_ref[...] = s0_ref[...]

    p_ref[...] = jnp.ones_like(p_ref)

    def first_row(pair_idx):
        return 2 * ((n_pairs - 1 - pair_idx) if reverse else pair_idx)

    def row_of(s):
        return 1 - s if reverse else s

    def prepare(pair_idx, ops_ref):
        t0 = first_row(pair_idx)
        r2, k2, v2 = _chain_tiles(r_ref, t0), _chain_tiles(k_ref, t0), _chain_tiles(v_ref, t0)
        w2, a2 = _chain_tiles(w_ref, t0), _chain_tiles(a_ref, t0)
        p = p_ref[...]
        for s in range(2):
            r, k, v, w, a = (x[row_of(s)] for x in (r2, k2, v2, w2, a2))
            kk = k * kk_ref[...]
            kk = kk * lax.rsqrt(jnp.sum(kk * kk, axis=0, keepdims=True) + 1e-12)
            kd = k * (1.0 + (a - 1.0) * ka_ref[...])
            bonus = jnp.sum(r * kd * rk_ref[...], axis=0, keepdims=True) * v
            if reverse:
                ops_ref[s, 5] = bonus
            else:
                bon_ref[t0 + row_of(s)] = bonus
            p_new = p * w
            inv = 1.0 / p_new
            for q, val in enumerate((kk * p, kk * a * inv, kd * inv, r * p_new, v)):
                ops_ref[s, q] = val
            p = p_new
        p_ref[...] = p

    def recur(pair_idx, ops_ref):
        t0 = first_row(pair_idx)
        ys = [None, None]
        for s in range(2):
            vt = ops_ref[s, 4]
            sa = jnp.zeros((HEAD, LANES), F32)
            for j in range(HEAD):
                sa = sa + s_ref[j] * ops_ref[s, 0, pl.ds(j, 1), :]
            y = jnp.zeros((HEAD, LANES), F32)
            for j in range(HEAD):
                un = s_ref[j] - sa * ops_ref[s, 1, pl.ds(j, 1), :] + vt * ops_ref[s, 2, pl.ds(j, 1), :]
                s_ref[j] = un
                y = y + un * ops_ref[s, 3, pl.ds(j, 1), :]
            if reverse:
                ys[row_of(s)] = (y, s)
            else:
                y_ref[t0 + row_of(s)] = y
        if reverse:
            tiles = []
            for tl in range(2):
                y_rev, s = ys[tl]
                y_sum = y_rev + yf_ref[t0 + tl]
                dev = y_sum - jnp.mean(y_sum, axis=0, keepdims=True)
                var = jnp.mean(dev * dev, axis=0, keepdims=True)
                yn = dev * lax.rsqrt(var + GN_EPS) * lw_ref[...] + lb_ref[...]
                tiles.append(yn + ops_ref[s, 5] + bonf_ref[t0 + tl])
            sq = _to_token_rows(tiles[0], tiles[1])
            for tl in range(2):
                for p in range(CHAIN_BATCH):
                    r0 = tl * HEAD + p * CHAIN_BATCH
                    o_ref[t0 + tl, :, pl.ds(p * LANES, LANES)] = sq[r0:r0 + CHAIN_BATCH, :]

    prepare(0, ops_a)

    def two_pairs(i, carry):
        prepare(2 * i + 1, ops_b)
        recur(2 * i, ops_a)
        prepare(2 * i + 2, ops_a)
        recur(2 * i + 1, ops_b)
        return carry

    lax.fori_loop(0, n_pairs // 2 - 1, two_pairs, 0)
    prepare(n_pairs - 1, ops_b)
    recur(n_pairs - 2, ops_a)
    recur(n_pairs - 1, ops_b)

    for j in range(HEAD):
        s_ref[j] = s_ref[j] * p_ref[pl.ds(j, 1), :]

    @pl.when(c == pl.num_programs(1) - 1)
    def _():
        sf_ref[...] = s_ref[...]


def _scan(rkv3, wa3, row0, seq, s0_f, s0_b, kk_l, ka_l, rk_l, lw_l, lb_l, *, tc=64):
    g_n = s0_f.shape[0]
    nc = seq // tc
    scratch = [pltpu.VMEM((HEAD, HEAD, LANES), F32), pltpu.VMEM((HEAD, LANES), F32),
               pltpu.VMEM((2, 6, HEAD, LANES), F32), pltpu.VMEM((2, 6, HEAD, LANES), F32)]
    state = pl.BlockSpec((None, HEAD, HEAD, LANES), lambda g, c: (g, 0, 0, 0))
    par = pl.BlockSpec((HEAD, LANES), lambda g, c: (0, 0))
    state_shape = jax.ShapeDtypeStruct((g_n, HEAD, HEAD, LANES), F32)
    seq_shape = jax.ShapeDtypeStruct((g_n, seq, HEAD, LANES), F32)

    def specs(reverse):
        chunk = (lambda c: nc - 1 - c) if reverse else (lambda c: c)
        tok = lambda col: pl.BlockSpec((tc, CHAIN_BATCH, W_BRANCH),
                                       lambda g, c: (row0 // tc + g * nc + chunk(c), 0, col))
        chain = pl.BlockSpec((None, tc, HEAD, LANES), lambda g, c: (g, chunk(c), 0, 0))
        rows = pl.BlockSpec((tc, CHAIN_BATCH, W_BRANCH), lambda g, c: (g * nc + chunk(c), 0, 0))
        d = int(reverse)
        return [tok(0), tok(1), tok(2), tok(d), tok(2 + d), state, par, par, par], chain, rows

    ins, chain, _ = specs(False)
    y_f, bon_f, sfin_f = pl.pallas_call(
        functools.partial(_scan_kernel, tc=tc, reverse=False),
        grid=(g_n, nc),
        in_specs=ins,
        out_specs=[chain, chain, state],
        out_shape=[seq_shape, seq_shape, state_shape],
        scratch_shapes=scratch,
        compiler_params=_params(("arbitrary", "arbitrary")),
        name="wkv7_scan_fwd",
    )(rkv3, rkv3, rkv3, wa3, wa3, s0_f, kk_l, ka_l, rk_l)

    ins, chain, rows = specs(True)
    out, sfin_b = pl.pallas_call(
        functools.partial(_scan_kernel, tc=tc, reverse=True),
        grid=(g_n, nc),
        in_specs=ins + [chain, chain, par, par],
        out_specs=[rows, state],
        out_shape=[jax.ShapeDtypeStruct((g_n * seq, CHAIN_BATCH, W_BRANCH), F32), state_shape],
        scratch_shapes=scratch,
        compiler_params=_params(("arbitrary", "arbitrary")),
        name="wkv7_scan_rev",
    )(rkv3, rkv3, rkv3, wa3, wa3, s0_b, kk_l, ka_l, rk_l, y_f, bon_f, lw_l, lb_l)
    return out, sfin_f, sfin_b


def _state_to_chain_layout(s):
    g_n = s.shape[0] // CHAIN_BATCH
    s = s.reshape(g_n, CHAIN_BATCH, N_HEADS // 2, 2, HEAD, HEAD)
    return jnp.transpose(s, (0, 5, 4, 3, 2, 1)).reshape(g_n, HEAD, HEAD, LANES)


def _state_from_chain_layout(s):
    g_n = s.shape[0]
    s = s.reshape(g_n, HEAD, HEAD, 2, N_HEADS // 2, CHAIN_BATCH)
    return jnp.transpose(s, (0, 5, 4, 3, 2, 1)).reshape(g_n * CHAIN_BATCH, N_HEADS, HEAD, HEAD)


def _head_param_to_chain_layout(p):
    p = jnp.transpose(p.reshape(N_HEADS // 2, 2, HEAD), (1, 0, 2))
    p = jnp.broadcast_to(p[:, :, None, :], (2, N_HEADS // 2, CHAIN_BATCH, HEAD))
    return p.reshape(LANES, HEAD).T


def _merge_kernel(z_ref, ybs_ref, ybp_ref, g_ref, ga_ref, gb_ref, wc_ref, wr_ref, o_ref, *, tm):
    def run(is_sample):
        y_a = jnp.dot(z_ref[...], wc_ref[...], preferred_element_type=F32)
        yb = ((ybs_ref if is_sample else ybp_ref)[...] * g_ref[...]).astype(BF16)
        y_b = jnp.dot(yb, wr_ref[...], preferred_element_type=F32)
        o_ref[...] = (ga_ref[...].astype(F32) * y_a + gb_ref[...].astype(F32) * y_b).astype(o_ref.dtype)

    _on_owner(pl.program_id(0) < N_SAMPLE // tm, run)


def _merge(z, yb_s, yb_p, g, gates, w_conv_bf, w_rwkv_bf):
    tm = 512
    row = lambda w: pl.BlockSpec((tm, w), lambda i: (i, 0))
    return pl.pallas_call(
        functools.partial(_merge_kernel, tm=tm),
        grid=(N_TOK // tm,),
        in_specs=[row(W_BRANCH), *_split_specs(tm, W_BRANCH), row(W_BRANCH),
                  pl.BlockSpec((tm, D_MODEL), lambda i: (i, 0)),
                  pl.BlockSpec((tm, D_MODEL), lambda i: (i, 1)),
                  pl.BlockSpec((W_BRANCH, D_MODEL), lambda i: (0, 0)),
                  pl.BlockSpec((W_BRANCH, D_MODEL), lambda i: (0, 0))],
        out_specs=row(D_MODEL),
        out_shape=jax.ShapeDtypeStruct((N_TOK, D_MODEL), BF16),
        compiler_params=_params(("arbitrary",)),
        name="merge",
    )(z, yb_s, yb_p, g, gates, gates, w_conv_bf, w_rwkv_bf)


def _outproj_kernel(m_ref, wo_ref, xs_ref, xp_ref, g1_ref, sh2_ref, sc2_ref, npost_ref, npre_ref,
                    x1_ref, h2_ref, *, tiles):
    def run(is_sample):
        out = jnp.dot(m_ref[...], wo_ref[...], preferred_element_type=F32)
        x1 = (xs_ref if is_sample else xp_ref)[...] + g1_ref[...] * (_rms(out) * npost_ref[...])
        x1_ref[...] = x1
        h2_ref[...] = (_rms(x1) * npre_ref[...]) * (1.0 + sc2_ref[...]) + sh2_ref[...]

    _on_owner(tiles.is_sample(), run)


def _outproj(merged, w_o_bf, x_sample, x_prompt, mod3, n_post, n_pre):
    tiles = _ColumnTiles(256)
    row = tiles.tokens(D_MODEL)
    vec = tiles.const((1, D_MODEL))
    out = jax.ShapeDtypeStruct((TIME_ROWS, CHAIN_BATCH * D_MODEL), F32)
    return pl.pallas_call(
        functools.partial(_outproj_kernel, tiles=tiles),
        grid=tiles.grid,
        in_specs=[row, tiles.const((D_MODEL, D_MODEL)),
                  tiles.sample_major(D_MODEL), tiles.prompt_major(D_MODEL),
                  tiles.mod(2), tiles.mod(3), tiles.mod(4), vec, vec],
        out_specs=[row, row],
        out_shape=[out, out],
        compiler_params=_params(("arbitrary", "arbitrary")),
        name="outproj",
    )(merged, w_o_bf, x_sample, x_prompt, mod3, mod3, mod3, n_post, n_pre)


def _first_index_of_max(x, axis, n):
    m = jnp.max(x, axis=axis, keepdims=True)
    idx = lax.broadcasted_iota(I32, x.shape, axis).astype(F32)
    first = jnp.min(jnp.where(x == m, idx, float(n)), axis=axis, keepdims=True)
    return m, idx, first


def _router_kernel(h_ref, rw_ref, rb_ref, eidx_ref, wsel_ref, rank_ref, cnt_ref, base_ref, *, tm):
    @pl.when(pl.program_id(0) == 0)
    def _():
        base_ref[...] = jnp.zeros_like(base_ref)

    logits = lax.dot_general(rw_ref[...], h_ref[...], (((1,), (1,)), ((), ())),
                             precision=lax.Precision.HIGHEST, preferred_element_type=F32)
    scores = jax.nn.sigmoid(logits)
    biased = scores + rb_ref[...]
    neg = -jnp.inf

    per_group = N_EXPERTS // N_GROUPS
    grp = biased.reshape(N_GROUPS, per_group, tm)
    m1, idx, first = _first_index_of_max(grp, 1, per_group)
    m2 = jnp.max(jnp.where(idx == first, neg, grp), axis=1, keepdims=True)
    gscore = (m1 + m2).reshape(N_GROUPS, tm)

    gsel = jnp.zeros((N_GROUPS, tm), F32)
    for _ in range(TOPK_GROUPS):
        _, gidx, gfirst = _first_index_of_max(gscore, 0, N_GROUPS)
        hit = gidx == gfirst
        gsel = jnp.where(hit, 1.0, gsel)
        gscore = jnp.where(hit, neg, gscore)
    emask = jnp.broadcast_to(gsel[:, None, :], (N_GROUPS, per_group, tm)).reshape(N_EXPERTS, tm)

    cand = jnp.where(emask > 0.5, biased, neg)
    mem = jnp.zeros((N_EXPERTS, tm), F32)
    picks = []
    for _ in range(TOP_K):
        _, eidx, efirst = _first_index_of_max(cand, 0, N_EXPERTS)
        hit = eidx == efirst
        mem = jnp.where(hit, 1.0, mem)
        cand = jnp.where(hit, neg, cand)
        picks.append((efirst, hit))

    s_i = lax.broadcasted_iota(I32, (tm, tm), 0)
    t_i = lax.broadcasted_iota(I32, (tm, tm), 1)
    upper = (s_i <= t_i).astype(BF16)
    incl = jnp.dot(mem.astype(BF16), upper, preferred_element_type=F32)
    rank = base_ref[...] + incl - mem
    base_ref[...] = base_ref[...] + jnp.sum(mem, axis=1, keepdims=True)
    cnt_ref[...] = base_ref[...]

    wsum = jnp.zeros((1, tm), F32)
    wrows = []
    for j, (efirst, hit) in enumerate(picks):
        wj = jnp.sum(jnp.where(hit, scores, 0.0), axis=0, keepdims=True)
        wrows.append(wj)
        wsum = wsum + wj
        eidx_ref[pl.ds(j, 1), :] = efirst.astype(I32)
        rank_ref[pl.ds(j, 1), :] = jnp.sum(jnp.where(hit, rank, 0.0), axis=0, keepdims=True).astype(I32)
    for j, wj in enumerate(wrows):
        wsel_ref[pl.ds(j, 1), :] = wj / wsum * ROUTED_SCALE


def _router(h2, rw_t, rb_col):
    tm = 256
    tok = lambda dt: jax.ShapeDtypeStruct((TOP_K, N_TOK), dt)
    tspec = pl.BlockSpec((TOP_K, tm), lambda i: (0, i))
    return pl.pallas_call(
        functools.partial(_router_kernel, tm=tm),
        grid=(N_TOK // tm,),
        in_specs=[pl.BlockSpec((tm, D_MODEL), lambda i: (i, 0)),
                  pl.BlockSpec((N_EXPERTS, D_MODEL), lambda i: (0, 0)),
                  pl.BlockSpec((N_EXPERTS, 1), lambda i: (0, 0))],
        out_specs=[tspec, tspec, tspec, pl.BlockSpec((N_EXPERTS, 1), lambda i: (0, 0))],
        out_shape=[tok(I32), tok(F32), tok(I32), jax.ShapeDtypeStruct((N_EXPERTS, 1), F32)],
        scratch_shapes=[pltpu.VMEM((N_EXPERTS, 1), F32)],
        compiler_params=_params(("arbitrary",)),
        name="router",
    )(h2, rw_t, rb_col)


def _slot_kernel(start_ref, eidx_ref, rank_ref, pos_ref):
    e = eidx_ref[...]
    pos = rank_ref[...]
    for j in range(N_EXPERTS):
        pos = pos + jnp.where(e == j, start_ref[j], 0)
    pos_ref[...] = pos


def _slots(pad_start, eidx, rank):
    full = pl.BlockSpec((TOP_K, N_TOK), lambda i, s: (0, 0))
    return pl.pallas_call(
        _slot_kernel,
        grid_spec=pltpu.PrefetchScalarGridSpec(num_scalar_prefetch=1, grid=(1,),
                                               in_specs=[full, full], out_specs=full),
        out_shape=jax.ShapeDtypeStruct((TOP_K, N_TOK), I32),
        compiler_params=_params(("arbitrary",)),
        name="slots",
    )(pad_start, eidx, rank)


def _gather_rows(idx_ref, base, n_rows, src_hbm, dst, sem, first_row=None):
    def issue(i, carry):
        pltpu.make_async_copy(src_hbm.at[pl.ds(idx_ref[base + i], 1), :], dst.at[pl.ds(i, 1), :], sem).start()
        return carry

    if first_row is not None:
        for i in range(first_row, first_row + n_rows):
            issue(i, 0)
    else:
        lax.fori_loop(0, n_rows, issue, 0, unroll=DMA_UNROLL)


def _wait_rows(n_rows, src_hbm, dst, sem):
    pltpu.make_async_copy(src_hbm.at[pl.ds(0, n_rows), :], dst, sem).wait()


def _expert_kernel(be_ref, first_ref, nused_ref, base_ref, tok_ref,
                   h_hbm, wg_ref, wu_ref, wd_ref, o_ref, xbuf, wgb, wub, wdb, sem):
    b = pl.program_id(0)
    n_used = nused_ref[0]
    slot = b % 2

    @pl.when(b == 0)
    def _():
        _gather_rows(tok_ref, base_ref[0], MOE_ROWS, h_hbm, xbuf.at[0], sem.at[0])

    @pl.when(b < n_used)
    def _():
        @pl.when(first_ref[b] == 1)
        def _():
            wgb[...] = wg_ref[...].astype(BF16)
            wub[...] = wu_ref[...].astype(BF16)
            wdb[...] = wd_ref[...].astype(BF16)

        _wait_rows(MOE_ROWS, h_hbm, xbuf.at[slot], sem.at[slot])
        x = xbuf[slot].astype(BF16)

        rows_per_group = MOE_ROWS // EXPERT_PIECES

        def lookahead(piece):
            _gather_rows(tok_ref, base_ref[b + 1], rows_per_group, h_hbm, xbuf.at[1 - slot],
                         sem.at[1 - slot], first_row=piece * rows_per_group)

        half = D_EXPERT // 2
        hid = []
        for n in range(2):
            cols = slice(n * half, (n + 1) * half)
            lookahead(2 * n)
            gate = jnp.dot(x, wgb[:, cols], preferred_element_type=F32)
            lookahead(2 * n + 1)
            up = jnp.dot(x, wub[:, cols], preferred_element_type=F32)
            hid.append((_silu(gate) * up).astype(BF16))
        hid = jnp.concatenate(hid, axis=1)
        n_out = EXPERT_PIECES - 4
        width = D_MODEL // n_out
        for n in range(n_out):
            cols = slice(n * width, (n + 1) * width)
            lookahead(4 + n)
            o_ref[:, cols] = jnp.dot(hid, wdb[:, cols], preferred_element_type=F32)

    @pl.when(b >= n_used)
    def _():
        @pl.when(b == n_used)
        def _():
            _wait_rows(MOE_ROWS, h_hbm, xbuf.at[slot], sem.at[slot])

        o_ref[...] = jnp.zeros_like(o_ref)


def _experts(blk_expert, blk_first, n_used, blk_base, tok_dense, h2, exp_gate, exp_up, exp_down):
    wspec_in = pl.BlockSpec((None, D_MODEL, D_EXPERT), lambda b, be, *_: (be[b], 0, 0))
    wspec_out = pl.BlockSpec((None, D_EXPERT, D_MODEL), lambda b, be, *_: (be[b], 0, 0))
    grid_spec = pltpu.PrefetchScalarGridSpec(
        num_scalar_prefetch=5,
        grid=(MOE_BLOCKS,),
        in_specs=[pl.BlockSpec(memory_space=pl.ANY), wspec_in, wspec_in, wspec_out],
        out_specs=pl.BlockSpec((MOE_ROWS, D_MODEL), lambda b, *_: (b, 0)),
        scratch_shapes=[pltpu.VMEM((2, MOE_ROWS, D_MODEL), F32),
                        pltpu.VMEM((D_MODEL, D_EXPERT), BF16),
                        pltpu.VMEM((D_MODEL, D_EXPERT), BF16),
                        pltpu.VMEM((D_EXPERT, D_MODEL), BF16),
                        pltpu.SemaphoreType.DMA((2,))],
    )
    return pl.pallas_call(
        _expert_kernel,
        grid_spec=grid_spec,
        out_shape=jax.ShapeDtypeStruct((MOE_BLOCKS * MOE_ROWS, D_MODEL), F32),
        compiler_params=_params(("arbitrary",)),
        name="experts",
    )(blk_expert, blk_first, n_used, blk_base, tok_dense, h2, exp_gate, exp_up, exp_down)


def _combine_kernel(pos_ref, y_hbm, wt_ref, h_ref, x1_ref, g2_ref, npost_ref,
                    sg_ref, su_ref, sd_ref, os_ref, op_ref, buf, sem, *, tiles):
    i = pl.program_id(0) * pl.num_programs(1) + pl.program_id(1)
    n_steps = pl.num_programs(0) * pl.num_programs(1)
    slot = i % 2
    tm = tiles.tt
    rows = TOP_K * tm

    @pl.when(i == 0)
    def _():
        _gather_rows(pos_ref, 0, rows, y_hbm, buf.at[0], sem.at[0])

    nxt = jnp.minimum(i + 1, n_steps - 1) * rows
    per_group = rows // COMBINE_PIECES

    def lookahead(piece):
        _gather_rows(pos_ref, nxt, per_group, y_hbm, buf.at[1 - slot], sem.at[1 - slot],
                     first_row=piece * per_group)

    hb = h_ref[...].astype(BF16)
    lookahead(0)
    gate = jnp.dot(hb, sg_ref[...], preferred_element_type=F32)
    lookahead(1)
    up = jnp.dot(hb, su_ref[...], preferred_element_type=F32)
    hid = (_silu(gate) * up).astype(BF16)
    lookahead(2)
    f = jnp.dot(hid, sd_ref[...], preferred_element_type=F32)

    _wait_rows(rows, y_hbm, buf.at[slot], sem.at[slot])
    wt = wt_ref[...]
    for j in range(TOP_K):
        if 3 + j < COMBINE_PIECES:
            lookahead(3 + j)
        f = f + buf[slot, pl.ds(j * tm, tm), :] * wt[:, j:j + 1]
    out = x1_ref[...] + g2_ref[...] * (_rms(f) * npost_ref[...])

    @pl.when(i == n_steps - 1)
    def _():
        _wait_rows(rows, y_hbm, buf.at[1 - slot], sem.at[1 - slot])

    def store(is_sample):
        (os_ref if is_sample else op_ref)[...] = out

    _on_owner(tiles.is_sample(), store)


def _combine(pos_tiles, y_sorted, wsel_tiles, h2, x1, mod3, n_post, sg_bf, su_bf, sd_bf):
    tiles = _ColumnTiles(COMBINE_ROWS)
    tm = tiles.tt
    row = tiles.tokens(D_MODEL)
    grid_spec = pltpu.PrefetchScalarGridSpec(
        num_scalar_prefetch=1,
        grid=tiles.grid,
        in_specs=[pl.BlockSpec(memory_space=pl.ANY),
                  pl.BlockSpec((tm, TOP_K), lambda j, b, p: (j * CHAIN_BATCH + b, 0)),
                  row, row, tiles.mod(5),
                  tiles.const((1, D_MODEL)), tiles.const((D_MODEL, D_EXPERT)),
                  tiles.const((D_MODEL, D_EXPERT)), tiles.const((D_EXPERT, D_MODEL))],
        out_specs=[tiles.sample_major(D_MODEL), tiles.prompt_major(D_MODEL)],
        scratch_shapes=[pltpu.VMEM((2, TOP_K * tm, D_MODEL), F32), pltpu.SemaphoreType.DMA((2,))],
    )
    return pl.pallas_call(
        functools.partial(_combine_kernel, tiles=tiles),
        grid_spec=grid_spec,
        out_shape=[jax.ShapeDtypeStruct((N_SAMPLE // SEQ_SAMPLE, SEQ_SAMPLE, D_MODEL), F32),
                   jax.ShapeDtypeStruct((N_PROMPT // SEQ_PROMPT, SEQ_PROMPT, D_MODEL), F32)],
        compiler_params=_params(("arbitrary", "arbitrary")),
        name="combine",
    )(pos_tiles, y_sorted, wsel_tiles, h2, x1, mod3, n_post, sg_bf, su_bf, sd_bf)


def _pad_rows(w, rows):
    return jnp.pad(w, ((0, rows - w.shape[0]), (0, 0)))


def kernel(x_prompt, x_sample, state_fwd, state_bwd, c, c_ctx, ada_w, ada_b, norm_pre_mix, norm_post_mix, norm_pre_ffn, norm_post_ffn, w_in, conv_w, w_out_conv, decay_w0, decay_w2, iclr_a0, iclr_a2, gate_g2, k_k, k_a, r_k, lnx_w, lnx_b, w_out_rwkv, w_o, router_w, router_bias, exp_gate, exp_up, exp_down, sh_gate, sh_up, sh_down):
    cond = jnp.concatenate([c_ctx[None, :], c, jnp.zeros((N_COND - 1 - c.shape[0], D_MODEL), F32)], axis=0)
    mod3 = _ada_table(cond, ada_w[0], ada_b).reshape(N_COND, 1, 6 * D_MODEL)

    h = _prenorm(x_sample, x_prompt, norm_pre_mix, mod3).reshape(N_TOK, D_MODEL)
    w_in_bf = w_in[0].astype(BF16)
    z = _conv_branch(h, w_in_bf, conv_w[0])
    rkv = _matmul(h, w_in_bf, col0=3 * W_BRANCH, n_cols=3 * W_BRANCH, tn=1024, out_dtype=F32, name="proj_rkv")
    c0 = 6 * W_BRANCH
    pad_cols = lambda lo: jnp.pad(w_in_bf[:, lo:lo + 96], ((0, 0), (0, LORA_PAD - 96)))
    w_lora = jnp.concatenate([pad_cols(c0), pad_cols(c0 + 96), pad_cols(c0 + 192), pad_cols(c0 + 288),
                              w_in_bf[:, c0 + 384:c0 + 640]], axis=1)
    lora = _matmul(h, w_lora, col0=0, n_cols=LORA_W, tn=LORA_W, out_dtype=F32, name="proj_lora")
    gates = _matmul(h, w_in_bf[:, c0 + 640:], col0=0, n_cols=2 * D_MODEL, tn=1024,
                    out_dtype=BF16, act="sigmoid", name="proj_gates")

    wd2 = jnp.stack([_pad_rows(decay_w2[0, 0], LORA_PAD), _pad_rows(decay_w2[0, 1], LORA_PAD)]).astype(BF16)
    wa2 = jnp.stack([_pad_rows(iclr_a2[0, 0], LORA_PAD), _pad_rows(iclr_a2[0, 1], LORA_PAD)]).astype(BF16)
    wa, g = _lora_stage(lora, wd2, wa2, gate_g2[0].astype(BF16), decay_w0[0], iclr_a0[0])

    kk_l = _head_param_to_chain_layout(k_k[0])
    ka_l = _head_param_to_chain_layout(k_a[0])
    rk_l = _head_param_to_chain_layout(r_k[0].reshape(-1))
    lw_l = _head_param_to_chain_layout(lnx_w[0])
    lb_l = _head_param_to_chain_layout(lnx_b[0])

    rkv3 = rkv.reshape(TIME_ROWS, CHAIN_BATCH, 3 * W_BRANCH)
    wa3 = wa.reshape(TIME_ROWS, CHAIN_BATCH, 4 * W_BRANCH)

    def run_scan(row0, seq, s0_f, s0_b):
        out, fin_f, fin_b = _scan(rkv3, wa3, row0, seq, _state_to_chain_layout(s0_f),
                                  _state_to_chain_layout(s0_b), kk_l, ka_l, rk_l, lw_l, lb_l)
        return out.reshape(-1, W_BRANCH), fin_f, fin_b

    zero_state = jnp.zeros((N_PROMPT // SEQ_PROMPT, N_HEADS, HEAD, HEAD), F32)
    yb_s, _, _ = run_scan(0, SEQ_SAMPLE, state_fwd[:, 0], state_bwd[:, 0])
    yb_p, fin_f, fin_b = run_scan(SEQ_SAMPLE, SEQ_PROMPT, zero_state, zero_state)

    merged = _merge(z, yb_s, yb_p, g, gates, w_out_conv[0].astype(BF16), w_out_rwkv[0].astype(BF16))
    x1, h2 = _outproj(merged.reshape(TIME_ROWS, CHAIN_BATCH * D_MODEL), w_o[0].astype(BF16),
                      x_sample, x_prompt, mod3, norm_post_mix, norm_pre_ffn)

    h2_flat = h2.reshape(N_TOK, D_MODEL)
    eidx, wsel, rank, counts = _router(h2_flat, router_w[0].T, router_bias[0][:, None])
    counts = counts[:, 0].astype(I32)
    padded = (counts + MOE_ROWS - 1) // MOE_ROWS * MOE_ROWS
    pad_end = jnp.cumsum(padded)
    pad_start = pad_end - padded
    pos = _slots(pad_start, eidx, rank)
    by_slot = jnp.argsort(pos.reshape(-1)).astype(I32) % N_TOK
    tok_dense = jnp.concatenate([by_slot, jnp.zeros((MOE_ROWS,), I32)])
    blk_start = jnp.arange(MOE_BLOCKS, dtype=I32) * MOE_ROWS
    blk_expert = jnp.minimum(jnp.sum(pad_end[None, :] <= blk_start[:, None], axis=1), N_EXPERTS - 1).astype(I32)
    n_blocks = pad_end[-1] // MOE_ROWS
    n_used = n_blocks.astype(I32).reshape(1)
    blk_first = (blk_start == pad_start[blk_expert]).astype(I32)
    seg_start = jnp.cumsum(counts) - counts
    blk_base = jnp.where(jnp.arange(MOE_BLOCKS) < n_blocks,
                         seg_start[blk_expert] + blk_start - pad_start[blk_expert], 0).astype(I32)
    y_sorted = _experts(blk_expert, blk_first, n_used, blk_base, tok_dense, h2_flat,
                        exp_gate[0], exp_up[0], exp_down[0])

    def tile_order(a, perm):
        a = a.reshape(TOP_K, TIME_ROWS // COMBINE_ROWS, COMBINE_ROWS, CHAIN_BATCH)
        return jnp.transpose(a, perm)

    pos_tiles = tile_order(pos, (1, 3, 0, 2)).reshape(-1)
    wsel_tiles = tile_order(wsel, (1, 3, 2, 0)).reshape(N_TOK, TOP_K)
    out_s, out_p = _combine(pos_tiles, y_sorted, wsel_tiles, h2, x1, mod3, norm_post_ffn,
                            sh_gate[0].astype(BF16), sh_up[0].astype(BF16), sh_down[0].astype(BF16))

    new_f = _state_from_chain_layout(fin_f)[:, None]
    new_b = _state_from_chain_layout(fin_b)[:, None]
    return (out_p, out_s, new_f, new_b)
```
